```python
import math
import jax, jax.numpy as jnp
from jax import lax
import numpy as np

D_MODEL = 1024
BATCH = 8
SEQ = 2048
DEPTH = 2
DEC_BATCH = 128
DEC_SEQ = 1
PAST_LEN = 16384
PAGE_SIZE = 128

N_EVEN = (DEPTH + 1) // 2
N_ODD = DEPTH // 2
D_A = D_MODEL // 2
S5_GROUP = 16
N_GA = D_A // S5_GROUP
P_A = 64
D_B = D_MODEL // 2
N_HB = 4
HD_B = D_B // N_HB
CHUNK = 128
D_IN_AB = D_A + 2 * D_B
D_C = D_MODEL
K_C = 31
D_FF = ((8 * D_MODEL) // 3 + 127) // 128 * 128
K_F = 3
ALPHA = (2.0 * DEPTH) ** 0.25
BETA = (8.0 * DEPTH) ** -0.25
LN_EPS = 1e-5

kernel_name = 'hybrid_s5_gmlp_conformer_step'


def _layer_norm(x, g, b):
    xf = x.astype(jnp.float32)
    mu = jnp.mean(xf, axis=-1, keepdims=True)
    xc = xf - mu
    var = jnp.mean(xc * xc, axis=-1, keepdims=True)
    y = xc * lax.rsqrt(var + LN_EPS) * g.astype(jnp.float32) + b.astype(jnp.float32)
    return y.astype(x.dtype)


def _causal_dwconv(x, cache, w, b):
    k = w.shape[0]
    if cache is None:
        cache = jnp.zeros((x.shape[0], k - 1, x.shape[2]), x.dtype)
    xc = jnp.concatenate([cache.astype(x.dtype), x], axis=1)
    y = lax.conv_general_dilated(xc, w[:, None, :].astype(x.dtype), window_strides=(1,), padding='VALID',
                                 dimension_numbers=('NWC', 'WIO', 'NWC'), feature_group_count=x.shape[2])
    return y + b.astype(x.dtype), xc[:, xc.shape[1] - (k - 1):]


def _scan_combine(e1, e2):
    a1, b1 = e1
    a2, b2 = e2
    return a1 * a2, a2 * b1 + b2


def _s5(u, s0, lam_re, lam_im, log_dt, b_re, b_im, c_re, c_im, d_skip, glu_w, glu_b):
    bn, t, _ = u.shape
    f32 = jnp.float32
    uf = u.astype(f32)
    lam = lax.complex(lam_re.astype(f32), lam_im.astype(f32))
    dt = jnp.exp(log_dt.astype(f32))[:, None]
    lam_bar = jnp.exp(lam * dt)
    b_bar = ((lam_bar - 1.0) / lam)[:, :, None] * lax.complex(b_re.astype(f32), b_im.astype(f32))
    c_mat = lax.complex(c_re.astype(f32), c_im.astype(f32))
    ug = uf.reshape(bn, t, N_GA, S5_GROUP).astype(jnp.complex64)
    bu = jnp.einsum('gpc,btgc->btgp', b_bar, ug)
    if s0 is not None:
        bu = bu.at[:, 0].add(lam_bar[None] * s0)
    a = jnp.broadcast_to(lam_bar[None, None], (1, t, N_GA, P_A))
    _, s = lax.associative_scan(_scan_combine, (a, bu), axis=1)
    y = jnp.real(jnp.einsum('gcp,btgp->btgc', c_mat, s)).reshape(bn, t, D_A) + d_skip.astype(f32) * uf
    g = jax.nn.gelu(y)
    out = g * jax.nn.sigmoid(g @ glu_w.astype(f32) + glu_b.astype(f32))
    return out.astype(u.dtype), s[:, t - 1]


def _spatial_gate(v, w_s, b_s):
    bn, t, _ = v.shape
    l = min(t, CHUNK)
    nc = t // l
    mask = jnp.tril(jnp.ones((l, l), v.dtype))
    w = w_s[:, :l, :l].astype(v.dtype) * mask
    vh = v.reshape(bn, nc, l, N_HB, HD_B)
    out = jnp.einsum('hts,bcshd->bcthd', w, vh) + jnp.transpose(b_s[:, :l]).astype(v.dtype)[:, :, None]
    return out.reshape(bn, t, D_B)


def _even_mixer(x, s0, w_in, lam_re, lam_im, log_dt, b_re, b_im, c_re, c_im, d_skip, glu_w, glu_b,
                sgu_g, sgu_bn, sgu_w, sgu_b, w_out):
    z = x @ w_in
    u_a = z[..., :D_A]
    u_b = z[..., D_A:D_A + D_B]
    v_b = z[..., D_A + D_B:]
    y_a, s_last = _s5(u_a, s0, lam_re, lam_im, log_dt, b_re, b_im, c_re, c_im, d_skip, glu_w, glu_b)
    v_n = _layer_norm(v_b, sgu_g, sgu_bn)
    y_b = u_b * _spatial_gate(v_n, sgu_w, sgu_b)
    out = jnp.concatenate([y_a, y_b], axis=-1) @ w_out
    return out, s_last, v_n


def _odd_mixer(x, cache, w_in, conv_w, conv_b, ln_g, ln_b, w_out):
    z = x @ w_in
    g = z[..., :D_C] * jax.nn.sigmoid(z[..., D_C:])
    h, new_cache = _causal_dwconv(g, cache, conv_w, conv_b)
    h = jax.nn.silu(_layer_norm(h, ln_g, ln_b))
    return h @ w_out, new_cache


def _conv_ffn(x, cache, w_gate, w_up, conv_w, conv_b, w_down):
    gate, new_cache = _causal_dwconv(x @ w_gate, cache, conv_w, conv_b)
    return (jax.nn.silu(gate) * (x @ w_up)) @ w_down, new_cache


def setup_inputs(seed: int = 0) -> dict:
    key = jax.random.key(seed)
    ks = jax.random.split(key, 40)
    f32 = jnp.float32

    def nrm(i, shape, scale):
        return scale * jax.random.normal(ks[i], shape, f32)

    n_idx = jnp.arange(P_A, dtype=f32)
    return {
        'x_prompt': nrm(0, (BATCH, SEQ, D_MODEL), 1.0),
        'x_sample': nrm(1, (DEC_BATCH, DEC_SEQ, D_MODEL), 1.0),
        'state_a_re': nrm(2, (N_EVEN, DEC_BATCH, N_GA, P_A), 0.1),
        'state_a_im': nrm(3, (N_EVEN, DEC_BATCH, N_GA, P_A), 0.1),
        'cache_c_conv': nrm(4, (N_ODD, DEC_BATCH, K_C - 1, D_C), 0.5),
        'cache_ffn_conv': nrm(5, (DEPTH, DEC_BATCH, K_F - 1, D_FF), 1.0),
        'w_in_ab': nrm(6, (N_EVEN, D_MODEL, D_IN_AB), D_MODEL ** -0.5),
        's5_lam_re': -0.5 + nrm(7, (N_EVEN, N_GA, P_A), 0.01),
        's5_lam_im': math.pi * n_idx + nrm(8, (N_EVEN, N_GA, P_A), 0.01),
        's5_log_dt': jax.random.uniform(ks[9], (N_EVEN, N_GA), f32, math.log(1e-3), math.log(1e-1)),
        's5_b_re': nrm(10, (N_EVEN, N_GA, P_A, S5_GROUP), (2.0 * S5_GROUP) ** -0.5),
        's5_b_im': nrm(11, (N_EVEN, N_GA, P_A, S5_GROUP), (2.0 * S5_GROUP) ** -0.5),
        's5_c_re': nrm(12, (N_EVEN, N_GA, S5_GROUP, P_A), (2.0 * P_A) ** -0.5),
        's5_c_im': nrm(13, (N_EVEN, N_GA, S5_GROUP, P_A), (2.0 * P_A) ** -0.5),
        's5_d': nrm(14, (N_EVEN, D_A), 1.0),
        's5_glu_w': nrm(15, (N_EVEN, D_A, D_A), D_A ** -0.5),
        's5_glu_b': nrm(16, (N_EVEN, D_A), 0.02),
        'sgu_ln_g': 1.0 + nrm(17, (N_EVEN, D_B), 0.02),
        'sgu_ln_b': nrm(18, (N_EVEN, D_B), 0.02),
        'sgu_w': nrm(19, (N_EVEN, N_HB, CHUNK, CHUNK), CHUNK ** -0.5),
        'sgu_b': 1.0 + nrm(20, (N_EVEN, N_HB, CHUNK), 0.02),
        'w_out_ab': nrm(21, (N_EVEN, D_A + D_B, D_MODEL), BETA * (D_A + D_B) ** -0.5),
        'w_in_c': nrm(22, (N_ODD, D_MODEL, 2 * D_C), D_MODEL ** -0.5),
        'conv_c_w': nrm(23, (N_ODD, K_C, D_C), K_C ** -0.5),
        'conv_c_b': nrm(24, (N_ODD, D_C), 0.02),
        'ln_c_g': 1.0 + nrm(25, (N_ODD, D_C), 0.02),
        'ln_c_b': nrm(26, (N_ODD, D_C), 0.02),
        'w_out_c': nrm(27, (N_ODD, D_C, D_MODEL), BETA * D_C ** -0.5),
        'ffn_w_gate': nrm(28, (DEPTH, D_MODEL, D_FF), D_MODEL ** -0.5),
        'ffn_w_up': nrm(29, (DEPTH, D_MODEL, D_FF), D_MODEL ** -0.5),
        'ffn_conv_w': nrm(30, (DEPTH, K_F, D_FF), K_F ** -0.5),
        'ffn_conv_b': nrm(31, (DEPTH, D_FF), 0.02),
        'ffn_w_down': nrm(32, (DEPTH, D_FF, D_MODEL), BETA * D_FF ** -0.5),
        'ln_mix_g': 1.0 + nrm(33, (DEPTH, D_MODEL), 0.02),
        'ln_mix_b': nrm(34, (DEPTH, D_MODEL), 0.02),
        'ln_ffn_g': 1.0 + nrm(35, (DEPTH, D_MODEL), 0.02),
        'ln_ffn_b': nrm(36, (DEPTH, D_MODEL), 0.02),
    }


def reference(x_prompt, x_sample, state_a_re, state_a_im, cache_c_conv, cache_ffn_conv,
              w_in_ab, s5_lam_re, s5_lam_im, s5_log_dt, s5_b_re, s5_b_im, s5_c_re, s5_c_im, s5_d,
              s5_glu_w, s5_glu_b, sgu_ln_g, sgu_ln_b, sgu_w, sgu_b, w_out_ab,
              w_in_c, conv_c_w, conv_c_b, ln_c_g, ln_c_b, w_out_c,
              ffn_w_gate, ffn_w_up, ffn_conv_w, ffn_conv_b, ffn_w_down,
              ln_mix_g, ln_mix_b, ln_ffn_g, ln_ffn_b):
    f32 = jnp.float32
    xp, xs = x_prompt, x_sample
    sa_re_p, sa_im_p, sa_re_s, sa_im_s, sb_v_s = [], [], [], [], []
    cc_p, cc_s, cf_p, cf_s = [], [], [], []
    for l in range(DEPTH):
        if l % 2 == 0:
            e = l // 2
            ep = (w_in_ab[e], s5_lam_re[e], s5_lam_im[e], s5_log_dt[e], s5_b_re[e], s5_b_im[e],
                  s5_c_re[e], s5_c_im[e], s5_d[e], s5_glu_w[e], s5_glu_b[e],
                  sgu_ln_g[e], sgu_ln_b[e], sgu_w[e], sgu_b[e], w_out_ab[e])
            s0 = lax.complex(state_a_re[e].astype(f32), state_a_im[e].astype(f32))
            mp, sp_last, _ = _even_mixer(xp, None, *ep)
            ms, ss_last, v_new = _even_mixer(xs, s0, *ep)
            sa_re_p.append(jnp.real(sp_last))
            sa_im_p.append(jnp.imag(sp_last))
            sa_re_s.append(jnp.real(ss_last))
            sa_im_s.append(jnp.imag(ss_last))
            sb_v_s.append(v_new)
        else:
            o = l // 2
            op = (w_in_c[o], conv_c_w[o], conv_c_b[o], ln_c_g[o], ln_c_b[o], w_out_c[o])
            mp, cp = _odd_mixer(xp, None, *op)
            ms, cs = _odd_mixer(xs, cache_c_conv[o], *op)
            cc_p.append(cp)
            cc_s.append(cs)
        xp = _layer_norm(ALPHA * xp + mp, ln_mix_g[l], ln_mix_b[l])
        xs = _layer_norm(ALPHA * xs + ms, ln_mix_g[l], ln_mix_b[l])
        fp_ = (ffn_w_gate[l], ffn_w_up[l], ffn_conv_w[l], ffn_conv_b[l], ffn_w_down[l])
        hp, fcp = _conv_ffn(xp, None, *fp_)
        hs, fcs = _conv_ffn(xs, cache_ffn_conv[l], *fp_)
        cf_p.append(fcp)
        cf_s.append(fcs)
        xp = _layer_norm(ALPHA * xp + hp, ln_ffn_g[l], ln_ffn_b[l])
        xs = _layer_norm(ALPHA * xs + hs, ln_ffn_g[l], ln_ffn_b[l])
    return (xp, xs,
            jnp.stack(sa_re_p), jnp.stack(sa_im_p), jnp.stack(sa_re_s), jnp.stack(sa_im_s),
            jnp.stack(sb_v_s),
            jnp.stack(cc_p), jnp.stack(cc_s),
            jnp.stack(cf_p), jnp.stack(cf_s))
```

```python
import functools
import math

import jax
import jax.numpy as jnp
from jax import lax
from jax.experimental import pallas as pl
from jax.experimental.pallas import tpu as pltpu

D_MODEL = 1024
BATCH = 8
SEQ = 2048
DEPTH = 2
DEC_BATCH = 128
N_EVEN = (DEPTH + 1) // 2
N_ODD = DEPTH // 2
D_A = D_MODEL // 2
S5_GROUP = 16
N_GA = D_A // S5_GROUP
P_A = 64
D_B = D_MODEL // 2
N_HB = 4
HD_B = D_B // N_HB
CHUNK = 128
D_C = D_MODEL
K_C = 31
D_FF = ((8 * D_MODEL) // 3 + 127) // 128 * 128
K_F = 3
ALPHA = (2.0 * DEPTH) ** 0.25
LN_EPS = 1e-5

F32 = jnp.float32
BF16 = jnp.bfloat16

M_PROMPT = BATCH * SEQ
ROWS_PER_CHUNK = CHUNK * BATCH
N_CG = 4
G_PER_CG = N_GA // N_CG
CH_PER_CG = G_PER_CG * S5_GROUP
ST_PER_CG = G_PER_CG * P_A
FF_CHUNKS = ((0, 1536), (1536, D_FF))
VMEM_LIMIT = 56 * 1024 * 1024


def _ln(x, g, b):
    mu = jnp.mean(x, axis=-1, keepdims=True)
    xc = x - mu
    var = jnp.mean(xc * xc, axis=-1, keepdims=True)
    return xc * lax.rsqrt(var + LN_EPS) * g + b


def _dot(a, b):
    return jnp.dot(a, b, preferred_element_type=F32)


def _const_spec(shape):
    nd = len(shape)
    return pl.BlockSpec(shape, lambda *_: (0,) * nd)


def _params(n_grid=1):
    return pltpu.CompilerParams(dimension_semantics=("arbitrary",) * n_grid,
                                vmem_limit_bytes=VMEM_LIMIT)


def _s5_prep_kernel(lr_ref, li_ref, ldt_ref, br_ref, bi_ref, ci_ref,
                    lbr_ref, lbi_ref, bbr_ref, bbi_ref, cneg_ref):
    lr = lr_ref[...]
    li = li_ref[...]
    dt = jnp.exp(ldt_ref[...])
    mag = jnp.exp(lr * dt)
    lbr = mag * jnp.cos(li * dt)
    lbi = mag * jnp.sin(li * dt)
    lbr_ref[...] = lbr
    lbi_ref[...] = lbi
    nr = lbr - 1.0
    ni = lbi
    den = lr * lr + li * li
    qr = (nr * lr + ni * li) / den
    qi = (ni * lr - nr * li) / den
    br = br_ref[...]
    bi = bi_ref[...]
    qr3 = qr[:, None, :]
    qi3 = qi[:, None, :]
    bbr_ref[...] = qr3 * br - qi3 * bi
    bbi_ref[...] = qr3 * bi + qi3 * br
    cneg_ref[...] = -ci_ref[...]


def _s5_prep(lam_re, lam_im, log_dt, b_re, b_im, c_im):
    gp = jax.ShapeDtypeStruct((N_GA, P_A), F32)
    gcp = jax.ShapeDtypeStruct((N_GA, S5_GROUP, P_A), F32)
    return pl.pallas_call(
        _s5_prep_kernel,
        out_shape=(gp, gp, gcp, gcp, gcp),
        name="s5_prep",
    )(lam_re, lam_im, log_dt.reshape(N_GA, 1), b_re, b_im, c_im)


def _block_diag_params(lbr, lbi, bbr, bbi, c_re, cneg):
    eye = jnp.eye(G_PER_CG, dtype=F32)

    def in_mat(b):
        b4 = b.reshape(N_CG, G_PER_CG, S5_GROUP, P_A)
        return jnp.einsum('jgcp,gh->jgchp', b4, eye).reshape(N_CG, CH_PER_CG, ST_PER_CG)

    def out_mat(c):
        c4 = c.reshape(N_CG, G_PER_CG, S5_GROUP, P_A)
        return jnp.einsum('jgcp,gh->jgphc', c4, eye).reshape(N_CG, ST_PER_CG, CH_PER_CG)

    bbd = jnp.concatenate([in_mat(bbr), in_mat(bbi)], axis=2).astype(BF16)
    cbd = jnp.concatenate([out_mat(c_re), out_mat(cneg)], axis=1).astype(BF16)
    lam = jnp.stack([lbr.reshape(-1), lbi.reshape(-1)])
    return bbd, cbd, lam


def _even_prompt_kernel(x_ref, win_ref, bbd_ref, cbd_ref, lam_ref, d_ref, gluw_ref, glub_ref,
                        sg_ref, sb_ref, sw_ref, sbt_ref, wout_ref, lng_ref, lnb_ref,
                        o_ref, sre_ref, sim_ref,
                        st_ref, scr_ref, cat_ref, vn_ref, gt_ref):
    @pl.when(pl.program_id(0) == 0)
    def _():
        st_ref[...] = jnp.zeros_like(st_ref)

    xb = x_ref[...]
    xbf = xb.astype(BF16)

    ua = _dot(xbf, win_ref[:, 0:D_A])
    uab = ua.astype(BF16)
    ys = []
    for j in range(N_CG):
        re_cols = slice(2 * ST_PER_CG * j, 2 * ST_PER_CG * j + ST_PER_CG)
        im_cols = slice(2 * ST_PER_CG * j + ST_PER_CG, 2 * ST_PER_CG * (j + 1))
        scr_ref[...] = _dot(uab[:, CH_PER_CG * j:CH_PER_CG * (j + 1)], bbd_ref[j])
        lr = jnp.broadcast_to(lam_ref[0:1, ST_PER_CG * j:ST_PER_CG * (j + 1)], (BATCH, ST_PER_CG))
        li = jnp.broadcast_to(lam_ref[1:2, ST_PER_CG * j:ST_PER_CG * (j + 1)], (BATCH, ST_PER_CG))

        def step(t, carry, lr=lr, li=li):
            sr, si = carry
            row = pl.multiple_of(t * BATCH, BATCH)
            nr = lr * sr - li * si + scr_ref[pl.ds(row, BATCH), 0:ST_PER_CG]
            ni = lr * si + li * sr + scr_ref[pl.ds(row, BATCH), ST_PER_CG:2 * ST_PER_CG]
            scr_ref[pl.ds(row, BATCH), 0:ST_PER_CG] = nr
            scr_ref[pl.ds(row, BATCH), ST_PER_CG:2 * ST_PER_CG] = ni
            return nr, ni

        sr, si = lax.fori_loop(0, CHUNK, step, (st_ref[:, re_cols], st_ref[:, im_cols]), unroll=4)
        st_ref[:, re_cols] = sr
        st_ref[:, im_cols] = si
        sre_ref[:, ST_PER_CG * j:ST_PER_CG * (j + 1)] = sr
        sim_ref[:, ST_PER_CG * j:ST_PER_CG * (j + 1)] = si
        ys.append(_dot(scr_ref[...].astype(BF16), cbd_ref[j]))
    y = jnp.concatenate(ys, axis=-1) + d_ref[...] * ua
    g = jax.nn.gelu(y, approximate=True)
    ya = g * jax.nn.sigmoid(_dot(g.astype(BF16), gluw_ref[...]) + glub_ref[...])
    cat_ref[:, 0:D_A] = ya.astype(BF16)

    ub = _dot(xbf, win_ref[:, D_A:D_A + D_B])
    vb = _dot(xbf, win_ref[:, D_A + D_B:D_A + 2 * D_B])
    vn = _ln(vb, sg_ref[...], sb_ref[...])
    for h in range(N_HB):
        vn_ref[h] = vn[:, HD_B * h:HD_B * (h + 1)]
    r_id = lax.broadcasted_iota(jnp.int32, (CHUNK, CHUNK), 0)
    c_id = lax.broadcasted_iota(jnp.int32, (CHUNK, CHUNK), 1)
    tril = (c_id <= r_id).astype(F32)
    wm = [(sw_ref[h] * tril).astype(BF16) for h in range(N_HB)]
    for b in range(BATCH):
        for h in range(N_HB):
            v_bh = vn_ref.at[h][pl.ds(b, CHUNK, stride=BATCH), :].astype(BF16)
            gt_ref.at[h][pl.ds(b, CHUNK, stride=BATCH), :] = _dot(wm[h], v_bh) + sbt_ref[:, h:h + 1]
    gate = jnp.concatenate([gt_ref[h] for h in range(N_HB)], axis=-1)
    cat_ref[:, D_A:] = (ub * gate).astype(BF16)

    out = _dot(cat_ref[...], wout_ref[...])
    o_ref[...] = _ln(ALPHA * xb + out, lng_ref[...], lnb_ref[...])


def _even_prompt(x, win, bbd, cbd, lam, d, gluw, glub, sg, sb, sw, sbt, wout, lng, lnb):
    rows = ROWS_PER_CHUNK
    n = M_PROMPT // rows
    consts = (win, bbd, cbd, lam, d, gluw, glub, sg, sb, sw, sbt, wout, lng, lnb)
    return pl.pallas_call(
        _even_prompt_kernel,
        grid=(n,),
        in_specs=[pl.BlockSpec((rows, D_MODEL), lambda i: (i, 0))] + [_const_spec(c.shape) for c in consts],
        out_specs=(pl.BlockSpec((rows, D_MODEL), lambda i: (i, 0)),
                   _const_spec((BATCH, N_GA * P_A)), _const_spec((BATCH, N_GA * P_A))),
        out_shape=(jax.ShapeDtypeStruct((M_PROMPT, D_MODEL), F32),
                   jax.ShapeDtypeStruct((BATCH, N_GA * P_A), F32),
                   jax.ShapeDtypeStruct((BATCH, N_GA * P_A), F32)),
        scratch_shapes=[pltpu.VMEM((BATCH, 2 * N_GA * P_A), F32),
                        pltpu.VMEM((rows, 2 * ST_PER_CG), F32),
                        pltpu.VMEM((rows, D_MODEL), BF16),
                        pltpu.VMEM((N_HB, rows, HD_B), F32),
                        pltpu.VMEM((N_HB, rows, HD_B), F32)],
        compiler_params=_params(),
        name="even_prompt",
    )(x, *consts)


def _even_sample_kernel(x_ref, s0r_ref, s0i_ref, win_ref, bbd_ref, cbd_ref, lam_ref, d_ref,
                        gluw_ref, glub_ref, sg_ref, sb_ref, sw0_ref, sb0_ref, wout_ref, lng_ref, lnb_ref,
                        o_ref, sre_ref, sim_ref, vn_ref):
    xb = x_ref[...]
    xbf = xb.astype(BF16)
    ua = _dot(xbf, win_ref[:, 0:D_A])
    uab = ua.astype(BF16)
    ys = []
    for j in range(N_CG):
        cols = slice(ST_PER_CG * j, ST_PER_CG * (j + 1))
        bu = _dot(uab[:, CH_PER_CG * j:CH_PER_CG * (j + 1)], bbd_ref[j])
        lr = lam_ref[0:1, cols]
        li = lam_ref[1:2, cols]
        s0r = s0r_ref[:, cols]
        s0i = s0i_ref[:, cols]
        sr = lr * s0r - li * s0i + bu[:, 0:ST_PER_CG]
        si = lr * s0i + li * s0r + bu[:, ST_PER_CG:]
        sre_ref[:, cols] = sr
        sim_ref[:, cols] = si
        ys.append(_dot(jnp.concatenate([sr, si], axis=-1).astype(BF16), cbd_ref[j]))
    y = jnp.concatenate(ys, axis=-1) + d_ref[...] * ua
    g = jax.nn.gelu(y, approximate=True)
    ya = g * jax.nn.sigmoid(_dot(g.astype(BF16), gluw_ref[...]) + glub_ref[...])

    ub = _dot(xbf, win_ref[:, D_A:D_A + D_B])
    vb = _dot(xbf, win_ref[:, D_A + D_B:D_A + 2 * D_B])
    vn = _ln(vb, sg_ref[...], sb_ref[...])
    vn_ref[...] = vn
    yb = ub * (sw0_ref[...] * vn + sb0_ref[...])

    cat = jnp.concatenate([ya, yb], axis=-1).astype(BF16)
    out = _dot(cat, wout_ref[...])
    o_ref[...] = _ln(ALPHA * xb + out, lng_ref[...], lnb_ref[...])


def _even_sample(x, s0r, s0i, win, bbd, cbd, lam, d, gluw, glub, sg, sb, sw0, sb0, wout, lng, lnb):
    ins = (x, s0r, s0i, win, bbd, cbd, lam, d, gluw, glub, sg, sb, sw0, sb0, wout, lng, lnb)
    st = jax.ShapeDtypeStruct((DEC_BATCH, N_GA * P_A), F32)
    return pl.pallas_call(
        _even_sample_kernel,
        out_shape=(jax.ShapeDtypeStruct((DEC_BATCH, D_MODEL), F32), st, st,
                   jax.ShapeDtypeStruct((DEC_BATCH, D_B), F32)),
        compiler_params=pltpu.CompilerParams(vmem_limit_bytes=VMEM_LIMIT),
        name="even_sample",
    )(*ins)


def _ffn_kernel(x_ref, halo_ref, wg_ref, wu_ref, cw_ref, cb_ref, wd_ref, lng_ref, lnb_ref,
                o_ref, cache_ref, gs_ref, *, tm, shift):
    hs = (K_F - 1) * shift

    @pl.when(pl.program_id(0) == 0)
    def _():
        gs_ref[0:hs, :] = halo_ref[...]

    xb = x_ref[...]
    xbf = xb.astype(BF16)
    acc = None
    for c0, c1 in FF_CHUNKS:
        gs_ref[hs:hs + tm, c0:c1] = _dot(xbf, wg_ref[:, c0:c1])
        conv = cb_ref[:, c0:c1]
        for k in range(K_F):
            conv = conv + cw_ref[k:k + 1, c0:c1] * gs_ref[k * shift:k * shift + tm, c0:c1]
        up = _dot(xbf, wu_ref[:, c0:c1])
        h = (jax.nn.silu(conv) * up).astype(BF16)
        part = _dot(h, wd_ref[c0:c1, :])
        acc = part if acc is None else acc + part
    o_ref[...] = _ln(ALPHA * xb + acc, lng_ref[...], lnb_ref[...])
    tail = gs_ref[tm:tm + hs, :]
    cache_ref[...] = tail
    gs_ref[0:hs, :] = tail


def _ffn(x, halo, wg, wu, cw, cb, wd, lng, lnb, *, tm, shift):
    m = x.shape[0]
    hs = (K_F - 1) * shift
    consts = (halo, wg, wu, cw, cb, wd, lng, lnb)
    return pl.pallas_call(
        functools.partial(_ffn_kernel, tm=tm, shift=shift),
        grid=(m // tm,),
        in_specs=[pl.BlockSpec((tm, D_MODEL), lambda i: (i, 0))] + [_const_spec(c.shape) for c in consts],
        out_specs=(pl.BlockSpec((tm, D_MODEL), lambda i: (i, 0)), _const_spec((hs, D_FF))),
        out_shape=(jax.ShapeDtypeStruct((m, D_MODEL), F32), jax.ShapeDtypeStruct((hs, D_FF), F32)),
        scratch_shapes=[pltpu.VMEM((hs + tm, D_FF), F32)],
        compiler_params=_params(),
        name="conv_ffn",
    )(x, *consts)


def _odd_prompt_kernel(x_ref, win_ref, cw_ref, cb_ref, lcg_ref, lcb_ref, wout_ref, lng_ref, lnb_ref,
                       o_ref, cache_ref, gs_ref, hc_ref, *, tm):
    hs = (K_C - 1) * BATCH
    rg = 8 * BATCH

    @pl.when(pl.program_id(0) == 0)
    def _():
        gs_ref[0:hs, :] = jnp.zeros((hs, D_C), F32)

    xb = x_ref[...]
    xbf = xb.astype(BF16)
    z1 = _dot(xbf, win_ref[:, 0:D_C])
    z2 = _dot(xbf, win_ref[:, D_C:2 * D_C])
    gs_ref[hs:hs + tm, :] = z1 * jax.nn.sigmoid(z2)

    def conv_rows(i, _):
        r0 = pl.multiple_of(i * rg, rg)
        for l in range(D_C // 128):
            cols = slice(128 * l, 128 * (l + 1))
            win = gs_ref[pl.ds(r0, rg + hs), cols]
            acc = jnp.broadcast_to(cb_ref[:, cols], (rg, 128))
            for k in range(K_C):
                acc = acc + cw_ref[k:k + 1, cols] * win[k * BATCH:k * BATCH + rg, :]
            hc_ref[pl.ds(r0, rg), cols] = acc
        return 0

    lax.fori_loop(0, tm // rg, conv_rows, 0)
    h = jax.nn.silu(_ln(hc_ref[...], lcg_ref[...], lcb_ref[...]))
    out = _dot(h.astype(BF16), wout_ref[...])
    o_ref[...] = _ln(ALPHA * xb + out, lng_ref[...], lnb_ref[...])
    tail = gs_ref[tm:tm + hs, :]
    cache_ref[...] = tail
    gs_ref[0:hs, :] = tail


def _odd_prompt(x, win, cw, cb, lcg, lcb, wout, lng, lnb, *, tm):
    hs = (K_C - 1) * BATCH
    consts = (win, cw, cb, lcg, lcb, wout, lng, lnb)
    return pl.pallas_call(
        functools.partial(_odd_prompt_kernel, tm=tm),
        grid=(M_PROMPT // tm,),
        in_specs=[pl.BlockSpec((tm, D_MODEL), lambda i: (i, 0))] + [_const_spec(c.shape) for c in consts],
        out_specs=(pl.BlockSpec((tm, D_MODEL), lambda i: (i, 0)), _const_spec((hs, D_C))),
        out_shape=(jax.ShapeDtypeStruct((M_PROMPT, D_MODEL), F32), jax.ShapeDtypeStruct((hs, D_C), F32)),
        scratch_shapes=[pltpu.VMEM((hs + tm, D_C), F32), pltpu.VMEM((tm, D_C), F32)],
        compiler_params=_params(),
        name="odd_prompt",
    )(x, *consts)


def _cache_conv_kernel(c_ref, cw_ref, o_ref):
    o_ref[...] = jnp.sum(c_ref[...] * cw_ref[0:K_C - 1, :][None], axis=1)


def _cache_conv(cache, cw, *, bc=32):
    return pl.pallas_call(
        _cache_conv_kernel,
        grid=(DEC_BATCH // bc,),
        in_specs=[pl.BlockSpec((bc, K_C - 1, D_C), lambda i: (i, 0, 0)), _const_spec(cw.shape)],
        out_specs=pl.BlockSpec((bc, D_C), lambda i: (i, 0)),
        out_shape=jax.ShapeDtypeStruct((DEC_BATCH, D_C), F32),
        compiler_params=_params(),
        name="odd_sample_cache_conv",
    )(cache, cw)


def _odd_sample_kernel(x_ref, pc_ref, win_ref, cw_ref, cb_ref, lcg_ref, lcb_ref, wout_ref, lng_ref, lnb_ref,
                       o_ref, g_ref):
    xb = x_ref[...]
    xbf = xb.astype(BF16)
    z1 = _dot(xbf, win_ref[:, 0:D_C])
    z2 = _dot(xbf, win_ref[:, D_C:2 * D_C])
    g = z1 * jax.nn.sigmoid(z2)
    g_ref[...] = g
    hc = pc_ref[...] + cw_ref[K_C - 1:K_C, :] * g + cb_ref[...]
    h = jax.nn.silu(_ln(hc, lcg_ref[...], lcb_ref[...]))
    out = _dot(h.astype(BF16), wout_ref[...])
    o_ref[...] = _ln(ALPHA * xb + out, lng_ref[...], lnb_ref[...])


def _odd_sample(x, pc, win, cw, cb, lcg, lcb, wout, lng, lnb):
    o = jax.ShapeDtypeStruct((DEC_BATCH, D_MODEL), F32)
    return pl.pallas_call(
        _odd_sample_kernel,
        out_shape=(o, jax.ShapeDtypeStruct((DEC_BATCH, D_C), F32)),
        compiler_params=pltpu.CompilerParams(vmem_limit_bytes=VMEM_LIMIT),
        name="odd_sample",
    )(x, pc, win, cw, cb, lcg, lcb, wout, lng, lnb)


def kernel(x_prompt, x_sample, state_a_re, state_a_im, cache_c_conv, cache_ffn_conv, w_in_ab, s5_lam_re, s5_lam_im, s5_log_dt, s5_b_re, s5_b_im, s5_c_re, s5_c_im, s5_d, s5_glu_w, s5_glu_b, sgu_ln_g, sgu_ln_b, sgu_w, sgu_b, w_out_ab, w_in_c, conv_c_w, conv_c_b, ln_c_g, ln_c_b, w_out_c, ffn_w_gate, ffn_w_up, ffn_conv_w, ffn_conv_b, ffn_w_down, ln_mix_g, ln_mix_b, ln_ffn_g, ln_ffn_b):
    row = lambda v: v.reshape(1, -1)
    xp = jnp.transpose(x_prompt, (1, 0, 2)).reshape(M_PROMPT, D_MODEL)
    xs = x_sample.reshape(DEC_BATCH, D_MODEL)

    sa_re_p, sa_im_p, sa_re_s, sa_im_s, sb_v_s = [], [], [], [], []
    cc_p, cc_s, cf_p, cf_s = [], [], [], []
    for l in range(DEPTH):
        lng, lnb = row(ln_mix_g[l]), row(ln_mix_b[l])
        if l % 2 == 0:
            e = l // 2
            lbr, lbi, bbr, bbi, cneg = _s5_prep(
                s5_lam_re[e], s5_lam_im[e], s5_log_dt[e],
                jnp.swapaxes(s5_b_re[e], 1, 2), jnp.swapaxes(s5_b_im[e], 1, 2), s5_c_im[e])
            bbd, cbd, lam = _block_diag_params(lbr, lbi, bbr, bbi, s5_c_re[e], cneg)
            win = w_in_ab[e].astype(BF16)
            wout = w_out_ab[e].astype(BF16)
            gluw = s5_glu_w[e].astype(BF16)
            shared = (win, bbd, cbd, lam, row(s5_d[e]), gluw, row(s5_glu_b[e]),
                      row(sgu_ln_g[e]), row(sgu_ln_b[e]))
            xp, sre, sim = _even_prompt(xp, *shared, sgu_w[e], jnp.transpose(sgu_b[e]), wout, lng, lnb)
            sw0 = row(jnp.repeat(sgu_w[e][:, 0, 0], HD_B))
            sb0 = row(jnp.repeat(sgu_b[e][:, 0], HD_B))
            xs, sres, sims, vn = _even_sample(
                xs, state_a_re[e].reshape(DEC_BATCH, -1), state_a_im[e].reshape(DEC_BATCH, -1),
                *shared, sw0, sb0, wout, lng, lnb)
            sa_re_p.append(sre.reshape(BATCH, N_GA, P_A))
            sa_im_p.append(sim.reshape(BATCH, N_GA, P_A))
            sa_re_s.append(sres.reshape(DEC_BATCH, N_GA, P_A))
            sa_im_s.append(sims.reshape(DEC_BATCH, N_GA, P_A))
            sb_v_s.append(vn.reshape(DEC_BATCH, 1, D_B))
        else:
            o = l // 2
            win = w_in_c[o].astype(BF16)
            wout = w_out_c[o].astype(BF16)
            rest = (row(conv_c_b[o]), row(ln_c_g[o]), row(ln_c_b[o]), wout, lng, lnb)
            xp, cp = _odd_prompt(xp, win, conv_c_w[o], *rest, tm=512)
            pc = _cache_conv(cache_c_conv[o], conv_c_w[o])
            xs, g = _odd_sample(xs, pc, win, conv_c_w[o], *rest)
            cc_p.append(jnp.transpose(cp.reshape(K_C - 1, BATCH, D_C), (1, 0, 2)))
            cc_s.append(jnp.concatenate([cache_c_conv[o][:, 1:], g[:, None, :]], axis=1))
        wg = ffn_w_gate[l].astype(BF16)
        wu = ffn_w_up[l].astype(BF16)
        wd = ffn_w_down[l].astype(BF16)
        frest = (wg, wu, ffn_conv_w[l], row(ffn_conv_b[l]), wd, row(ln_ffn_g[l]), row(ln_ffn_b[l]))
        xp, fcp = _ffn(xp, jnp.zeros(((K_F - 1) * BATCH, D_FF), F32), *frest, tm=256, shift=BATCH)
        halo_s = jnp.transpose(cache_ffn_conv[l], (1, 0, 2)).reshape((K_F - 1) * DEC_BATCH, D_FF)
        xs, fcs = _ffn(xs, halo_s, *frest, tm=DEC_BATCH, shift=DEC_BATCH)
        cf_p.append(jnp.transpose(fcp.reshape(K_F - 1, BATCH, D_FF), (1, 0, 2)))
        cf_s.append(jnp.transpose(fcs.reshape(K_F - 1, DEC_BATCH, D_FF), (1, 0, 2)))

    yp = jnp.transpose(xp.reshape(SEQ, BATCH, D_MODEL), (1, 0, 2))
    ys = xs.reshape(DEC_BATCH, 1, D_MODEL)
    return (yp, ys,
            jnp.stack(sa_re_p), jnp.stack(sa_im_p), jnp.stack(sa_re_s), jnp.stack(sa_im_s),
            jnp.stack(sb_v_s),
            jnp.stack(cc_p), jnp.stack(cc_s),
            jnp.stack(cf_p), jnp.stack(cf_s))
```

```python
import functools

import jax
import jax.numpy as jnp
from jax import lax
from jax.experimental import pallas as pl
from jax.experimental.pallas import tpu as pltpu

D_MODEL = 1024
BATCH = 8
SEQ = 2048
DEPTH = 2
DEC_BATCH = 128
N_EVEN = (DEPTH + 1) // 2
N_ODD = DEPTH // 2
D_A = D_MODEL // 2
S5_GROUP = 16
N_GA = D_A // S5_GROUP
P_A = 64
D_B = D_MODEL // 2
N_HB = 4
HD_B = D_B // N_HB
CHUNK = 128
D_C = D_MODEL
K_C = 31
D_FF = ((8 * D_MODEL) // 3 + 127) // 128 * 128
K_F = 3
ALPHA = (2.0 * DEPTH) ** 0.25
LN_EPS = 1e-5

F32 = jnp.float32
BF16 = jnp.bfloat16

LANES = 128
MXU_N = 256
M_PROMPT = BATCH * SEQ
ROWS_PER_CHUNK = CHUNK * BATCH
N_CG = 4
G_PER_CG = N_GA // N_CG
CH_PER_CG = G_PER_CG * S5_GROUP
ST_PER_CG = G_PER_CG * P_A
FF_CHUNKS = ((0, 6 * MXU_N), (6 * MXU_N, D_FF))
FFN_TM = 512
ODD_TM = 512
CONV_ROWS = 8 * BATCH
VMEM_LIMIT = 56 * 1024 * 1024


def _ln(x, g, b):
    mu = jnp.mean(x, axis=-1, keepdims=True)
    xc = x - mu
    var = jnp.mean(xc * xc, axis=-1, keepdims=True)
    return xc * lax.rsqrt(var + LN_EPS) * g + b


def _dot(a, b):
    return jnp.dot(a, b, preferred_element_type=F32)


def _whole(arr):
    nd = arr.ndim
    return pl.BlockSpec(arr.shape, lambda *_: (0,) * nd, pipeline_mode=pl.Buffered(1))


def _layer(arr, l):
    nd = arr.ndim
    return pl.BlockSpec((None,) + arr.shape[1:], lambda *_: (l,) + (0,) * (nd - 1),
                        pipeline_mode=pl.Buffered(1))


def _rows3(v):
    return v.reshape(v.shape[0], 1, v.shape[1])


def _params(n_grid=1):
    return pltpu.CompilerParams(dimension_semantics=("arbitrary",) * n_grid,
                                vmem_limit_bytes=VMEM_LIMIT)


def _s5_prep_kernel(lr_ref, li_ref, ldt_ref, br_ref, bi_ref, ci_ref,
                    lbr_ref, lbi_ref, bbr_ref, bbi_ref, cneg_ref):
    lr = lr_ref[...]
    li = li_ref[...]
    dt = jnp.exp(ldt_ref[...])
    mag = jnp.exp(lr * dt)
    lbr = mag * jnp.cos(li * dt)
    lbi = mag * jnp.sin(li * dt)
    lbr_ref[...] = lbr
    lbi_ref[...] = lbi
    nr = lbr - 1.0
    ni = lbi
    den = lr * lr + li * li
    qr = (nr * lr + ni * li) / den
    qi = (ni * lr - nr * li) / den
    br = br_ref[...]
    bi = bi_ref[...]
    qr3 = qr[:, None, :]
    qi3 = qi[:, None, :]
    bbr_ref[...] = qr3 * br - qi3 * bi
    bbi_ref[...] = qr3 * bi + qi3 * br
    cneg_ref[...] = -ci_ref[...]


def _s5_prep(lam_re, lam_im, log_dt, b_re, b_im, c_im):
    gp = jax.ShapeDtypeStruct((N_GA, P_A), F32)
    gcp = jax.ShapeDtypeStruct((N_GA, S5_GROUP, P_A), F32)
    return pl.pallas_call(
        _s5_prep_kernel,
        out_shape=(gp, gp, gcp, gcp, gcp),
        name="s5_prep",
    )(lam_re, lam_im, log_dt.reshape(N_GA, 1), b_re, b_im, c_im)


def _block_diag_params(lbr, lbi, bbr, bbi, c_re, cneg):
    eye = jnp.eye(G_PER_CG, dtype=F32)

    def in_mat(b):
        b4 = b.reshape(N_CG, G_PER_CG, S5_GROUP, P_A)
        return jnp.einsum('jgcp,gh->jgchp', b4, eye).reshape(N_CG, CH_PER_CG, ST_PER_CG)

    def out_mat(c):
        c4 = c.reshape(N_CG, G_PER_CG, S5_GROUP, P_A)
        return jnp.einsum('jgcp,gh->jgphc', c4, eye).reshape(N_CG, ST_PER_CG, CH_PER_CG)

    bbd = jnp.concatenate([in_mat(bbr), in_mat(bbi)], axis=2).astype(BF16)
    cbd = jnp.concatenate([out_mat(c_re), out_mat(cneg)], axis=1).astype(BF16)
    lam = jnp.stack([lbr.reshape(-1), lbi.reshape(-1)])
    return bbd, cbd, lam


def _even_prompt_kernel(x_ref, win_ref, bbd_ref, cbd_ref, lam_ref, d_ref, gluw_ref, glub_ref,
                        sg_ref, sb_ref, sw_ref, sbt_ref, wout_ref, lng_ref, lnb_ref,
                        o_ref, sre_ref, sim_ref,
                        st_ref, xt_ref, scr_ref, cat_ref, vn_ref, gt_ref):
    @pl.when(pl.program_id(0) == 0)
    def _():
        st_ref[...] = jnp.zeros_like(st_ref)

    for c in range(D_MODEL // LANES):
        for b in range(BATCH):
            xt_ref.at[c][pl.ds(b, CHUNK, stride=BATCH), :] = x_ref[b, :, LANES * c:LANES * (c + 1)]
    xb = jnp.concatenate([xt_ref[c] for c in range(D_MODEL // LANES)], axis=-1)
    xbf = xb.astype(BF16)

    ua = _dot(xbf, win_ref[:, 0:D_A])
    uab = ua.astype(BF16)
    ys = []
    for j in range(N_CG):
        re_cols = slice(2 * ST_PER_CG * j, 2 * ST_PER_CG * j + ST_PER_CG)
        im_cols = slice(2 * ST_PER_CG * j + ST_PER_CG, 2 * ST_PER_CG * (j + 1))
        scr_ref[...] = _dot(uab[:, CH_PER_CG * j:CH_PER_CG * (j + 1)], bbd_ref[j])
        lr = jnp.broadcast_to(lam_ref[0:1, ST_PER_CG * j:ST_PER_CG * (j + 1)], (BATCH, ST_PER_CG))
        li = jnp.broadcast_to(lam_ref[1:2, ST_PER_CG * j:ST_PER_CG * (j + 1)], (BATCH, ST_PER_CG))

        def step(t, carry, lr=lr, li=li):
            sr, si = carry
            row = pl.multiple_of(t * BATCH, BATCH)
            nr = lr * sr - li * si + scr_ref[pl.ds(row, BATCH), 0:ST_PER_CG]
            ni = lr * si + li * sr + scr_ref[pl.ds(row, BATCH), ST_PER_CG:2 * ST_PER_CG]
            scr_ref[pl.ds(row, BATCH), 0:ST_PER_CG] = nr
            scr_ref[pl.ds(row, BATCH), ST_PER_CG:2 * ST_PER_CG] = ni
            return nr, ni

        sr, si = lax.fori_loop(0, CHUNK, step, (st_ref[:, re_cols], st_ref[:, im_cols]), unroll=4)
        st_ref[:, re_cols] = sr
        st_ref[:, im_cols] = si
        sre_ref[:, ST_PER_CG * j:ST_PER_CG * (j + 1)] = sr
        sim_ref[:, ST_PER_CG * j:ST_PER_CG * (j + 1)] = si
        ys.append(_dot(scr_ref[...].astype(BF16), cbd_ref[j]))
    y = jnp.concatenate(ys, axis=-1) + d_ref[...] * ua
    g = jax.nn.gelu(y, approximate=True)
    ya = g * jax.nn.sigmoid(_dot(g.astype(BF16), gluw_ref[...]) + glub_ref[...])
    cat_ref[:, 0:D_A] = ya.astype(BF16)

    ub = _dot(xbf, win_ref[:, D_A:D_A + D_B])
    vb = _dot(xbf, win_ref[:, D_A + D_B:D_A + 2 * D_B])
    vn = _ln(vb, sg_ref[...], sb_ref[...])
    for h in range(N_HB):
        vn_ref[h] = vn[:, HD_B * h:HD_B * (h + 1)]
    r_id = lax.broadcasted_iota(jnp.int32, (CHUNK, CHUNK), 0)
    c_id = lax.broadcasted_iota(jnp.int32, (CHUNK, CHUNK), 1)
    tril = (c_id <= r_id).astype(F32)
    wm = [(sw_ref[h] * tril).astype(BF16) for h in range(N_HB)]
    for b in range(BATCH):
        for h in range(N_HB):
            v_bh = vn_ref.at[h][pl.ds(b, CHUNK, stride=BATCH), :].astype(BF16)
            gt_ref.at[h][pl.ds(b, CHUNK, stride=BATCH), :] = _dot(wm[h], v_bh) + sbt_ref[:, h:h + 1]
    gate = jnp.concatenate([gt_ref[h] for h in range(N_HB)], axis=-1)
    cat_ref[:, D_A:] = (ub * gate).astype(BF16)

    out = _dot(cat_ref[...], wout_ref[...])
    o_ref[...] = _ln(ALPHA * xb + out, lng_ref[...], lnb_ref[...])


def _even_prompt(x, e, l, win, bbd, cbd, lam, d3, gluw, glub3, sg3, sb3, sw, sbt, wout, lng3, lnb3):
    rows = ROWS_PER_CHUNK
    n = SEQ // CHUNK
    in_specs = [pl.BlockSpec((BATCH, CHUNK, D_MODEL), lambda i: (0, i, 0)),
                _layer(win, e), _whole(bbd), _whole(cbd), _whole(lam), _layer(d3, e), _layer(gluw, e),
                _layer(glub3, e), _layer(sg3, e), _layer(sb3, e), _layer(sw, e), _layer(sbt, e),
                _layer(wout, e), _layer(lng3, l), _layer(lnb3, l)]
    st = jax.ShapeDtypeStruct((BATCH, N_GA * P_A), F32)
    st_spec = pl.BlockSpec((BATCH, N_GA * P_A), lambda i: (0, 0))
    return pl.pallas_call(
        _even_prompt_kernel,
        grid=(n,),
        in_specs=in_specs,
        out_specs=(pl.BlockSpec((rows, D_MODEL), lambda i: (i, 0)), st_spec, st_spec),
        out_shape=(jax.ShapeDtypeStruct((M_PROMPT, D_MODEL), F32), st, st),
        scratch_shapes=[pltpu.VMEM((BATCH, 2 * N_GA * P_A), F32),
                        pltpu.VMEM((D_MODEL // LANES, rows, LANES), F32),
                        pltpu.VMEM((rows, 2 * ST_PER_CG), F32),
                        pltpu.VMEM((rows, D_MODEL), BF16),
                        pltpu.VMEM((N_HB, rows, HD_B), F32),
                        pltpu.VMEM((N_HB, rows, HD_B), F32)],
        compiler_params=_params(),
        name="even_prompt",
    )(x, win, bbd, cbd, lam, d3, gluw, glub3, sg3, sb3, sw, sbt, wout, lng3, lnb3)


def _even_sample_kernel(x_ref, s0r_ref, s0i_ref, win_ref, bbd_ref, cbd_ref, lam_ref, d_ref,
                        gluw_ref, glub_ref, sg_ref, sb_ref, sw0_ref, sb0_ref, wout_ref, lng_ref, lnb_ref,
                        o_ref, sre_ref, sim_ref, vn_ref):
    xb = x_ref[...]
    xbf = xb.astype(BF16)
    ua = _dot(xbf, win_ref[:, 0:D_A])
    uab = ua.astype(BF16)
    ys = []
    for j in range(N_CG):
        cols = slice(ST_PER_CG * j, ST_PER_CG * (j + 1))
        bu = _dot(uab[:, CH_PER_CG * j:CH_PER_CG * (j + 1)], bbd_ref[j])
        lr = lam_ref[0:1, cols]
        li = lam_ref[1:2, cols]
        s0r = s0r_ref[:, cols]
        s0i = s0i_ref[:, cols]
        sr = lr * s0r - li * s0i + bu[:, 0:ST_PER_CG]
        si = lr * s0i + li * s0r + bu[:, ST_PER_CG:]
        sre_ref[:, cols] = sr
        sim_ref[:, cols] = si
        ys.append(_dot(jnp.concatenate([sr, si], axis=-1).astype(BF16), cbd_ref[j]))
    y = jnp.concatenate(ys, axis=-1) + d_ref[...] * ua
    g = jax.nn.gelu(y, approximate=True)
    ya = g * jax.nn.sigmoid(_dot(g.astype(BF16), gluw_ref[...]) + glub_ref[...])

    ub = _dot(xbf, win_ref[:, D_A:D_A + D_B])
    vb = _dot(xbf, win_ref[:, D_A + D_B:D_A + 2 * D_B])
    vn = _ln(vb, sg_ref[...], sb_ref[...])
    vn_ref[...] = vn
    yb = ub * (sw0_ref[...] * vn + sb0_ref[...])

    cat = jnp.concatenate([ya, yb], axis=-1).astype(BF16)
    out = _dot(cat, wout_ref[...])
    o_ref[...] = _ln(ALPHA * xb + out, lng_ref[...], lnb_ref[...])


def _even_sample(x, s0r, s0i, e, l, win, bbd, cbd, lam, d3, gluw, glub3, sg3, sb3, sw0, sb0, wout, lng3, lnb3):
    in_specs = [_whole(x), _whole(s0r), _whole(s0i),
                _layer(win, e), _whole(bbd), _whole(cbd), _whole(lam), _layer(d3, e), _layer(gluw, e),
                _layer(glub3, e), _layer(sg3, e), _layer(sb3, e), _whole(sw0), _whole(sb0),
                _layer(wout, e), _layer(lng3, l), _layer(lnb3, l)]
    shapes = ((DEC_BATCH, D_MODEL), (DEC_BATCH, N_GA * P_A), (DEC_BATCH, N_GA * P_A), (DEC_BATCH, D_B))
    return pl.pallas_call(
        _even_sample_kernel,
        grid=(1,),
        in_specs=in_specs,
        out_specs=tuple(pl.BlockSpec(s, lambda i: (0, 0)) for s in shapes),
        out_shape=tuple(jax.ShapeDtypeStruct(s, F32) for s in shapes),
        compiler_params=_params(),
        name="even_sample",
    )(x, s0r, s0i, win, bbd, cbd, lam, d3, gluw, glub3, sg3, sb3, sw0, sb0, wout, lng3, lnb3)


def _ffn_kernel(x_ref, halo_ref, wg_ref, wu_ref, cw_ref, cb_ref, wd_ref, lng_ref, lnb_ref,
                o_ref, cache_ref, gs_ref, *rest, tm, shift, batch_major_out):
    hs = (K_F - 1) * shift

    @pl.when(pl.program_id(0) == 0)
    def _():
        gs_ref[0:hs, :] = halo_ref[...]

    xb = x_ref[...]
    xbf = xb.astype(BF16)
    acc = None
    for c0, c1 in FF_CHUNKS:
        gs_ref[hs:hs + tm, c0:c1] = _dot(xbf, wg_ref[:, c0:c1])
        conv = cb_ref[:, c0:c1]
        for k in range(K_F):
            conv = conv + cw_ref[k:k + 1, c0:c1] * gs_ref[k * shift:k * shift + tm, c0:c1]
        up = _dot(xbf, wu_ref[:, c0:c1])
        h = (jax.nn.silu(conv) * up).astype(BF16)
        part = _dot(h, wd_ref[c0:c1, :])
        acc = part if acc is None else acc + part
    y = _ln(ALPHA * xb + acc, lng_ref[...], lnb_ref[...])
    if batch_major_out:
        yt_ref, = rest
        for c in range(D_MODEL // LANES):
            yt_ref[c] = y[:, LANES * c:LANES * (c + 1)]
        for c in range(D_MODEL // LANES):
            for b in range(BATCH):
                o_ref[b, :, LANES * c:LANES * (c + 1)] = yt_ref.at[c][pl.ds(b, tm // BATCH, stride=BATCH), :]
    else:
        o_ref[...] = y
    tail = gs_ref[tm:tm + hs, :]
    cache_ref[...] = tail
    gs_ref[0:hs, :] = tail


def _ffn(x, halo, l, wg, wu, cw, cb3, wd, lng3, lnb3, *, tm, shift, batch_major_out=False):
    m = x.shape[0]
    hs = (K_F - 1) * shift
    in_specs = [pl.BlockSpec((tm, D_MODEL), lambda i: (i, 0)), _whole(halo),
                _layer(wg, l), _layer(wu, l), _layer(cw, l), _layer(cb3, l), _layer(wd, l),
                _layer(lng3, l), _layer(lnb3, l)]
    scratch = [pltpu.VMEM((hs + tm, D_FF), F32)]
    if batch_major_out:
        tpos = tm // BATCH
        o_spec = pl.BlockSpec((BATCH, tpos, D_MODEL), lambda i: (0, i, 0))
        o_shape = jax.ShapeDtypeStruct((BATCH, m // BATCH, D_MODEL), F32)
        scratch.append(pltpu.VMEM((D_MODEL // LANES, tm, LANES), F32))
    else:
        o_spec = pl.BlockSpec((tm, D_MODEL), lambda i: (i, 0))
        o_shape = jax.ShapeDtypeStruct((m, D_MODEL), F32)
    return pl.pallas_call(
        functools.partial(_ffn_kernel, tm=tm, shift=shift, batch_major_out=batch_major_out),
        grid=(m // tm,),
        in_specs=in_specs,
        out_specs=(o_spec, pl.BlockSpec((hs, D_FF), lambda i: (0, 0))),
        out_shape=(o_shape, jax.ShapeDtypeStruct((hs, D_FF), F32)),
        scratch_shapes=scratch,
        compiler_params=_params(),
        name="conv_ffn",
    )(x, halo, wg, wu, cw, cb3, wd, lng3, lnb3)


def _odd_prompt_kernel(x_ref, win_ref, cw_ref, cb_ref, lcg_ref, lcb_ref, wout_ref, lng_ref, lnb_ref,
                       o_ref, cache_ref, gs_ref, hc_ref, *, tm):
    hs = (K_C - 1) * BATCH

    @pl.when(pl.program_id(0) == 0)
    def _():
        gs_ref[0:hs, :] = jnp.zeros((hs, D_C), F32)

    xb = x_ref[...]
    xbf = xb.astype(BF16)
    z1 = _dot(xbf, win_ref[:, 0:D_C])
    z2 = _dot(xbf, win_ref[:, D_C:2 * D_C])
    gs_ref[hs:hs + tm, :] = z1 * jax.nn.sigmoid(z2)

    def conv_rows(i, _):
        r0 = pl.multiple_of(i * CONV_ROWS, CONV_ROWS)
        for l0 in range(0, D_C, LANES):
            cols = slice(l0, l0 + LANES)
            win = gs_ref[pl.ds(r0, CONV_ROWS + hs), cols]
            acc = jnp.broadcast_to(cb_ref[:, cols], (CONV_ROWS, LANES))
            for k in range(K_C):
                acc = acc + cw_ref[k:k + 1, cols] * win[k * BATCH:k * BATCH + CONV_ROWS, :]
            hc_ref[pl.ds(r0, CONV_ROWS), cols] = acc
        return 0

    lax.fori_loop(0, tm // CONV_ROWS, conv_rows, 0)
    h = jax.nn.silu(_ln(hc_ref[...], lcg_ref[...], lcb_ref[...]))
    out = _dot(h.astype(BF16), wout_ref[...])
    o_ref[...] = _ln(ALPHA * xb + out, lng_ref[...], lnb_ref[...])
    tail = gs_ref[tm:tm + hs, :]
    cache_ref[...] = tail
    gs_ref[0:hs, :] = tail


def _odd_prompt(x, o, l, win, cw, cb3, lcg3, lcb3, wout, lng3, lnb3, *, tm):
    hs = (K_C - 1) * BATCH
    in_specs = [pl.BlockSpec((tm, D_MODEL), lambda i: (i, 0)),
                _layer(win, o), _layer(cw, o), _layer(cb3, o), _layer(lcg3, o), _layer(lcb3, o),
                _layer(wout, o), _layer(lng3, l), _layer(lnb3, l)]
    return pl.pallas_call(
        functools.partial(_odd_prompt_kernel, tm=tm),
        grid=(M_PROMPT // tm,),
        in_specs=in_specs,
        out_specs=(pl.BlockSpec((tm, D_MODEL), lambda i: (i, 0)), pl.BlockSpec((hs, D_C), lambda i: (0, 0))),
        out_shape=(jax.ShapeDtypeStruct((M_PROMPT, D_MODEL), F32), jax.ShapeDtypeStruct((hs, D_C), F32)),
        scratch_shapes=[pltpu.VMEM((hs + tm, D_C), F32), pltpu.VMEM((tm, D_C), F32)],
        compiler_params=_params(),
        name="odd_prompt",
    )(x, win, cw, cb3, lcg3, lcb3, wout, lng3, lnb3)


def _cache_conv_kernel(c_ref, cw_ref, o_ref):
    o_ref[...] = jnp.sum(c_ref[...] * cw_ref[0:K_C - 1, :][None], axis=1)


def _cache_conv(cache, o, cw, *, bc=32):
    return pl.pallas_call(
        _cache_conv_kernel,
        grid=(DEC_BATCH // bc,),
        in_specs=[pl.BlockSpec((None, bc, K_C - 1, D_C), lambda i: (o, i, 0, 0)), _layer(cw, o)],
        out_specs=pl.BlockSpec((bc, D_C), lambda i: (i, 0)),
        out_shape=jax.ShapeDtypeStruct((DEC_BATCH, D_C), F32),
        compiler_params=_params(),
        name="odd_sample_cache_conv",
    )(cache, cw)


def _odd_sample_kernel(x_ref, pc_ref, win_ref, cw_ref, cb_ref, lcg_ref, lcb_ref, wout_ref, lng_ref, lnb_ref,
                       o_ref, g_ref):
    xb = x_ref[...]
    xbf = xb.astype(BF16)
    z1 = _dot(xbf, win_ref[:, 0:D_C])
    z2 = _dot(xbf, win_ref[:, D_C:2 * D_C])
    g = z1 * jax.nn.sigmoid(z2)
    g_ref[...] = g
    hc = pc_ref[...] + cw_ref[K_C - 1:K_C, :] * g + cb_ref[...]
    h = jax.nn.silu(_ln(hc, lcg_ref[...], lcb_ref[...]))
    out = _dot(h.astype(BF16), wout_ref[...])
    o_ref[...] = _ln(ALPHA * xb + out, lng_ref[...], lnb_ref[...])


def _odd_sample(x, pc, o, l, win, cw, cb3, lcg3, lcb3, wout, lng3, lnb3):
    in_specs = [_whole(x), _whole(pc),
                _layer(win, o), _layer(cw, o), _layer(cb3, o), _layer(lcg3, o), _layer(lcb3, o),
                _layer(wout, o), _layer(lng3, l), _layer(lnb3, l)]
    shapes = ((DEC_BATCH, D_MODEL), (DEC_BATCH, D_C))
    return pl.pallas_call(
        _odd_sample_kernel,
        grid=(1,),
        in_specs=in_specs,
        out_specs=tuple(pl.BlockSpec(s, lambda i: (0, 0)) for s in shapes),
        out_shape=tuple(jax.ShapeDtypeStruct(s, F32) for s in shapes),
        compiler_params=_params(),
        name="odd_sample",
    )(x, pc, win, cw, cb3, lcg3, lcb3, wout, lng3, lnb3)


def kernel(x_prompt, x_sample, state_a_re, state_a_im, cache_c_conv, cache_ffn_conv, w_in_ab, s5_lam_re, s5_lam_im, s5_log_dt, s5_b_re, s5_b_im, s5_c_re, s5_c_im, s5_d, s5_glu_w, s5_glu_b, sgu_ln_g, sgu_ln_b, sgu_w, sgu_b, w_out_ab, w_in_c, conv_c_w, conv_c_b, ln_c_g, ln_c_b, w_out_c, ffn_w_gate, ffn_w_up, ffn_conv_w, ffn_conv_b, ffn_w_down, ln_mix_g, ln_mix_b, ln_ffn_g, ln_ffn_b):
    xs = x_sample.reshape(DEC_BATCH, D_MODEL)
    xp = x_prompt

    win_ab, wout_ab, gluw = w_in_ab.astype(BF16), w_out_ab.astype(BF16), s5_glu_w.astype(BF16)
    win_c, wout_c = w_in_c.astype(BF16), w_out_c.astype(BF16)
    wg, wu, wd = ffn_w_gate.astype(BF16), ffn_w_up.astype(BF16), ffn_w_down.astype(BF16)
    d3, glub3, sg3, sb3 = _rows3(s5_d), _rows3(s5_glu_b), _rows3(sgu_ln_g), _rows3(sgu_ln_b)
    sbt = jnp.swapaxes(sgu_b, 1, 2)
    ccb3, lcg3, lcb3 = _rows3(conv_c_b), _rows3(ln_c_g), _rows3(ln_c_b)
    fcb3 = _rows3(ffn_conv_b)
    lmg3, lmb3, lfg3, lfb3 = _rows3(ln_mix_g), _rows3(ln_mix_b), _rows3(ln_ffn_g), _rows3(ln_ffn_b)
    halo_p = jnp.zeros(((K_F - 1) * BATCH, D_FF), F32)

    sa_re_p, sa_im_p, sa_re_s, sa_im_s, sb_v_s = [], [], [], [], []
    cc_p, cc_s, cf_p, cf_s = [], [], [], []
    for l in range(DEPTH):
        if l % 2 == 0:
            e = l // 2
            lbr, lbi, bbr, bbi, cneg = _s5_prep(
                s5_lam_re[e], s5_lam_im[e], s5_log_dt[e],
                jnp.swapaxes(s5_b_re[e], 1, 2), jnp.swapaxes(s5_b_im[e], 1, 2), s5_c_im[e])
            bbd, cbd, lam = _block_diag_params(lbr, lbi, bbr, bbi, s5_c_re[e], cneg)
            if l > 0:
                xp = jnp.transpose(xp.reshape(SEQ, BATCH, D_MODEL), (1, 0, 2))
            shared = (win_ab, bbd, cbd, lam, d3, gluw, glub3, sg3, sb3)
            xp, sre, sim = _even_prompt(xp, e, l, *shared, sgu_w, sbt, wout_ab, lmg3, lmb3)
            sw0 = jnp.repeat(sgu_w[e][:, 0, 0], HD_B).reshape(1, D_B)
            sb0 = jnp.repeat(sgu_b[e][:, 0], HD_B).reshape(1, D_B)
            xs, sres, sims, vn = _even_sample(
                xs, state_a_re[e].reshape(DEC_BATCH, -1), state_a_im[e].reshape(DEC_BATCH, -1),
                e, l, *shared, sw0, sb0, wout_ab, lmg3, lmb3)
            sa_re_p.append(sre.reshape(BATCH, N_GA, P_A))
            sa_im_p.append(sim.reshape(BATCH, N_GA, P_A))
            sa_re_s.append(sres.reshape(DEC_BATCH, N_GA, P_A))
            sa_im_s.append(sims.reshape(DEC_BATCH, N_GA, P_A))
            sb_v_s.append(vn.reshape(DEC_BATCH, 1, D_B))
        else:
            o = l // 2
            rest = (win_c, conv_c_w, ccb3, lcg3, lcb3, wout_c, lmg3, lmb3)
            xp, cp = _odd_prompt(xp, o, l, *rest, tm=ODD_TM)
            pc = _cache_conv(cache_c_conv, o, conv_c_w)
            xs, g = _odd_sample(xs, pc, o, l, *rest)
            cc_p.append(jnp.transpose(cp.reshape(K_C - 1, BATCH, D_C), (1, 0, 2)))
            cc_s.append(jnp.concatenate([cache_c_conv[o][:, 1:], g[:, None, :]], axis=1))
        frest = (wg, wu, ffn_conv_w, fcb3, wd, lfg3, lfb3)
        xp, fcp = _ffn(xp, halo_p, l, *frest, tm=FFN_TM, shift=BATCH, batch_major_out=(l == DEPTH - 1))
        halo_s = jnp.transpose(cache_ffn_conv[l], (1, 0, 2)).reshape((K_F - 1) * DEC_BATCH, D_FF)
        xs, fcs = _ffn(xs, halo_s, l, *frest, tm=DEC_BATCH, shift=DEC_BATCH)
        cf_p.append(jnp.transpose(fcp.reshape(K_F - 1, BATCH, D_FF), (1, 0, 2)))
        cf_s.append(jnp.transpose(fcs.reshape(K_F - 1, DEC_BATCH, D_FF), (1, 0, 2)))

    ys = xs.reshape(DEC_BATCH, 1, D_MODEL)
    return (xp, ys,
            jnp.stack(sa_re_p), jnp.stack(sa_im_p), jnp.stack(sa_re_s), jnp.stack(sa_im_s),
            jnp.stack(sb_v_s),
            jnp.stack(cc_p), jnp.stack(cc_s),
            jnp.stack(cf_p), jnp.stack(cf_s))
```

```python
import functools

import jax
import jax.numpy as jnp
from jax import lax
from jax.experimental import pallas as pl
from jax.experimental.pallas import tpu as pltpu

D_MODEL = 1024
BATCH = 8
SEQ = 2048
DEPTH = 2
DEC_BATCH = 128
N_EVEN = (DEPTH + 1) // 2
N_ODD = DEPTH // 2
D_A = D_MODEL // 2
S5_GROUP = 16
N_GA = D_A // S5_GROUP
P_A = 64
D_B = D_MODEL // 2
N_HB = 4
HD_B = D_B // N_HB
CHUNK = 128
D_C = D_MODEL
K_C = 31
D_FF = ((8 * D_MODEL) // 3 + 127) // 128 * 128
K_F = 3
ALPHA = (2.0 * DEPTH) ** 0.25
LN_EPS = 1e-5

F32 = jnp.float32
BF16 = jnp.bfloat16

LANES = 128
MXU_N = 256
M_PROMPT = BATCH * SEQ
ROWS_PER_CHUNK = CHUNK * BATCH
N_CG = 4
G_PER_CG = N_GA // N_CG
CH_PER_CG = G_PER_CG * S5_GROUP
ST_PER_CG = G_PER_CG * P_A
N_FB = D_FF // MXU_N
FF_BLOCK_CHUNKS = (tuple(range(0, 6)), tuple(range(6, N_FB)))
FFN_TM = 512
ODD_TM = 512
CONV_ROWS = 6 * BATCH
ODD_PAD = N_FB * CONV_ROWS - ODD_TM
VMEM_LIMIT = 56 * 1024 * 1024


def _ln(x, g, b):
    mu = jnp.mean(x, axis=-1, keepdims=True)
    xc = x - mu
    var = jnp.mean(xc * xc, axis=-1, keepdims=True)
    return xc * lax.rsqrt(var + LN_EPS) * g + b


def _dot(a, b):
    return jnp.dot(a, b, preferred_element_type=F32)


def _whole(arr):
    nd = arr.ndim
    return pl.BlockSpec(arr.shape, lambda *_: (0,) * nd, pipeline_mode=pl.Buffered(1))


def _layer(arr, l):
    nd = arr.ndim
    return pl.BlockSpec((None,) + arr.shape[1:], lambda *_: (l,) + (0,) * (nd - 1),
                        pipeline_mode=pl.Buffered(1))


def _rows3(v):
    return v.reshape(v.shape[0], 1, v.shape[1])


def _params(n_grid=1):
    return pltpu.CompilerParams(dimension_semantics=("arbitrary",) * n_grid,
                                vmem_limit_bytes=VMEM_LIMIT)


def _cast_kernel(w_ref, o_ref):
    o_ref[...] = w_ref[...].astype(BF16)


def _to_bf16(w, *, rb=256):
    n_l, r, c = w.shape
    return pl.pallas_call(
        _cast_kernel,
        grid=(n_l, r // rb),
        in_specs=[pl.BlockSpec((None, rb, c), lambda l, i: (l, i, 0))],
        out_specs=pl.BlockSpec((None, rb, c), lambda l, i: (l, i, 0)),
        out_shape=jax.ShapeDtypeStruct(w.shape, BF16),
        compiler_params=_params(2),
        name="to_bf16",
    )(w)


def _to_bf16_hidden_blocks(w):
    n_l, r, _ = w.shape
    return pl.pallas_call(
        _cast_kernel,
        grid=(n_l, N_FB),
        in_specs=[pl.BlockSpec((None, r, MXU_N), lambda l, j: (l, 0, j))],
        out_specs=pl.BlockSpec((None, None, r, MXU_N), lambda l, j: (l, j, 0, 0)),
        out_shape=jax.ShapeDtypeStruct((n_l, N_FB, r, MXU_N), BF16),
        compiler_params=_params(2),
        name="to_bf16_hidden_blocks",
    )(w)


def _s5_prep_kernel(lr_ref, li_ref, ldt_ref, br_ref, bi_ref, ci_ref,
                    lbr_ref, lbi_ref, bbr_ref, bbi_ref, cneg_ref):
    lr = lr_ref[...]
    li = li_ref[...]
    dt = jnp.exp(ldt_ref[...])
    mag = jnp.exp(lr * dt)
    lbr = mag * jnp.cos(li * dt)
    lbi = mag * jnp.sin(li * dt)
    lbr_ref[...] = lbr
    lbi_ref[...] = lbi
    nr = lbr - 1.0
    ni = lbi
    den = lr * lr + li * li
    qr = (nr * lr + ni * li) / den
    qi = (ni * lr - nr * li) / den
    br = br_ref[...]
    bi = bi_ref[...]
    qr3 = qr[:, None, :]
    qi3 = qi[:, None, :]
    bbr_ref[...] = qr3 * br - qi3 * bi
    bbi_ref[...] = qr3 * bi + qi3 * br
    cneg_ref[...] = -ci_ref[...]


def _s5_prep(lam_re, lam_im, log_dt, b_re, b_im, c_im):
    gp = jax.ShapeDtypeStruct((N_GA, P_A), F32)
    gcp = jax.ShapeDtypeStruct((N_GA, S5_GROUP, P_A), F32)
    return pl.pallas_call(
        _s5_prep_kernel,
        out_shape=(gp, gp, gcp, gcp, gcp),
        name="s5_prep",
    )(lam_re, lam_im, log_dt.reshape(N_GA, 1), b_re, b_im, c_im)


def _block_diag_params(lbr, lbi, bbr, bbi, c_re, cneg):
    eye = jnp.eye(G_PER_CG, dtype=F32)

    def in_mat(b):
        b4 = b.reshape(N_CG, G_PER_CG, S5_GROUP, P_A)
        return jnp.einsum('jgcp,gh->jgchp', b4, eye).reshape(N_CG, CH_PER_CG, ST_PER_CG)

    def out_mat(c):
        c4 = c.reshape(N_CG, G_PER_CG, S5_GROUP, P_A)
        return jnp.einsum('jgcp,gh->jgphc', c4, eye).reshape(N_CG, ST_PER_CG, CH_PER_CG)

    bbd = jnp.concatenate([in_mat(bbr), in_mat(bbi)], axis=2).astype(BF16)
    cbd = jnp.concatenate([out_mat(c_re), out_mat(cneg)], axis=1).astype(BF16)
    lam = jnp.stack([lbr.reshape(-1), lbi.reshape(-1)])
    return bbd, cbd, lam


def _even_prompt_kernel(x_ref, win_ref, bbd_ref, cbd_ref, lam_ref, d_ref, gluw_ref, glub_ref,
                        sg_ref, sb_ref, sw_ref, sbt_ref, wout_ref, lng_ref, lnb_ref,
                        o_ref, sre_ref, sim_ref,
                        st_ref, xt_ref, scr_ref, cat_ref, vn_ref, gt_ref):
    @pl.when(pl.program_id(0) == 0)
    def _():
        st_ref[...] = jnp.zeros_like(st_ref)

    for c in range(D_MODEL // LANES):
        for b in range(BATCH):
            xt_ref.at[c][pl.ds(b, CHUNK, stride=BATCH), :] = x_ref[b, :, LANES * c:LANES * (c + 1)]
    xb = jnp.concatenate([xt_ref[c] for c in range(D_MODEL // LANES)], axis=-1)
    xbf = xb.astype(BF16)

    ua = _dot(xbf, win_ref[:, 0:D_A])
    uab = ua.astype(BF16)
    ys = []
    for j in range(N_CG):
        re_cols = slice(2 * ST_PER_CG * j, 2 * ST_PER_CG * j + ST_PER_CG)
        im_cols = slice(2 * ST_PER_CG * j + ST_PER_CG, 2 * ST_PER_CG * (j + 1))
        scr_ref[...] = _dot(uab[:, CH_PER_CG * j:CH_PER_CG * (j + 1)], bbd_ref[j])
        lr = jnp.broadcast_to(lam_ref[0:1, ST_PER_CG * j:ST_PER_CG * (j + 1)], (BATCH, ST_PER_CG))
        li = jnp.broadcast_to(lam_ref[1:2, ST_PER_CG * j:ST_PER_CG * (j + 1)], (BATCH, ST_PER_CG))

        def step(t, carry, lr=lr, li=li):
            sr, si = carry
            row = pl.multiple_of(t * BATCH, BATCH)
            nr = lr * sr - li * si + scr_ref[pl.ds(row, BATCH), 0:ST_PER_CG]
            ni = lr * si + li * sr + scr_ref[pl.ds(row, BATCH), ST_PER_CG:2 * ST_PER_CG]
            scr_ref[pl.ds(row, BATCH), 0:ST_PER_CG] = nr
            scr_ref[pl.ds(row, BATCH), ST_PER_CG:2 * ST_PER_CG] = ni
            return nr, ni

        sr, si = lax.fori_loop(0, CHUNK, step, (st_ref[:, re_cols], st_ref[:, im_cols]), unroll=4)
        st_ref[:, re_cols] = sr
        st_ref[:, im_cols] = si
        sre_ref[:, ST_PER_CG * j:ST_PER_CG * (j + 1)] = sr
        sim_ref[:, ST_PER_CG * j:ST_PER_CG * (j + 1)] = si
        ys.append(_dot(scr_ref[...].astype(BF16), cbd_ref[j]))
    y = jnp.concatenate(ys, axis=-1) + d_ref[...] * ua
    g = jax.nn.gelu(y, approximate=True)
    ya = g * jax.nn.sigmoid(_dot(g.astype(BF16), gluw_ref[...]) + glub_ref[...])
    cat_ref[:, 0:D_A] = ya.astype(BF16)

    ub = _dot(xbf, win_ref[:, D_A:D_A + D_B])
    vb = _dot(xbf, win_ref[:, D_A + D_B:D_A + 2 * D_B])
    vn = _ln(vb, sg_ref[...], sb_ref[...])
    for h in range(N_HB):
        vn_ref[h] = vn[:, HD_B * h:HD_B * (h + 1)]
    r_id = lax.broadcasted_iota(jnp.int32, (CHUNK, CHUNK), 0)
    c_id = lax.broadcasted_iota(jnp.int32, (CHUNK, CHUNK), 1)
    tril = (c_id <= r_id).astype(F32)
    wm = [(sw_ref[h] * tril).astype(BF16) for h in range(N_HB)]
    for b in range(0, BATCH, 2):
        for h in range(N_HB):
            v2 = jnp.concatenate([vn_ref.at[h][pl.ds(b, CHUNK, stride=BATCH), :],
                                  vn_ref.at[h][pl.ds(b + 1, CHUNK, stride=BATCH), :]], axis=-1)
            g2 = _dot(wm[h], v2.astype(BF16)) + sbt_ref[:, h:h + 1]
            gt_ref.at[h][pl.ds(b, CHUNK, stride=BATCH), :] = g2[:, 0:HD_B]
            gt_ref.at[h][pl.ds(b + 1, CHUNK, stride=BATCH), :] = g2[:, HD_B:]
    gate = jnp.concatenate([gt_ref[h] for h in range(N_HB)], axis=-1)
    cat_ref[:, D_A:] = (ub * gate).astype(BF16)

    out = _dot(cat_ref[...], wout_ref[...])
    o_ref[...] = _ln(ALPHA * xb + out, lng_ref[...], lnb_ref[...])


def _even_prompt(x, e, l, win, bbd, cbd, lam, d3, gluw, glub3, sg3, sb3, sw, sbt, wout, lng3, lnb3):
    rows = ROWS_PER_CHUNK
    n = SEQ // CHUNK
    in_specs = [pl.BlockSpec((BATCH, CHUNK, D_MODEL), lambda i: (0, i, 0)),
                _layer(win, e), _whole(bbd), _whole(cbd), _whole(lam), _layer(d3, e), _layer(gluw, e),
                _layer(glub3, e), _layer(sg3, e), _layer(sb3, e), _layer(sw, e), _layer(sbt, e),
                _layer(wout, e), _layer(lng3, l), _layer(lnb3, l)]
    st = jax.ShapeDtypeStruct((BATCH, N_GA * P_A), F32)
    st_spec = pl.BlockSpec((BATCH, N_GA * P_A), lambda i: (0, 0))
    return pl.pallas_call(
        _even_prompt_kernel,
        grid=(n,),
        in_specs=in_specs,
        out_specs=(pl.BlockSpec((rows, D_MODEL), lambda i: (i, 0)), st_spec, st_spec),
        out_shape=(jax.ShapeDtypeStruct((M_PROMPT, D_MODEL), F32), st, st),
        scratch_shapes=[pltpu.VMEM((BATCH, 2 * N_GA * P_A), F32),
                        pltpu.VMEM((D_MODEL // LANES, rows, LANES), F32),
                        pltpu.VMEM((rows, 2 * ST_PER_CG), F32),
                        pltpu.VMEM((rows, D_MODEL), BF16),
                        pltpu.VMEM((N_HB, rows, HD_B), F32),
                        pltpu.VMEM((N_HB, rows, HD_B), F32)],
        compiler_params=_params(),
        name="even_prompt",
    )(x, win, bbd, cbd, lam, d3, gluw, glub3, sg3, sb3, sw, sbt, wout, lng3, lnb3)


def _even_sample_kernel(x_ref, s0r_ref, s0i_ref, win_ref, bbd_ref, cbd_ref, lam_ref, d_ref,
                        gluw_ref, glub_ref, sg_ref, sb_ref, sw0_ref, sb0_ref, wout_ref, lng_ref, lnb_ref,
                        o_ref, sre_ref, sim_ref, vn_ref):
    xb = x_ref[...]
    xbf = xb.astype(BF16)
    ua = _dot(xbf, win_ref[:, 0:D_A])
    uab = ua.astype(BF16)
    ys = []
    for j in range(N_CG):
        cols = slice(ST_PER_CG * j, ST_PER_CG * (j + 1))
        bu = _dot(uab[:, CH_PER_CG * j:CH_PER_CG * (j + 1)], bbd_ref[j])
        lr = lam_ref[0:1, cols]
        li = lam_ref[1:2, cols]
        s0r = s0r_ref[:, cols]
        s0i = s0i_ref[:, cols]
        sr = lr * s0r - li * s0i + bu[:, 0:ST_PER_CG]
        si = lr * s0i + li * s0r + bu[:, ST_PER_CG:]
        sre_ref[:, cols] = sr
        sim_ref[:, cols] = si
        ys.append(_dot(jnp.concatenate([sr, si], axis=-1).astype(BF16), cbd_ref[j]))
    y = jnp.concatenate(ys, axis=-1) + d_ref[...] * ua
    g = jax.nn.gelu(y, approximate=True)
    ya = g * jax.nn.sigmoid(_dot(g.astype(BF16), gluw_ref[...]) + glub_ref[...])

    ub = _dot(xbf, win_ref[:, D_A:D_A + D_B])
    vb = _dot(xbf, win_ref[:, D_A + D_B:D_A + 2 * D_B])
    vn = _ln(vb, sg_ref[...], sb_ref[...])
    vn_ref[...] = vn
    yb = ub * (sw0_ref[...] * vn + sb0_ref[...])

    cat = jnp.concatenate([ya, yb], axis=-1).astype(BF16)
    out = _dot(cat, wout_ref[...])
    o_ref[...] = _ln(ALPHA * xb + out, lng_ref[...], lnb_ref[...])


def _even_sample(x, s0r, s0i, e, l, win, bbd, cbd, lam, d3, gluw, glub3, sg3, sb3, sw0, sb0, wout, lng3, lnb3):
    in_specs = [_whole(x), _whole(s0r), _whole(s0i),
                _layer(win, e), _whole(bbd), _whole(cbd), _whole(lam), _layer(d3, e), _layer(gluw, e),
                _layer(glub3, e), _layer(sg3, e), _layer(sb3, e), _whole(sw0), _whole(sb0),
                _layer(wout, e), _layer(lng3, l), _layer(lnb3, l)]
    shapes = ((DEC_BATCH, D_MODEL), (DEC_BATCH, N_GA * P_A), (DEC_BATCH, N_GA * P_A), (DEC_BATCH, D_B))
    return pl.pallas_call(
        _even_sample_kernel,
        grid=(1,),
        in_specs=in_specs,
        out_specs=tuple(pl.BlockSpec(s, lambda i: (0, 0)) for s in shapes),
        out_shape=tuple(jax.ShapeDtypeStruct(s, F32) for s in shapes),
        compiler_params=_params(),
        name="even_sample",
    )(x, s0r, s0i, win, bbd, cbd, lam, d3, gluw, glub3, sg3, sb3, sw0, sb0, wout, lng3, lnb3)


def _store_batch_major(y, o_ref, yt_ref, tm):
    for c in range(D_MODEL // LANES):
        yt_ref[c] = y[:, LANES * c:LANES * (c + 1)]
    for c in range(D_MODEL // LANES):
        for b in range(BATCH):
            o_ref[b, :, LANES * c:LANES * (c + 1)] = yt_ref.at[c][pl.ds(b, tm // BATCH, stride=BATCH), :]


def _ffn_kernel(x_ref, halo_ref, wg_ref, wu_ref, cw_ref, cb_ref, wd_ref, lng_ref, lnb_ref,
                o_ref, cache_ref, gs_ref, *rest, tm, shift, batch_major_out):
    hs = (K_F - 1) * shift

    @pl.when(pl.program_id(0) == 0)
    def _():
        gs_ref[0:hs, :] = halo_ref[...]

    xb = x_ref[...]
    xbf = xb.astype(BF16)
    acc = None
    for blocks in FF_BLOCK_CHUNKS:
        hb = []
        for j in blocks:
            cols = slice(MXU_N * j, MXU_N * (j + 1))
            gs_ref[hs:hs + tm, cols] = _dot(xbf, wg_ref[j])
            conv = cb_ref[j]
            for k in range(K_F):
                conv = conv + cw_ref[j, k:k + 1, :] * gs_ref[k * shift:k * shift + tm, cols]
            hb.append((jax.nn.silu(conv) * _dot(xbf, wu_ref[j])).astype(BF16))
        part = _dot(jnp.concatenate(hb, axis=-1), wd_ref[MXU_N * blocks[0]:MXU_N * (blocks[-1] + 1), :])
        acc = part if acc is None else acc + part
    y = _ln(ALPHA * xb + acc, lng_ref[...], lnb_ref[...])
    if batch_major_out:
        _store_batch_major(y, o_ref, rest[0], tm)
    else:
        o_ref[...] = y
    tail = gs_ref[tm:tm + hs, :]
    cache_ref[...] = tail
    gs_ref[0:hs, :] = tail


def _ffn(x, halo, l, wg, wu, cw, cb3, wd, lng3, lnb3, *, tm, shift, batch_major_out=False):
    m = x.shape[0]
    hs = (K_F - 1) * shift
    in_specs = [pl.BlockSpec((tm, D_MODEL), lambda i: (i, 0)), _whole(halo),
                _layer(wg, l), _layer(wu, l), _layer(cw, l), _layer(cb3, l), _layer(wd, l),
                _layer(lng3, l), _layer(lnb3, l)]
    scratch = [pltpu.VMEM((hs + tm, D_FF), F32)]
    if batch_major_out:
        tpos = tm // BATCH
        o_spec = pl.BlockSpec((BATCH, tpos, D_MODEL), lambda i: (0, i, 0))
        o_shape = jax.ShapeDtypeStruct((BATCH, m // BATCH, D_MODEL), F32)
        scratch.append(pltpu.VMEM((D_MODEL // LANES, tm, LANES), F32))
    else:
        o_spec = pl.BlockSpec((tm, D_MODEL), lambda i: (i, 0))
        o_shape = jax.ShapeDtypeStruct((m, D_MODEL), F32)
    return pl.pallas_call(
        functools.partial(_ffn_kernel, tm=tm, shift=shift, batch_major_out=batch_major_out),
        grid=(m // tm,),
        in_specs=in_specs,
        out_specs=(o_spec, pl.BlockSpec((hs, D_FF), lambda i: (0, 0))),
        out_shape=(o_shape, jax.ShapeDtypeStruct((hs, D_FF), F32)),
        scratch_shapes=scratch,
        compiler_params=_params(),
        name="conv_ffn",
    )(x, halo, wg, wu, cw, cb3, wd, lng3, lnb3)


def _odd_ffn_kernel(x_ref, win_ref, ccw_ref, ccb_ref, lcg_ref, lcb_ref, wout_ref, lmg_ref, lmb_ref,
                    wg_ref, wu_ref, fcw_ref, fcb_ref, wd_ref, lfg_ref, lfb_ref,
                    o_ref, cc_ref, cf_ref,
                    gs_ref, hc_ref, x1_ref, x1b_ref, acc_ref, work_ref, fh_ref, *rest,
                    tm, n, batch_major_out):
    hs = (K_C - 1) * BATCH
    fhs = (K_F - 1) * BATCH
    i = pl.program_id(0)

    @pl.when(i == 0)
    def _():
        gs_ref[0:hs, :] = jnp.zeros((hs, D_C), F32)
        gs_ref[hs + tm:, :] = jnp.zeros((ODD_PAD, D_C), F32)
        fh_ref[...] = jnp.zeros_like(fh_ref)
        x1_ref[...] = jnp.zeros_like(x1_ref)
        x1b_ref[...] = jnp.zeros_like(x1b_ref)

    @pl.when(i < n)
    def _():
        xbf = x_ref[...].astype(BF16)
        z1 = _dot(xbf, win_ref[:, 0:D_C])
        z2 = _dot(xbf, win_ref[:, D_C:2 * D_C])
        gs_ref[hs:hs + tm, :] = z1 * jax.nn.sigmoid(z2)

    def aligned(v, m):
        return v if isinstance(v, int) else pl.multiple_of(v, m)

    def conv_slice(j, lane_tiles=range(D_C // LANES)):
        r0 = aligned(j * CONV_ROWS, BATCH)
        for l0 in [LANES * t for t in lane_tiles]:
            cols = slice(l0, l0 + LANES)
            win = gs_ref[pl.ds(r0, CONV_ROWS + hs), cols]
            acc = jnp.broadcast_to(ccb_ref[:, cols], (CONV_ROWS, LANES))
            for k in range(K_C):
                acc = acc + ccw_ref[k:k + 1, cols] * win[k * BATCH:k * BATCH + CONV_ROWS, :]
            hc_ref[pl.ds(r0, CONV_ROWS), cols] = acc

    def ffn_block(j, w):
        xbf = x1b_ref[...]
        work_ref[w, 0:fhs, :] = fh_ref[j]
        work_ref[w, fhs:fhs + tm, :] = _dot(xbf, wg_ref[j])
        conv = fcb_ref[j]
        for k in range(K_F):
            conv = conv + fcw_ref[j, k:k + 1, :] * work_ref[w, k * BATCH:k * BATCH + tm, :]
        fh_ref[j] = work_ref[w, tm:tm + fhs, :]
        h = (jax.nn.silu(conv) * _dot(xbf, wu_ref[j])).astype(BF16)
        k0 = aligned(j * MXU_N, MXU_N)
        return _dot(h, wd_ref[pl.ds(k0, MXU_N), :])

    def pair(p, _):
        conv_slice(2 * p)
        conv_slice(2 * p + 1)
        acc_ref[...] += ffn_block(2 * p, 0) + ffn_block(2 * p + 1, 1)
        return 0

    acc_ref[...] = jnp.zeros_like(acc_ref)
    lax.fori_loop(0, N_FB // 2, pair, 0)
    conv_slice(N_FB - 1)
    acc_ref[...] += ffn_block(N_FB - 1, 0)

    @pl.when(i > 0)
    def _():
        y = _ln(ALPHA * x1_ref[...] + acc_ref[...], lfg_ref[...], lfb_ref[...])
        if batch_major_out:
            _store_batch_major(y, o_ref, rest[0], tm)
        else:
            o_ref[...] = y
        cf_ref[...] = fh_ref[...]

    @pl.when(i < n)
    def _():
        h = jax.nn.silu(_ln(hc_ref[0:tm, :], lcg_ref[...], lcb_ref[...]))
        out = _dot(h.astype(BF16), wout_ref[...])
        x1 = _ln(ALPHA * x_ref[...] + out, lmg_ref[...], lmb_ref[...])
        x1_ref[...] = x1
        x1b_ref[...] = x1.astype(BF16)
        tail = gs_ref[tm:tm + hs, :]
        cc_ref[...] = tail
        gs_ref[0:hs, :] = tail


def _odd_ffn(x, o, l, win, ccw, ccb3, lcg3, lcb3, wout, lmg3, lmb3, wg, wu, fcw, fcb, wd, lfg3, lfb3,
             *, tm, batch_major_out):
    n = M_PROMPT // tm
    hs = (K_C - 1) * BATCH
    fhs = (K_F - 1) * BATCH
    in_specs = [pl.BlockSpec((tm, D_MODEL), lambda i: (jnp.minimum(i, n - 1), 0)),
                _layer(win, o), _layer(ccw, o), _layer(ccb3, o), _layer(lcg3, o), _layer(lcb3, o),
                _layer(wout, o), _layer(lmg3, l), _layer(lmb3, l),
                _layer(wg, l), _layer(wu, l), _layer(fcw, l), _layer(fcb, l), _layer(wd, l),
                _layer(lfg3, l), _layer(lfb3, l)]
    scratch = [pltpu.VMEM((hs + tm + ODD_PAD, D_C), F32),
               pltpu.VMEM((tm + ODD_PAD, D_C), F32),
               pltpu.VMEM((tm, D_MODEL), F32),
               pltpu.VMEM((tm, D_MODEL), BF16),
               pltpu.VMEM((tm, D_MODEL), F32),
               pltpu.VMEM((2, fhs + tm, MXU_N), F32),
               pltpu.VMEM((N_FB, fhs, MXU_N), F32)]
    if batch_major_out:
        o_spec = pl.BlockSpec((BATCH, tm // BATCH, D_MODEL), lambda i: (0, jnp.maximum(i - 1, 0), 0))
        o_shape = jax.ShapeDtypeStruct((BATCH, SEQ, D_MODEL), F32)
        scratch.append(pltpu.VMEM((D_MODEL // LANES, tm, LANES), F32))
    else:
        o_spec = pl.BlockSpec((tm, D_MODEL), lambda i: (jnp.maximum(i - 1, 0), 0))
        o_shape = jax.ShapeDtypeStruct((M_PROMPT, D_MODEL), F32)
    return pl.pallas_call(
        functools.partial(_odd_ffn_kernel, tm=tm, n=n, batch_major_out=batch_major_out),
        grid=(n + 1,),
        in_specs=in_specs,
        out_specs=(o_spec, pl.BlockSpec((hs, D_C), lambda i: (0, 0)),
                   pl.BlockSpec((N_FB, fhs, MXU_N), lambda i: (0, 0, 0))),
        out_shape=(o_shape, jax.ShapeDtypeStruct((hs, D_C), F32),
                   jax.ShapeDtypeStruct((N_FB, fhs, MXU_N), F32)),
        scratch_shapes=scratch,
        compiler_params=_params(),
        name="odd_ffn_prompt",
    )(x, win, ccw, ccb3, lcg3, lcb3, wout, lmg3, lmb3, wg, wu, fcw, fcb, wd, lfg3, lfb3)


def _cache_conv_kernel(c_ref, cw_ref, o_ref):
    o_ref[...] = jnp.sum(c_ref[...] * cw_ref[0:K_C - 1, :][None], axis=1)


def _cache_conv(cache, o, cw, *, bc=32):
    return pl.pallas_call(
        _cache_conv_kernel,
        grid=(DEC_BATCH // bc,),
        in_specs=[pl.BlockSpec((None, bc, K_C - 1, D_C), lambda i: (o, i, 0, 0)), _layer(cw, o)],
        out_specs=pl.BlockSpec((bc, D_C), lambda i: (i, 0)),
        out_shape=jax.ShapeDtypeStruct((DEC_BATCH, D_C), F32),
        compiler_params=_params(),
        name="odd_sample_cache_conv",
    )(cache, cw)


def _cache_shift_kernel(c_ref, g_ref, o_ref):
    o_ref[:, 0:K_C - 2, :] = c_ref[:, 1:K_C - 1, :]
    o_ref[:, K_C - 2:K_C - 1, :] = g_ref[...]


def _cache_shift(cache, o, g3, *, bc=32):
    return pl.pallas_call(
        _cache_shift_kernel,
        grid=(DEC_BATCH // bc,),
        in_specs=[pl.BlockSpec((None, bc, K_C - 1, D_C), lambda i: (o, i, 0, 0)),
                  pl.BlockSpec((bc, 1, D_C), lambda i: (i, 0, 0))],
        out_specs=pl.BlockSpec((bc, K_C - 1, D_C), lambda i: (i, 0, 0)),
        out_shape=jax.ShapeDtypeStruct((DEC_BATCH, K_C - 1, D_C), F32),
        compiler_params=_params(),
        name="odd_sample_cache_shift",
    )(cache, g3)


def _odd_sample_kernel(x_ref, pc_ref, win_ref, cw_ref, cb_ref, lcg_ref, lcb_ref, wout_ref, lng_ref, lnb_ref,
                       o_ref, g_ref):
    xb = x_ref[...]
    xbf = xb.astype(BF16)
    z1 = _dot(xbf, win_ref[:, 0:D_C])
    z2 = _dot(xbf, win_ref[:, D_C:2 * D_C])
    g = z1 * jax.nn.sigmoid(z2)
    g_ref[...] = g
    hc = pc_ref[...] + cw_ref[K_C - 1:K_C, :] * g + cb_ref[...]
    h = jax.nn.silu(_ln(hc, lcg_ref[...], lcb_ref[...]))
    out = _dot(h.astype(BF16), wout_ref[...])
    o_ref[...] = _ln(ALPHA * xb + out, lng_ref[...], lnb_ref[...])


def _odd_sample(x, pc, o, l, win, cw, cb3, lcg3, lcb3, wout, lng3, lnb3):
    in_specs = [_whole(x), _whole(pc),
                _layer(win, o), _layer(cw, o), _layer(cb3, o), _layer(lcg3, o), _layer(lcb3, o),
                _layer(wout, o), _layer(lng3, l), _layer(lnb3, l)]
    shapes = ((DEC_BATCH, D_MODEL), (DEC_BATCH, D_C))
    return pl.pallas_call(
        _odd_sample_kernel,
        grid=(1,),
        in_specs=in_specs,
        out_specs=tuple(pl.BlockSpec(s, lambda i: (0, 0)) for s in shapes),
        out_shape=tuple(jax.ShapeDtypeStruct(s, F32) for s in shapes),
        compiler_params=_params(),
        name="odd_sample",
    )(x, pc, win, cw, cb3, lcg3, lcb3, wout, lng3, lnb3)


def kernel(x_prompt, x_sample, state_a_re, state_a_im, cache_c_conv, cache_ffn_conv, w_in_ab, s5_lam_re, s5_lam_im, s5_log_dt, s5_b_re, s5_b_im, s5_c_re, s5_c_im, s5_d, s5_glu_w, s5_glu_b, sgu_ln_g, sgu_ln_b, sgu_w, sgu_b, w_out_ab, w_in_c, conv_c_w, conv_c_b, ln_c_g, ln_c_b, w_out_c, ffn_w_gate, ffn_w_up, ffn_conv_w, ffn_conv_b, ffn_w_down, ln_mix_g, ln_mix_b, ln_ffn_g, ln_ffn_b):
    xs = x_sample.reshape(DEC_BATCH, D_MODEL)
    xp = x_prompt

    win_ab, wout_ab, gluw = _to_bf16(w_in_ab), _to_bf16(w_out_ab), _to_bf16(s5_glu_w)
    win_c, wout_c = _to_bf16(w_in_c), _to_bf16(w_out_c)
    wg, wu = _to_bf16_hidden_blocks(ffn_w_gate), _to_bf16_hidden_blocks(ffn_w_up)
    wd = _to_bf16(ffn_w_down)
    fcw = jnp.transpose(ffn_conv_w.reshape(DEPTH, K_F, N_FB, MXU_N), (0, 2, 1, 3))
    fcb = ffn_conv_b.reshape(DEPTH, N_FB, 1, MXU_N)
    d3, glub3, sg3, sb3 = _rows3(s5_d), _rows3(s5_glu_b), _rows3(sgu_ln_g), _rows3(sgu_ln_b)
    sbt = jnp.swapaxes(sgu_b, 1, 2)
    ccb3, lcg3, lcb3 = _rows3(conv_c_b), _rows3(ln_c_g), _rows3(ln_c_b)
    lmg3, lmb3, lfg3, lfb3 = _rows3(ln_mix_g), _rows3(ln_mix_b), _rows3(ln_ffn_g), _rows3(ln_ffn_b)
    halo_p = jnp.zeros(((K_F - 1) * BATCH, D_FF), F32)

    sa_re_p, sa_im_p, sa_re_s, sa_im_s, sb_v_s = [], [], [], [], []
    cc_p, cc_s, cf_p, cf_s = [], [], [], []
    for l in range(DEPTH):
        if l % 2 == 0:
            e = l // 2
            lbr, lbi, bbr, bbi, cneg = _s5_prep(
                s5_lam_re[e], s5_lam_im[e], s5_log_dt[e],
                jnp.swapaxes(s5_b_re[e], 1, 2), jnp.swapaxes(s5_b_im[e], 1, 2), s5_c_im[e])
            bbd, cbd, lam = _block_diag_params(lbr, lbi, bbr, bbi, s5_c_re[e], cneg)
            if l > 0:
                xp = jnp.transpose(xp.reshape(SEQ, BATCH, D_MODEL), (1, 0, 2))
            shared = (win_ab, bbd, cbd, lam, d3, gluw, glub3, sg3, sb3)
            xp, sre, sim = _even_prompt(xp, e, l, *shared, sgu_w, sbt, wout_ab, lmg3, lmb3)
            sw0 = jnp.repeat(sgu_w[e][:, 0, 0], HD_B).reshape(1, D_B)
            sb0 = jnp.repeat(sgu_b[e][:, 0], HD_B).reshape(1, D_B)
            xs, sres, sims, vn = _even_sample(
                xs, state_a_re[e].reshape(DEC_BATCH, -1), state_a_im[e].reshape(DEC_BATCH, -1),
                e, l, *shared, sw0, sb0, wout_ab, lmg3, lmb3)
            sa_re_p.append(sre.reshape(BATCH, N_GA, P_A))
            sa_im_p.append(sim.reshape(BATCH, N_GA, P_A))
            sa_re_s.append(sres.reshape(DEC_BATCH, N_GA, P_A))
            sa_im_s.append(sims.reshape(DEC_BATCH, N_GA, P_A))
            sb_v_s.append(vn.reshape(DEC_BATCH, 1, D_B))
        else:
            o = l // 2
            rest = (win_c, conv_c_w, ccb3, lcg3, lcb3, wout_c, lmg3, lmb3)
            xp, cp, fcb_p = _odd_ffn(xp, o, l, *rest, wg, wu, fcw, fcb, wd, lfg3, lfb3,
                                     tm=ODD_TM, batch_major_out=(l == DEPTH - 1))
            fcp = jnp.transpose(fcb_p, (1, 0, 2)).reshape((K_F - 1) * BATCH, D_FF)
            pc = _cache_conv(cache_c_conv, o, conv_c_w)
            xs, g = _odd_sample(xs, pc, o, l, *rest)
            cc_p.append(jnp.transpose(cp.reshape(K_C - 1, BATCH, D_C), (1, 0, 2)))
            cc_s.append(_cache_shift(cache_c_conv, o, g.reshape(DEC_BATCH, 1, D_C)))
        frest = (wg, wu, fcw, fcb, wd, lfg3, lfb3)
        if l % 2 == 0:
            xp, fcp = _ffn(xp, halo_p, l, *frest, tm=FFN_TM, shift=BATCH, batch_major_out=(l == DEPTH - 1))
        halo_s = jnp.transpose(cache_ffn_conv[l], (1, 0, 2)).reshape((K_F - 1) * DEC_BATCH, D_FF)
        xs, fcs = _ffn(xs, halo_s, l, *frest, tm=DEC_BATCH, shift=DEC_BATCH)
        cf_p.append(jnp.transpose(fcp.reshape(K_F - 1, BATCH, D_FF), (1, 0, 2)))
        cf_s.append(jnp.transpose(fcs.reshape(K_F - 1, DEC_BATCH, D_FF), (1, 0, 2)))

    ys = xs.reshape(DEC_BATCH, 1, D_MODEL)
    return (xp, ys,
            jnp.stack(sa_re_p), jnp.stack(sa_im_p), jnp.stack(sa_re_s), jnp.stack(sa_im_s),
            jnp.stack(sb_v_s),
            jnp.stack(cc_p), jnp.stack(cc_s),
            jnp.stack(cf_p), jnp.stack(cf_s))
```

```python
import functools

import jax
import jax.numpy as jnp
from jax import lax
from jax.experimental import pallas as pl
from jax.experimental.pallas import tpu as pltpu

D_MODEL = 1024
BATCH = 8
SEQ = 2048
DEPTH = 2
DEC_BATCH = 128
N_EVEN = (DEPTH + 1) // 2
N_ODD = DEPTH // 2
D_A = D_MODEL // 2
S5_GROUP = 16
N_GA = D_A // S5_GROUP
P_A = 64
D_B = D_MODEL // 2
N_HB = 4
HD_B = D_B // N_HB
CHUNK = 128
D_C = D_MODEL
K_C = 31
D_FF = ((8 * D_MODEL) // 3 + 127) // 128 * 128
K_F = 3
ALPHA = (2.0 * DEPTH) ** 0.25
LN_EPS = 1e-5

F32 = jnp.float32
BF16 = jnp.bfloat16

LANES = 128
MXU_N = 256
M_PROMPT = BATCH * SEQ
ROWS_PER_CHUNK = CHUNK * BATCH
N_CG = 4
G_PER_CG = N_GA // N_CG
CH_PER_CG = G_PER_CG * S5_GROUP
ST_PER_CG = G_PER_CG * P_A
FF_CHUNKS = ((0, 6 * MXU_N), (6 * MXU_N, D_FF))
FFN_TM = 512
ODD_TM = 512
N_ZB = 2 * D_C // MXU_N
CONV_ROWS = ODD_TM // N_ZB
VMEM_LIMIT = 56 * 1024 * 1024


def _ln(x, g, b):
    mu = jnp.mean(x, axis=-1, keepdims=True)
    xc = x - mu
    var = jnp.mean(xc * xc, axis=-1, keepdims=True)
    return xc * lax.rsqrt(var + LN_EPS) * g + b


def _dot(a, b):
    return jnp.dot(a, b, preferred_element_type=F32)


def _whole(arr):
    nd = arr.ndim
    return pl.BlockSpec(arr.shape, lambda *_: (0,) * nd, pipeline_mode=pl.Buffered(1))


def _layer(arr, l):
    nd = arr.ndim
    return pl.BlockSpec((None,) + arr.shape[1:], lambda *_: (l,) + (0,) * (nd - 1),
                        pipeline_mode=pl.Buffered(1))


def _rows3(v):
    return v.reshape(v.shape[0], 1, v.shape[1])


def _params(n_grid=1):
    return pltpu.CompilerParams(dimension_semantics=("arbitrary",) * n_grid,
                                vmem_limit_bytes=VMEM_LIMIT)


def _s5_prep_kernel(lr_ref, li_ref, ldt_ref, br_ref, bi_ref, ci_ref,
                    lbr_ref, lbi_ref, bbr_ref, bbi_ref, cneg_ref):
    lr = lr_ref[...]
    li = li_ref[...]
    dt = jnp.exp(ldt_ref[...])
    mag = jnp.exp(lr * dt)
    lbr = mag * jnp.cos(li * dt)
    lbi = mag * jnp.sin(li * dt)
    lbr_ref[...] = lbr
    lbi_ref[...] = lbi
    nr = lbr - 1.0
    ni = lbi
    den = lr * lr + li * li
    qr = (nr * lr + ni * li) / den
    qi = (ni * lr - nr * li) / den
    br = br_ref[...]
    bi = bi_ref[...]
    qr3 = qr[:, None, :]
    qi3 = qi[:, None, :]
    bbr_ref[...] = qr3 * br - qi3 * bi
    bbi_ref[...] = qr3 * bi + qi3 * br
    cneg_ref[...] = -ci_ref[...]


def _s5_prep(lam_re, lam_im, log_dt, b_re, b_im, c_im):
    gp = jax.ShapeDtypeStruct((N_GA, P_A), F32)
    gcp = jax.ShapeDtypeStruct((N_GA, S5_GROUP, P_A), F32)
    return pl.pallas_call(
        _s5_prep_kernel,
        out_shape=(gp, gp, gcp, gcp, gcp),
        name="s5_prep",
    )(lam_re, lam_im, log_dt.reshape(N_GA, 1), b_re, b_im, c_im)


def _block_diag_params(lbr, lbi, bbr, bbi, c_re, cneg):
    eye = jnp.eye(G_PER_CG, dtype=F32)

    def in_mat(b):
        b4 = b.reshape(N_CG, G_PER_CG, S5_GROUP, P_A)
        return jnp.einsum('jgcp,gh->jgchp', b4, eye).reshape(N_CG, CH_PER_CG, ST_PER_CG)

    def out_mat(c):
        c4 = c.reshape(N_CG, G_PER_CG, S5_GROUP, P_A)
        return jnp.einsum('jgcp,gh->jgphc', c4, eye).reshape(N_CG, ST_PER_CG, CH_PER_CG)

    bbd = jnp.concatenate([in_mat(bbr), in_mat(bbi)], axis=2).astype(BF16)
    cbd = jnp.concatenate([out_mat(c_re), out_mat(cneg)], axis=1).astype(BF16)
    lam = jnp.stack([lbr.reshape(-1), lbi.reshape(-1)])
    return bbd, cbd, lam


def _even_prompt_kernel(x_ref, win_ref, bbd_ref, cbd_ref, lam_ref, d_ref, gluw_ref, glub_ref,
                        sg_ref, sb_ref, sw_ref, sbt_ref, wout_ref, lng_ref, lnb_ref,
                        o_ref, sre_ref, sim_ref,
                        st_ref, xt_ref, scr_ref, cat_ref, vn_ref, gt_ref):
    @pl.when(pl.program_id(0) == 0)
    def _():
        st_ref[...] = jnp.zeros_like(st_ref)

    for c in range(D_MODEL // LANES):
        for b in range(BATCH):
            xt_ref.at[c][pl.ds(b, CHUNK, stride=BATCH), :] = x_ref[b, :, LANES * c:LANES * (c + 1)]
    xb = jnp.concatenate([xt_ref[c] for c in range(D_MODEL // LANES)], axis=-1)
    xbf = xb.astype(BF16)

    ua = _dot(xbf, win_ref[:, 0:D_A])
    uab = ua.astype(BF16)
    ys = []
    for j in range(N_CG):
        re_cols = slice(2 * ST_PER_CG * j, 2 * ST_PER_CG * j + ST_PER_CG)
        im_cols = slice(2 * ST_PER_CG * j + ST_PER_CG, 2 * ST_PER_CG * (j + 1))
        scr_ref[...] = _dot(uab[:, CH_PER_CG * j:CH_PER_CG * (j + 1)], bbd_ref[j])
        lr = jnp.broadcast_to(lam_ref[0:1, ST_PER_CG * j:ST_PER_CG * (j + 1)], (BATCH, ST_PER_CG))
        li = jnp.broadcast_to(lam_ref[1:2, ST_PER_CG * j:ST_PER_CG * (j + 1)], (BATCH, ST_PER_CG))

        def step(t, carry, lr=lr, li=li):
            sr, si = carry
            row = pl.multiple_of(t * BATCH, BATCH)
            nr = lr * sr - li * si + scr_ref[pl.ds(row, BATCH), 0:ST_PER_CG]
            ni = lr * si + li * sr + scr_ref[pl.ds(row, BATCH), ST_PER_CG:2 * ST_PER_CG]
            scr_ref[pl.ds(row, BATCH), 0:ST_PER_CG] = nr
            scr_ref[pl.ds(row, BATCH), ST_PER_CG:2 * ST_PER_CG] = ni
            return nr, ni

        sr, si = lax.fori_loop(0, CHUNK, step, (st_ref[:, re_cols], st_ref[:, im_cols]), unroll=4)
        st_ref[:, re_cols] = sr
        st_ref[:, im_cols] = si
        sre_ref[:, ST_PER_CG * j:ST_PER_CG * (j + 1)] = sr
        sim_ref[:, ST_PER_CG * j:ST_PER_CG * (j + 1)] = si
        ys.append(_dot(scr_ref[...].astype(BF16), cbd_ref[j]))
    y = jnp.concatenate(ys, axis=-1) + d_ref[...] * ua
    g = jax.nn.gelu(y, approximate=True)
    ya = g * jax.nn.sigmoid(_dot(g.astype(BF16), gluw_ref[...]) + glub_ref[...])
    cat_ref[:, 0:D_A] = ya.astype(BF16)

    ub = _dot(xbf, win_ref[:, D_A:D_A + D_B])
    vb = _dot(xbf, win_ref[:, D_A + D_B:D_A + 2 * D_B])
    vn = _ln(vb, sg_ref[...], sb_ref[...])
    for h in range(N_HB):
        vn_ref[h] = vn[:, HD_B * h:HD_B * (h + 1)]
    r_id = lax.broadcasted_iota(jnp.int32, (CHUNK, CHUNK), 0)
    c_id = lax.broadcasted_iota(jnp.int32, (CHUNK, CHUNK), 1)
    tril = (c_id <= r_id).astype(F32)
    wm = [(sw_ref[h] * tril).astype(BF16) for h in range(N_HB)]
    for b in range(0, BATCH, 2):
        for h in range(N_HB):
            v2 = jnp.concatenate([vn_ref.at[h][pl.ds(b, CHUNK, stride=BATCH), :],
                                  vn_ref.at[h][pl.ds(b + 1, CHUNK, stride=BATCH), :]], axis=-1)
            g2 = _dot(wm[h], v2.astype(BF16)) + sbt_ref[:, h:h + 1]
            gt_ref.at[h][pl.ds(b, CHUNK, stride=BATCH), :] = g2[:, 0:HD_B]
            gt_ref.at[h][pl.ds(b + 1, CHUNK, stride=BATCH), :] = g2[:, HD_B:]
    gate = jnp.concatenate([gt_ref[h] for h in range(N_HB)], axis=-1)
    cat_ref[:, D_A:] = (ub * gate).astype(BF16)

    out = _dot(cat_ref[...], wout_ref[...])
    o_ref[...] = _ln(ALPHA * xb + out, lng_ref[...], lnb_ref[...])


def _even_prompt(x, e, l, win, bbd, cbd, lam, d3, gluw, glub3, sg3, sb3, sw, sbt, wout, lng3, lnb3):
    rows = ROWS_PER_CHUNK
    n = SEQ // CHUNK
    in_specs = [pl.BlockSpec((BATCH, CHUNK, D_MODEL), lambda i: (0, i, 0)),
                _layer(win, e), _whole(bbd), _whole(cbd), _whole(lam), _layer(d3, e), _layer(gluw, e),
                _layer(glub3, e), _layer(sg3, e), _layer(sb3, e), _layer(sw, e), _layer(sbt, e),
                _layer(wout, e), _layer(lng3, l), _layer(lnb3, l)]
    st = jax.ShapeDtypeStruct((BATCH, N_GA * P_A), F32)
    st_spec = pl.BlockSpec((BATCH, N_GA * P_A), lambda i: (0, 0))
    return pl.pallas_call(
        _even_prompt_kernel,
        grid=(n,),
        in_specs=in_specs,
        out_specs=(pl.BlockSpec((rows, D_MODEL), lambda i: (i, 0)), st_spec, st_spec),
        out_shape=(jax.ShapeDtypeStruct((M_PROMPT, D_MODEL), F32), st, st),
        scratch_shapes=[pltpu.VMEM((BATCH, 2 * N_GA * P_A), F32),
                        pltpu.VMEM((D_MODEL // LANES, rows, LANES), F32),
                        pltpu.VMEM((rows, 2 * ST_PER_CG), F32),
                        pltpu.VMEM((rows, D_MODEL), BF16),
                        pltpu.VMEM((N_HB, rows, HD_B), F32),
                        pltpu.VMEM((N_HB, rows, HD_B), F32)],
        compiler_params=_params(),
        name="even_prompt",
    )(x, win, bbd, cbd, lam, d3, gluw, glub3, sg3, sb3, sw, sbt, wout, lng3, lnb3)


def _even_sample_kernel(x_ref, s0r_ref, s0i_ref, win_ref, bbd_ref, cbd_ref, lam_ref, d_ref,
                        gluw_ref, glub_ref, sg_ref, sb_ref, sw0_ref, sb0_ref, wout_ref, lng_ref, lnb_ref,
                        o_ref, sre_ref, sim_ref, vn_ref):
    xb = x_ref[...]
    xbf = xb.astype(BF16)
    ua = _dot(xbf, win_ref[:, 0:D_A])
    uab = ua.astype(BF16)
    ys = []
    for j in range(N_CG):
        cols = slice(ST_PER_CG * j, ST_PER_CG * (j + 1))
        bu = _dot(uab[:, CH_PER_CG * j:CH_PER_CG * (j + 1)], bbd_ref[j])
        lr = lam_ref[0:1, cols]
        li = lam_ref[1:2, cols]
        s0r = s0r_ref[:, cols]
        s0i = s0i_ref[:, cols]
        sr = lr * s0r - li * s0i + bu[:, 0:ST_PER_CG]
        si = lr * s0i + li * s0r + bu[:, ST_PER_CG:]
        sre_ref[:, cols] = sr
        sim_ref[:, cols] = si
        ys.append(_dot(jnp.concatenate([sr, si], axis=-1).astype(BF16), cbd_ref[j]))
    y = jnp.concatenate(ys, axis=-1) + d_ref[...] * ua
    g = jax.nn.gelu(y, approximate=True)
    ya = g * jax.nn.sigmoid(_dot(g.astype(BF16), gluw_ref[...]) + glub_ref[...])

    ub = _dot(xbf, win_ref[:, D_A:D_A + D_B])
    vb = _dot(xbf, win_ref[:, D_A + D_B:D_A + 2 * D_B])
    vn = _ln(vb, sg_ref[...], sb_ref[...])
    vn_ref[...] = vn
    yb = ub * (sw0_ref[...] * vn + sb0_ref[...])

    cat = jnp.concatenate([ya, yb], axis=-1).astype(BF16)
    out = _dot(cat, wout_ref[...])
    o_ref[...] = _ln(ALPHA * xb + out, lng_ref[...], lnb_ref[...])


def _even_sample(x, s0r, s0i, e, l, win, bbd, cbd, lam, d3, gluw, glub3, sg3, sb3, sw0, sb0, wout, lng3, lnb3):
    in_specs = [_whole(x), _whole(s0r), _whole(s0i),
                _layer(win, e), _whole(bbd), _whole(cbd), _whole(lam), _layer(d3, e), _layer(gluw, e),
                _layer(glub3, e), _layer(sg3, e), _layer(sb3, e), _whole(sw0), _whole(sb0),
                _layer(wout, e), _layer(lng3, l), _layer(lnb3, l)]
    shapes = ((DEC_BATCH, D_MODEL), (DEC_BATCH, N_GA * P_A), (DEC_BATCH, N_GA * P_A), (DEC_BATCH, D_B))
    return pl.pallas_call(
        _even_sample_kernel,
        grid=(1,),
        in_specs=in_specs,
        out_specs=tuple(pl.BlockSpec(s, lambda i: (0, 0)) for s in shapes),
        out_shape=tuple(jax.ShapeDtypeStruct(s, F32) for s in shapes),
        compiler_params=_params(),
        name="even_sample",
    )(x, s0r, s0i, win, bbd, cbd, lam, d3, gluw, glub3, sg3, sb3, sw0, sb0, wout, lng3, lnb3)


def _store_batch_major(y, o_ref, yt_ref, tm):
    for c in range(D_MODEL // LANES):
        yt_ref[c] = y[:, LANES * c:LANES * (c + 1)]
    for c in range(D_MODEL // LANES):
        for b in range(BATCH):
            o_ref[b, :, LANES * c:LANES * (c + 1)] = yt_ref.at[c][pl.ds(b, tm // BATCH, stride=BATCH), :]


def _ffn_kernel(x_ref, halo_ref, wg_ref, wu_ref, cw_ref, cb_ref, wd_ref, lng_ref, lnb_ref,
                o_ref, cache_ref, gs_ref, *rest, tm, shift, batch_major_out):
    hs = (K_F - 1) * shift

    @pl.when(pl.program_id(0) == 0)
    def _():
        gs_ref[0:hs, :] = halo_ref[...]

    xb = x_ref[...]
    xbf = xb.astype(BF16)
    acc = None
    for c0, c1 in FF_CHUNKS:
        gs_ref[hs:hs + tm, c0:c1] = _dot(xbf, wg_ref[:, c0:c1])
        conv = cb_ref[:, c0:c1]
        for k in range(K_F):
            conv = conv + cw_ref[k:k + 1, c0:c1] * gs_ref[k * shift:k * shift + tm, c0:c1]
        up = _dot(xbf, wu_ref[:, c0:c1])
        h = (jax.nn.silu(conv) * up).astype(BF16)
        part = _dot(h, wd_ref[c0:c1, :])
        acc = part if acc is None else acc + part
    y = _ln(ALPHA * xb + acc, lng_ref[...], lnb_ref[...])
    if batch_major_out:
        _store_batch_major(y, o_ref, rest[0], tm)
    else:
        o_ref[...] = y
    tail = gs_ref[tm:tm + hs, :]
    cache_ref[...] = tail
    gs_ref[0:hs, :] = tail


def _ffn(x, halo, l, wg, wu, cw, cb3, wd, lng3, lnb3, *, tm, shift, batch_major_out=False):
    m = x.shape[0]
    hs = (K_F - 1) * shift
    in_specs = [pl.BlockSpec((tm, D_MODEL), lambda i: (i, 0)), _whole(halo),
                _layer(wg, l), _layer(wu, l), _layer(cw, l), _layer(cb3, l), _layer(wd, l),
                _layer(lng3, l), _layer(lnb3, l)]
    scratch = [pltpu.VMEM((hs + tm, D_FF), F32)]
    if batch_major_out:
        tpos = tm // BATCH
        o_spec = pl.BlockSpec((BATCH, tpos, D_MODEL), lambda i: (0, i, 0))
        o_shape = jax.ShapeDtypeStruct((BATCH, m // BATCH, D_MODEL), F32)
        scratch.append(pltpu.VMEM((D_MODEL // LANES, tm, LANES), F32))
    else:
        o_spec = pl.BlockSpec((tm, D_MODEL), lambda i: (i, 0))
        o_shape = jax.ShapeDtypeStruct((m, D_MODEL), F32)
    return pl.pallas_call(
        functools.partial(_ffn_kernel, tm=tm, shift=shift, batch_major_out=batch_major_out),
        grid=(m // tm,),
        in_specs=in_specs,
        out_specs=(o_spec, pl.BlockSpec((hs, D_FF), lambda i: (0, 0))),
        out_shape=(o_shape, jax.ShapeDtypeStruct((hs, D_FF), F32)),
        scratch_shapes=scratch,
        compiler_params=_params(),
        name="conv_ffn",
    )(x, halo, wg, wu, cw, cb3, wd, lng3, lnb3)


def _odd_glu(z_of_block):
    half = N_ZB // 2
    z1 = jnp.concatenate([z_of_block(b) for b in range(half)], axis=-1)
    z2 = jnp.concatenate([z_of_block(half + b) for b in range(half)], axis=-1)
    return z1 * jax.nn.sigmoid(z2)


def _odd_prompt_kernel(x_ref, xn_ref, win_ref, cw_ref, cb_ref, lcg_ref, lcb_ref, wout_ref, lng_ref, lnb_ref,
                       o_ref, cache_ref, gs_ref, hc_ref, z_ref, xnb_ref, *, tm):
    hs = (K_C - 1) * BATCH

    @pl.when(pl.program_id(0) == 0)
    def _():
        gs_ref[0:hs, :] = jnp.zeros((hs, D_C), F32)
        xbf = x_ref[...].astype(BF16)
        gs_ref[hs:hs + tm, :] = _odd_glu(lambda b: _dot(xbf, win_ref[b]))

    xnb_ref[...] = xn_ref[...].astype(BF16)

    def conv_and_project(j, _):
        r0 = pl.multiple_of(j * CONV_ROWS, CONV_ROWS)
        for l0 in range(0, D_C, LANES):
            cols = slice(l0, l0 + LANES)
            win = gs_ref[pl.ds(r0, CONV_ROWS + hs), cols]
            acc = jnp.broadcast_to(cb_ref[:, cols], (CONV_ROWS, LANES))
            for k in range(K_C):
                acc = acc + cw_ref[k:k + 1, cols] * win[k * BATCH:k * BATCH + CONV_ROWS, :]
            hc_ref[pl.ds(r0, CONV_ROWS), cols] = acc
        z_ref[j] = _dot(xnb_ref[...], win_ref[j])
        return 0

    lax.fori_loop(0, N_ZB, conv_and_project, 0)

    h = jax.nn.silu(_ln(hc_ref[...], lcg_ref[...], lcb_ref[...]))
    out = _dot(h.astype(BF16), wout_ref[...])
    o_ref[...] = _ln(ALPHA * x_ref[...] + out, lng_ref[...], lnb_ref[...])
    tail = gs_ref[tm:tm + hs, :]
    cache_ref[...] = tail
    gs_ref[0:hs, :] = tail
    gs_ref[hs:hs + tm, :] = _odd_glu(lambda b: z_ref[b])


def _odd_prompt(x, o, l, win, cw, cb3, lcg3, lcb3, wout, lng3, lnb3, *, tm):
    n = M_PROMPT // tm
    hs = (K_C - 1) * BATCH
    in_specs = [pl.BlockSpec((tm, D_MODEL), lambda i: (i, 0)),
                pl.BlockSpec((tm, D_MODEL), lambda i: (jnp.minimum(i + 1, n - 1), 0)),
                _layer(win, o), _layer(cw, o), _layer(cb3, o), _layer(lcg3, o), _layer(lcb3, o),
                _layer(wout, o), _layer(lng3, l), _layer(lnb3, l)]
    return pl.pallas_call(
        functools.partial(_odd_prompt_kernel, tm=tm),
        grid=(n,),
        in_specs=in_specs,
        out_specs=(pl.BlockSpec((tm, D_MODEL), lambda i: (i, 0)), pl.BlockSpec((hs, D_C), lambda i: (0, 0))),
        out_shape=(jax.ShapeDtypeStruct((M_PROMPT, D_MODEL), F32), jax.ShapeDtypeStruct((hs, D_C), F32)),
        scratch_shapes=[pltpu.VMEM((hs + tm, D_C), F32),
                        pltpu.VMEM((tm, D_C), F32),
                        pltpu.VMEM((N_ZB, tm, MXU_N), F32),
                        pltpu.VMEM((tm, D_MODEL), BF16)],
        compiler_params=_params(),
        name="odd_prompt",
    )(x, x, win, cw, cb3, lcg3, lcb3, wout, lng3, lnb3)


def _cache_conv_kernel(c_ref, cw_ref, o_ref):
    o_ref[...] = jnp.sum(c_ref[...] * cw_ref[0:K_C - 1, :][None], axis=1)


def _cache_conv(cache, o, cw, *, bc=32):
    return pl.pallas_call(
        _cache_conv_kernel,
        grid=(DEC_BATCH // bc,),
        in_specs=[pl.BlockSpec((None, bc, K_C - 1, D_C), lambda i: (o, i, 0, 0)), _layer(cw, o)],
        out_specs=pl.BlockSpec((bc, D_C), lambda i: (i, 0)),
        out_shape=jax.ShapeDtypeStruct((DEC_BATCH, D_C), F32),
        compiler_params=_params(),
        name="odd_sample_cache_conv",
    )(cache, cw)


def _cache_shift_kernel(c_ref, g_ref, o_ref):
    o_ref[:, 0:K_C - 2, :] = c_ref[:, 1:K_C - 1, :]
    o_ref[:, K_C - 2:K_C - 1, :] = g_ref[...]


def _cache_shift(cache, o, g3, *, bc=32):
    return pl.pallas_call(
        _cache_shift_kernel,
        grid=(DEC_BATCH // bc,),
        in_specs=[pl.BlockSpec((None, bc, K_C - 1, D_C), lambda i: (o, i, 0, 0)),
                  pl.BlockSpec((bc, 1, D_C), lambda i: (i, 0, 0))],
        out_specs=pl.BlockSpec((None, bc, K_C - 1, D_C), lambda i: (0, i, 0, 0)),
        out_shape=jax.ShapeDtypeStruct((1, DEC_BATCH, K_C - 1, D_C), F32),
        compiler_params=_params(),
        name="odd_sample_cache_shift",
    )(cache, g3)


def _odd_sample_kernel(x_ref, pc_ref, win_ref, cw_ref, cb_ref, lcg_ref, lcb_ref, wout_ref, lng_ref, lnb_ref,
                       o_ref, g_ref):
    xb = x_ref[...]
    xbf = xb.astype(BF16)
    g = _odd_glu(lambda b: _dot(xbf, win_ref[b]))
    g_ref[...] = g
    hc = pc_ref[...] + cw_ref[K_C - 1:K_C, :] * g + cb_ref[...]
    h = jax.nn.silu(_ln(hc, lcg_ref[...], lcb_ref[...]))
    out = _dot(h.astype(BF16), wout_ref[...])
    o_ref[...] = _ln(ALPHA * xb + out, lng_ref[...], lnb_ref[...])


def _odd_sample(x, pc, o, l, win, cw, cb3, lcg3, lcb3, wout, lng3, lnb3):
    in_specs = [_whole(x), _whole(pc),
                _layer(win, o), _layer(cw, o), _layer(cb3, o), _layer(lcg3, o), _layer(lcb3, o),
                _layer(wout, o), _layer(lng3, l), _layer(lnb3, l)]
    shapes = ((DEC_BATCH, D_MODEL), (DEC_BATCH, D_C))
    return pl.pallas_call(
        _odd_sample_kernel,
        grid=(1,),
        in_specs=in_specs,
        out_specs=tuple(pl.BlockSpec(s, lambda i: (0, 0)) for s in shapes),
        out_shape=tuple(jax.ShapeDtypeStruct(s, F32) for s in shapes),
        compiler_params=_params(),
        name="odd_sample",
    )(x, pc, win, cw, cb3, lcg3, lcb3, wout, lng3, lnb3)


def kernel(x_prompt, x_sample, state_a_re, state_a_im, cache_c_conv, cache_ffn_conv, w_in_ab, s5_lam_re, s5_lam_im, s5_log_dt, s5_b_re, s5_b_im, s5_c_re, s5_c_im, s5_d, s5_glu_w, s5_glu_b, sgu_ln_g, sgu_ln_b, sgu_w, sgu_b, w_out_ab, w_in_c, conv_c_w, conv_c_b, ln_c_g, ln_c_b, w_out_c, ffn_w_gate, ffn_w_up, ffn_conv_w, ffn_conv_b, ffn_w_down, ln_mix_g, ln_mix_b, ln_ffn_g, ln_ffn_b):
    xs = x_sample.reshape(DEC_BATCH, D_MODEL)
    xp = x_prompt

    win_ab, wout_ab, gluw = w_in_ab.astype(BF16), w_out_ab.astype(BF16), s5_glu_w.astype(BF16)
    wout_c = w_out_c.astype(BF16)
    win_c = jnp.transpose(w_in_c.astype(BF16).reshape(N_ODD, D_MODEL, N_ZB, MXU_N), (0, 2, 1, 3))
    wg, wu, wd = ffn_w_gate.astype(BF16), ffn_w_up.astype(BF16), ffn_w_down.astype(BF16)
    fcb3 = _rows3(ffn_conv_b)
    d3, glub3, sg3, sb3 = _rows3(s5_d), _rows3(s5_glu_b), _rows3(sgu_ln_g), _rows3(sgu_ln_b)
    sbt = jnp.swapaxes(sgu_b, 1, 2)
    ccb3, lcg3, lcb3 = _rows3(conv_c_b), _rows3(ln_c_g), _rows3(ln_c_b)
    lmg3, lmb3, lfg3, lfb3 = _rows3(ln_mix_g), _rows3(ln_mix_b), _rows3(ln_ffn_g), _rows3(ln_ffn_b)
    halo_p = jnp.zeros(((K_F - 1) * BATCH, D_FF), F32)

    sa_re_p, sa_im_p, sa_re_s, sa_im_s, sb_v_s = [], [], [], [], []
    cc_p, cc_s, cf_p, cf_s = [], [], [], []
    for l in range(DEPTH):
        if l % 2 == 0:
            e = l // 2
            lbr, lbi, bbr, bbi, cneg = _s5_prep(
                s5_lam_re[e], s5_lam_im[e], s5_log_dt[e],
                jnp.swapaxes(s5_b_re[e], 1, 2), jnp.swapaxes(s5_b_im[e], 1, 2), s5_c_im[e])
            bbd, cbd, lam = _block_diag_params(lbr, lbi, bbr, bbi, s5_c_re[e], cneg)
            if l > 0:
                xp = jnp.transpose(xp.reshape(SEQ, BATCH, D_MODEL), (1, 0, 2))
            shared = (win_ab, bbd, cbd, lam, d3, gluw, glub3, sg3, sb3)
            xp, sre, sim = _even_prompt(xp, e, l, *shared, sgu_w, sbt, wout_ab, lmg3, lmb3)
            sw0 = jnp.repeat(sgu_w[e][:, 0, 0], HD_B).reshape(1, D_B)
            sb0 = jnp.repeat(sgu_b[e][:, 0], HD_B).reshape(1, D_B)
            xs, sres, sims, vn = _even_sample(
                xs, state_a_re[e].reshape(DEC_BATCH, -1), state_a_im[e].reshape(DEC_BATCH, -1),
                e, l, *shared, sw0, sb0, wout_ab, lmg3, lmb3)
            sa_re_p.append(sre.reshape(BATCH, N_GA, P_A))
            sa_im_p.append(sim.reshape(BATCH, N_GA, P_A))
            sa_re_s.append(sres.reshape(DEC_BATCH, N_GA, P_A))
            sa_im_s.append(sims.reshape(DEC_BATCH, N_GA, P_A))
            sb_v_s.append(vn.reshape(DEC_BATCH, 1, D_B))
        else:
            o = l // 2
            rest = (win_c, conv_c_w, ccb3, lcg3, lcb3, wout_c, lmg3, lmb3)
            xp, cp = _odd_prompt(xp, o, l, *rest, tm=ODD_TM)
            pc = _cache_conv(cache_c_conv, o, conv_c_w)
            xs, g = _odd_sample(xs, pc, o, l, *rest)
            cc_p.append(jnp.transpose(cp.reshape(K_C - 1, BATCH, D_C), (1, 0, 2)))
            cc_s.append(_cache_shift(cache_c_conv, o, g.reshape(DEC_BATCH, 1, D_C)))
        frest = (wg, wu, ffn_conv_w, fcb3, wd, lfg3, lfb3)
        xp, fcp = _ffn(xp, halo_p, l, *frest, tm=FFN_TM, shift=BATCH, batch_major_out=(l == DEPTH - 1))
        halo_s = jnp.transpose(cache_ffn_conv[l], (1, 0, 2)).reshape((K_F - 1) * DEC_BATCH, D_FF)
        xs, fcs = _ffn(xs, halo_s, l, *frest, tm=DEC_BATCH, shift=DEC_BATCH)
        cf_p.append(jnp.transpose(fcp.reshape(K_F - 1, BATCH, D_FF), (1, 0, 2)))
        cf_s.append(jnp.transpose(fcs.reshape(K_F - 1, DEC_BATCH, D_FF), (1, 0, 2)))

    ys = xs.reshape(DEC_BATCH, 1, D_MODEL)
    return (xp, ys,
            jnp.stack(sa_re_p), jnp.stack(sa_im_p), jnp.stack(sa_re_s), jnp.stack(sa_im_s),
            jnp.stack(sb_v_s),
            jnp.stack(cc_p), jnp.concatenate(cc_s, axis=0),
            jnp.stack(cf_p), jnp.stack(cf_s))
```

```python
import functools

import jax
import jax.numpy as jnp
from jax import lax
from jax.experimental import pallas as pl
from jax.experimental.pallas import tpu as pltpu

D_MODEL = 1024
BATCH = 8
SEQ = 2048
DEPTH = 2
DEC_BATCH = 128
N_EVEN = (DEPTH + 1) // 2
N_ODD = DEPTH // 2
D_A = D_MODEL // 2
S5_GROUP = 16
N_GA = D_A // S5_GROUP
P_A = 64
D_B = D_MODEL // 2
N_HB = 4
HD_B = D_B // N_HB
CHUNK = 128
D_C = D_MODEL
K_C = 31
D_FF = ((8 * D_MODEL) // 3 + 127) // 128 * 128
K_F = 3
ALPHA = (2.0 * DEPTH) ** 0.25
LN_EPS = 1e-5

F32 = jnp.float32
BF16 = jnp.bfloat16

LANES = 128
MXU_N = 256
M_PROMPT = BATCH * SEQ
ROWS_PER_CHUNK = CHUNK * BATCH
N_CG = 4
G_PER_CG = N_GA // N_CG
CH_PER_CG = G_PER_CG * S5_GROUP
ST_PER_CG = G_PER_CG * P_A
FF_CHUNKS = ((0, 6 * MXU_N), (6 * MXU_N, D_FF))
FFN_TM = 512
ODD_TM = 512
CONV_ROWS = 8 * BATCH
VMEM_LIMIT = 56 * 1024 * 1024


def _ln(x, g, b):
    mu = jnp.mean(x, axis=-1, keepdims=True)
    xc = x - mu
    var = jnp.mean(xc * xc, axis=-1, keepdims=True)
    return xc * lax.rsqrt(var + LN_EPS) * g + b


def _dot(a, b):
    return jnp.dot(a, b, preferred_element_type=F32)


def _whole(arr):
    nd = arr.ndim
    return pl.BlockSpec(arr.shape, lambda *_: (0,) * nd, pipeline_mode=pl.Buffered(1))


def _layer(arr, l):
    nd = arr.ndim
    return pl.BlockSpec((None,) + arr.shape[1:], lambda *_: (l,) + (0,) * (nd - 1),
                        pipeline_mode=pl.Buffered(1))


def _rows3(v):
    return v.reshape(v.shape[0], 1, v.shape[1])


def _params(n_grid=1):
    return pltpu.CompilerParams(dimension_semantics=("arbitrary",) * n_grid,
                                vmem_limit_bytes=VMEM_LIMIT)


def _s5_prep_kernel(lr_ref, li_ref, ldt_ref, br_ref, bi_ref, ci_ref,
                    lbr_ref, lbi_ref, bbr_ref, bbi_ref, cneg_ref):
    lr = lr_ref[...]
    li = li_ref[...]
    dt = jnp.exp(ldt_ref[...])
    mag = jnp.exp(lr * dt)
    lbr = mag * jnp.cos(li * dt)
    lbi = mag * jnp.sin(li * dt)
    lbr_ref[...] = lbr
    lbi_ref[...] = lbi
    nr = lbr - 1.0
    ni = lbi
    den = lr * lr + li * li
    qr = (nr * lr + ni * li) / den
    qi = (ni * lr - nr * li) / den
    br = br_ref[...]
    bi = bi_ref[...]
    qr3 = qr[:, None, :]
    qi3 = qi[:, None, :]
    bbr_ref[...] = qr3 * br - qi3 * bi
    bbi_ref[...] = qr3 * bi + qi3 * br
    cneg_ref[...] = -ci_ref[...]


def _s5_prep(lam_re, lam_im, log_dt, b_re, b_im, c_im):
    gp = jax.ShapeDtypeStruct((N_GA, P_A), F32)
    gcp = jax.ShapeDtypeStruct((N_GA, S5_GROUP, P_A), F32)
    return pl.pallas_call(
        _s5_prep_kernel,
        out_shape=(gp, gp, gcp, gcp, gcp),
        name="s5_prep",
    )(lam_re, lam_im, log_dt.reshape(N_GA, 1), b_re, b_im, c_im)


def _block_diag_params(lbr, lbi, bbr, bbi, c_re, cneg):
    eye = jnp.eye(G_PER_CG, dtype=F32)

    def in_mat(b):
        b4 = b.reshape(N_CG, G_PER_CG, S5_GROUP, P_A)
        return jnp.einsum('jgcp,gh->jgchp', b4, eye).reshape(N_CG, CH_PER_CG, ST_PER_CG)

    def out_mat(c):
        c4 = c.reshape(N_CG, G_PER_CG, S5_GROUP, P_A)
        return jnp.einsum('jgcp,gh->jgphc', c4, eye).reshape(N_CG, ST_PER_CG, CH_PER_CG)

    bbd = jnp.concatenate([in_mat(bbr), in_mat(bbi)], axis=2).astype(BF16)
    cbd = jnp.concatenate([out_mat(c_re), out_mat(cneg)], axis=1).astype(BF16)
    lam = jnp.stack([lbr.reshape(-1), lbi.reshape(-1)])
    return bbd, cbd, lam


def _even_prompt_kernel(x_ref, win_ref, bbd_ref, cbd_ref, lam_ref, d_ref, gluw_ref, glub_ref,
                        sg_ref, sb_ref, sw_ref, sbt_ref, wout_ref, lng_ref, lnb_ref,
                        o_ref, sre_ref, sim_ref,
                        st_ref, xt_ref, scr_ref, cat_ref, vn_ref, gt_ref):
    @pl.when(pl.program_id(0) == 0)
    def _():
        st_ref[...] = jnp.zeros_like(st_ref)

    for c in range(D_MODEL // LANES):
        for b in range(BATCH):
            xt_ref.at[c][pl.ds(b, CHUNK, stride=BATCH), :] = x_ref[b, :, LANES * c:LANES * (c + 1)]
    xb = jnp.concatenate([xt_ref[c] for c in range(D_MODEL // LANES)], axis=-1)
    xbf = xb.astype(BF16)

    ua = _dot(xbf, win_ref[:, 0:D_A])
    uab = ua.astype(BF16)
    ys = []
    for j in range(N_CG):
        re_cols = slice(2 * ST_PER_CG * j, 2 * ST_PER_CG * j + ST_PER_CG)
        im_cols = slice(2 * ST_PER_CG * j + ST_PER_CG, 2 * ST_PER_CG * (j + 1))
        scr_ref[...] = _dot(uab[:, CH_PER_CG * j:CH_PER_CG * (j + 1)], bbd_ref[j])
        lr = jnp.broadcast_to(lam_ref[0:1, ST_PER_CG * j:ST_PER_CG * (j + 1)], (BATCH, ST_PER_CG))
        li = jnp.broadcast_to(lam_ref[1:2, ST_PER_CG * j:ST_PER_CG * (j + 1)], (BATCH, ST_PER_CG))

        def step(t, carry, lr=lr, li=li):
            sr, si = carry
            row = pl.multiple_of(t * BATCH, BATCH)
            nr = lr * sr - li * si + scr_ref[pl.ds(row, BATCH), 0:ST_PER_CG]
            ni = lr * si + li * sr + scr_ref[pl.ds(row, BATCH), ST_PER_CG:2 * ST_PER_CG]
            scr_ref[pl.ds(row, BATCH), 0:ST_PER_CG] = nr
            scr_ref[pl.ds(row, BATCH), ST_PER_CG:2 * ST_PER_CG] = ni
            return nr, ni

        sr, si = lax.fori_loop(0, CHUNK, step, (st_ref[:, re_cols], st_ref[:, im_cols]), unroll=4)
        st_ref[:, re_cols] = sr
        st_ref[:, im_cols] = si
        sre_ref[:, ST_PER_CG * j:ST_PER_CG * (j + 1)] = sr
        sim_ref[:, ST_PER_CG * j:ST_PER_CG * (j + 1)] = si
        ys.append(_dot(scr_ref[...].astype(BF16), cbd_ref[j]))
    y = jnp.concatenate(ys, axis=-1) + d_ref[...] * ua
    g = jax.nn.gelu(y, approximate=True)
    ya = g * jax.nn.sigmoid(_dot(g.astype(BF16), gluw_ref[...]) + glub_ref[...])
    cat_ref[:, 0:D_A] = ya.astype(BF16)

    ub = _dot(xbf, win_ref[:, D_A:D_A + D_B])
    vb = _dot(xbf, win_ref[:, D_A + D_B:D_A + 2 * D_B])
    vn = _ln(vb, sg_ref[...], sb_ref[...])
    for h in range(N_HB):
        vn_ref[h] = vn[:, HD_B * h:HD_B * (h + 1)]
    r_id = lax.broadcasted_iota(jnp.int32, (CHUNK, CHUNK), 0)
    c_id = lax.broadcasted_iota(jnp.int32, (CHUNK, CHUNK), 1)
    tril = (c_id <= r_id).astype(F32)
    wm = [(sw_ref[h] * tril).astype(BF16) for h in range(N_HB)]
    for b in range(0, BATCH, 2):
        for h in range(N_HB):
            v2 = jnp.concatenate([vn_ref.at[h][pl.ds(b, CHUNK, stride=BATCH), :],
                                  vn_ref.at[h][pl.ds(b + 1, CHUNK, stride=BATCH), :]], axis=-1)
            g2 = _dot(wm[h], v2.astype(BF16)) + sbt_ref[:, h:h + 1]
            gt_ref.at[h][pl.ds(b, CHUNK, stride=BATCH), :] = g2[:, 0:HD_B]
            gt_ref.at[h][pl.ds(b + 1, CHUNK, stride=BATCH), :] = g2[:, HD_B:]
    gate = jnp.concatenate([gt_ref[h] for h in range(N_HB)], axis=-1)
    cat_ref[:, D_A:] = (ub * gate).astype(BF16)

    out = _dot(cat_ref[...], wout_ref[...])
    o_ref[...] = _ln(ALPHA * xb + out, lng_ref[...], lnb_ref[...])


def _even_prompt(x, e, l, win, bbd, cbd, lam, d3, gluw, glub3, sg3, sb3, sw, sbt, wout, lng3, lnb3):
    rows = ROWS_PER_CHUNK
    n = SEQ // CHUNK
    in_specs = [pl.BlockSpec((BATCH, CHUNK, D_MODEL), lambda i: (0, i, 0)),
                _layer(win, e), _whole(bbd), _whole(cbd), _whole(lam), _layer(d3, e), _layer(gluw, e),
                _layer(glub3, e), _layer(sg3, e), _layer(sb3, e), _layer(sw, e), _layer(sbt, e),
                _layer(wout, e), _layer(lng3, l), _layer(lnb3, l)]
    st = jax.ShapeDtypeStruct((BATCH, N_GA * P_A), F32)
    st_spec = pl.BlockSpec((BATCH, N_GA * P_A), lambda i: (0, 0))
    return pl.pallas_call(
        _even_prompt_kernel,
        grid=(n,),
        in_specs=in_specs,
        out_specs=(pl.BlockSpec((rows, D_MODEL), lambda i: (i, 0)), st_spec, st_spec),
        out_shape=(jax.ShapeDtypeStruct((M_PROMPT, D_MODEL), F32), st, st),
        scratch_shapes=[pltpu.VMEM((BATCH, 2 * N_GA * P_A), F32),
                        pltpu.VMEM((D_MODEL // LANES, rows, LANES), F32),
                        pltpu.VMEM((rows, 2 * ST_PER_CG), F32),
                        pltpu.VMEM((rows, D_MODEL), BF16),
                        pltpu.VMEM((N_HB, rows, HD_B), F32),
                        pltpu.VMEM((N_HB, rows, HD_B), F32)],
        compiler_params=_params(),
        name="even_prompt",
    )(x, win, bbd, cbd, lam, d3, gluw, glub3, sg3, sb3, sw, sbt, wout, lng3, lnb3)


def _even_sample_kernel(x_ref, s0r_ref, s0i_ref, win_ref, bbd_ref, cbd_ref, lam_ref, d_ref,
                        gluw_ref, glub_ref, sg_ref, sb_ref, sw0_ref, sb0_ref, wout_ref, lng_ref, lnb_ref,
                        o_ref, sre_ref, sim_ref, vn_ref):
    xb = x_ref[...]
    xbf = xb.astype(BF16)
    ua = _dot(xbf, win_ref[:, 0:D_A])
    uab = ua.astype(BF16)
    ys = []
    for j in range(N_CG):
        cols = slice(ST_PER_CG * j, ST_PER_CG * (j + 1))
        bu = _dot(uab[:, CH_PER_CG * j:CH_PER_CG * (j + 1)], bbd_ref[j])
        lr = lam_ref[0:1, cols]
        li = lam_ref[1:2, cols]
        s0r = s0r_ref[:, cols]
        s0i = s0i_ref[:, cols]
        sr = lr * s0r - li * s0i + bu[:, 0:ST_PER_CG]
        si = lr * s0i + li * s0r + bu[:, ST_PER_CG:]
        sre_ref[:, cols] = sr
        sim_ref[:, cols] = si
        ys.append(_dot(jnp.concatenate([sr, si], axis=-1).astype(BF16), cbd_ref[j]))
    y = jnp.concatenate(ys, axis=-1) + d_ref[...] * ua
    g = jax.nn.gelu(y, approximate=True)
    ya = g * jax.nn.sigmoid(_dot(g.astype(BF16), gluw_ref[...]) + glub_ref[...])

    ub = _dot(xbf, win_ref[:, D_A:D_A + D_B])
    vb = _dot(xbf, win_ref[:, D_A + D_B:D_A + 2 * D_B])
    vn = _ln(vb, sg_ref[...], sb_ref[...])
    vn_ref[...] = vn
    yb = ub * (sw0_ref[...] * vn + sb0_ref[...])

    cat = jnp.concatenate([ya, yb], axis=-1).astype(BF16)
    out = _dot(cat, wout_ref[...])
    o_ref[...] = _ln(ALPHA * xb + out, lng_ref[...], lnb_ref[...])


def _even_sample(x, s0r, s0i, e, l, win, bbd, cbd, lam, d3, gluw, glub3, sg3, sb3, sw0, sb0, wout, lng3, lnb3):
    in_specs = [_whole(x), _whole(s0r), _whole(s0i),
                _layer(win, e), _whole(bbd), _whole(cbd), _whole(lam), _layer(d3, e), _layer(gluw, e),
                _layer(glub3, e), _layer(sg3, e), _layer(sb3, e), _whole(sw0), _whole(sb0),
                _layer(wout, e), _layer(lng3, l), _layer(lnb3, l)]
    shapes = ((DEC_BATCH, D_MODEL), (DEC_BATCH, N_GA * P_A), (DEC_BATCH, N_GA * P_A), (DEC_BATCH, D_B))
    return pl.pallas_call(
        _even_sample_kernel,
        grid=(1,),
        in_specs=in_specs,
        out_specs=tuple(pl.BlockSpec(s, lambda i: (0, 0)) for s in shapes),
        out_shape=tuple(jax.ShapeDtypeStruct(s, F32) for s in shapes),
        compiler_params=_params(),
        name="even_sample",
    )(x, s0r, s0i, win, bbd, cbd, lam, d3, gluw, glub3, sg3, sb3, sw0, sb0, wout, lng3, lnb3)


def _store_batch_major(y, o_ref, yt_ref, tm):
    for c in range(D_MODEL // LANES):
        yt_ref[c] = y[:, LANES * c:LANES * (c + 1)]
    for c in range(D_MODEL // LANES):
        for b in range(BATCH):
            o_ref[b, :, LANES * c:LANES * (c + 1)] = yt_ref.at[c][pl.ds(b, tm // BATCH, stride=BATCH), :]


def _ffn_kernel(x_ref, halo_ref, wg_ref, wu_ref, cw_ref, cb_ref, wd_ref, lng_ref, lnb_ref,
                o_ref, cache_ref, gs_ref, *rest, tm, shift, batch_major_out):
    hs = (K_F - 1) * shift

    @pl.when(pl.program_id(0) == 0)
    def _():
        gs_ref[0:hs, :] = halo_ref[...]

    xb = x_ref[...]
    xbf = xb.astype(BF16)
    acc = None
    for c0, c1 in FF_CHUNKS:
        gs_ref[hs:hs + tm, c0:c1] = _dot(xbf, wg_ref[:, c0:c1])
        conv = cb_ref[:, c0:c1]
        for k in range(K_F):
            conv = conv + cw_ref[k:k + 1, c0:c1] * gs_ref[k * shift:k * shift + tm, c0:c1]
        up = _dot(xbf, wu_ref[:, c0:c1])
        h = (jax.nn.silu(conv) * up).astype(BF16)
        part = _dot(h, wd_ref[c0:c1, :])
        acc = part if acc is None else acc + part
    y = _ln(ALPHA * xb + acc, lng_ref[...], lnb_ref[...])
    if batch_major_out:
        _store_batch_major(y, o_ref, rest[0], tm)
    else:
        o_ref[...] = y
    tail = gs_ref[tm:tm + hs, :]
    cache_ref[...] = tail
    gs_ref[0:hs, :] = tail


def _ffn(x, halo, l, wg, wu, cw, cb3, wd, lng3, lnb3, *, tm, shift, batch_major_out=False):
    m = x.shape[0]
    hs = (K_F - 1) * shift
    in_specs = [pl.BlockSpec((tm, D_MODEL), lambda i: (i, 0)), _whole(halo),
                _layer(wg, l), _layer(wu, l), _layer(cw, l), _layer(cb3, l), _layer(wd, l),
                _layer(lng3, l), _layer(lnb3, l)]
    scratch = [pltpu.VMEM((hs + tm, D_FF), F32)]
    if batch_major_out:
        tpos = tm // BATCH
        o_spec = pl.BlockSpec((BATCH, tpos, D_MODEL), lambda i: (0, i, 0))
        o_shape = jax.ShapeDtypeStruct((BATCH, m // BATCH, D_MODEL), F32)
        scratch.append(pltpu.VMEM((D_MODEL // LANES, tm, LANES), F32))
    else:
        o_spec = pl.BlockSpec((tm, D_MODEL), lambda i: (i, 0))
        o_shape = jax.ShapeDtypeStruct((m, D_MODEL), F32)
    return pl.pallas_call(
        functools.partial(_ffn_kernel, tm=tm, shift=shift, batch_major_out=batch_major_out),
        grid=(m // tm,),
        in_specs=in_specs,
        out_specs=(o_spec, pl.BlockSpec((hs, D_FF), lambda i: (0, 0))),
        out_shape=(o_shape, jax.ShapeDtypeStruct((hs, D_FF), F32)),
        scratch_shapes=scratch,
        compiler_params=_params(),
        name="conv_ffn",
    )(x, halo, wg, wu, cw, cb3, wd, lng3, lnb3)


def _odd_prompt_kernel(x_ref, win_ref, cw_ref, cb_ref, lcg_ref, lcb_ref, wout_ref, lng_ref, lnb_ref,
                       o_ref, cache_ref, gs_ref, hc_ref, *, tm):
    hs = (K_C - 1) * BATCH

    @pl.when(pl.program_id(0) == 0)
    def _():
        gs_ref[0:hs, :] = jnp.zeros((hs, D_C), F32)

    xb = x_ref[...]
    xbf = xb.astype(BF16)
    z1 = _dot(xbf, win_ref[:, 0:D_C])
    z2 = _dot(xbf, win_ref[:, D_C:2 * D_C])
    gs_ref[hs:hs + tm, :] = z1 * jax.nn.sigmoid(z2)

    def conv_rows(i, _):
        r0 = pl.multiple_of(i * CONV_ROWS, CONV_ROWS)
        for l0 in range(0, D_C, LANES):
            cols = slice(l0, l0 + LANES)
            win = gs_ref[pl.ds(r0, CONV_ROWS + hs), cols]
            acc = jnp.broadcast_to(cb_ref[:, cols], (CONV_ROWS, LANES))
            for k in range(K_C):
                acc = acc + cw_ref[k:k + 1, cols] * win[k * BATCH:k * BATCH + CONV_ROWS, :]
            hc_ref[pl.ds(r0, CONV_ROWS), cols] = acc
        return 0

    lax.fori_loop(0, tm // CONV_ROWS, conv_rows, 0)
    h = jax.nn.silu(_ln(hc_ref[...], lcg_ref[...], lcb_ref[...]))
    out = _dot(h.astype(BF16), wout_ref[...])
    o_ref[...] = _ln(ALPHA * xb + out, lng_ref[...], lnb_ref[...])
    tail = gs_ref[tm:tm + hs, :]
    cache_ref[...] = tail
    gs_ref[0:hs, :] = tail


def _odd_prompt(x, o, l, win, cw, cb3, lcg3, lcb3, wout, lng3, lnb3, *, tm):
    hs = (K_C - 1) * BATCH
    in_specs = [pl.BlockSpec((tm, D_MODEL), lambda i: (i, 0)),
                _layer(win, o), _layer(cw, o), _layer(cb3, o), _layer(lcg3, o), _layer(lcb3, o),
                _layer(wout, o), _layer(lng3, l), _layer(lnb3, l)]
    return pl.pallas_call(
        functools.partial(_odd_prompt_kernel, tm=tm),
        grid=(M_PROMPT // tm,),
        in_specs=in_specs,
        out_specs=(pl.BlockSpec((tm, D_MODEL), lambda i: (i, 0)), pl.BlockSpec((hs, D_C), lambda i: (0, 0))),
        out_shape=(jax.ShapeDtypeStruct((M_PROMPT, D_MODEL), F32), jax.ShapeDtypeStruct((hs, D_C), F32)),
        scratch_shapes=[pltpu.VMEM((hs + tm, D_C), F32), pltpu.VMEM((tm, D_C), F32)],
        compiler_params=_params(),
        name="odd_prompt",
    )(x, win, cw, cb3, lcg3, lcb3, wout, lng3, lnb3)


def _odd_sample_kernel(x_ref, c_ref, win_ref, cw_ref, cb_ref, lcg_ref, lcb_ref, wout_ref, lng_ref, lnb_ref,
                       o_ref, oc_ref, g_ref, pc_ref):
    k = pl.program_id(0)
    nk = K_C - 1

    @pl.when(k == 0)
    def _():
        xbf = x_ref[...].astype(BF16)
        z1 = _dot(xbf, win_ref[:, 0:D_C])
        z2 = _dot(xbf, win_ref[:, D_C:2 * D_C])
        g_ref[...] = z1 * jax.nn.sigmoid(z2)
        pc_ref[...] = jnp.broadcast_to(cb_ref[...], pc_ref.shape)

    @pl.when(k < nk)
    def _():
        c = c_ref[...]
        pc_ref[...] += cw_ref[pl.ds(k, 1), :] * c
        oc_ref[...] = c

    @pl.when(k == nk)
    def _():
        g = g_ref[...]
        oc_ref[...] = g
        hc = pc_ref[...] + cw_ref[nk:nk + 1, :] * g
        h = jax.nn.silu(_ln(hc, lcg_ref[...], lcb_ref[...]))
        out = _dot(h.astype(BF16), wout_ref[...])
        o_ref[...] = _ln(ALPHA * x_ref[...] + out, lng_ref[...], lnb_ref[...])


def _odd_sample(x, cache_t, o, l, win, cw, cb3, lcg3, lcb3, wout, lng3, lnb3):
    nk = K_C - 1
    in_specs = [_whole(x),
                pl.BlockSpec((None, None, DEC_BATCH, D_C), lambda k: (o, jnp.minimum(k, nk - 1), 0, 0)),
                _layer(win, o), _layer(cw, o), _layer(cb3, o), _layer(lcg3, o), _layer(lcb3, o),
                _layer(wout, o), _layer(lng3, l), _layer(lnb3, l)]
    return pl.pallas_call(
        _odd_sample_kernel,
        grid=(nk + 1,),
        in_specs=in_specs,
        out_specs=(pl.BlockSpec((DEC_BATCH, D_MODEL), lambda k: (0, 0)),
                   pl.BlockSpec((None, DEC_BATCH, D_C), lambda k: (jnp.maximum(k - 1, 0), 0, 0))),
        out_shape=(jax.ShapeDtypeStruct((DEC_BATCH, D_MODEL), F32),
                   jax.ShapeDtypeStruct((nk, DEC_BATCH, D_C), F32)),
        scratch_shapes=[pltpu.VMEM((DEC_BATCH, D_C), F32), pltpu.VMEM((DEC_BATCH, D_C), F32)],
        compiler_params=_params(),
        name="odd_sample",
    )(x, cache_t, win, cw, cb3, lcg3, lcb3, wout, lng3, lnb3)


def kernel(x_prompt, x_sample, state_a_re, state_a_im, cache_c_conv, cache_ffn_conv, w_in_ab, s5_lam_re, s5_lam_im, s5_log_dt, s5_b_re, s5_b_im, s5_c_re, s5_c_im, s5_d, s5_glu_w, s5_glu_b, sgu_ln_g, sgu_ln_b, sgu_w, sgu_b, w_out_ab, w_in_c, conv_c_w, conv_c_b, ln_c_g, ln_c_b, w_out_c, ffn_w_gate, ffn_w_up, ffn_conv_w, ffn_conv_b, ffn_w_down, ln_mix_g, ln_mix_b, ln_ffn_g, ln_ffn_b):
    xs = x_sample.reshape(DEC_BATCH, D_MODEL)
    xp = x_prompt

    win_ab, wout_ab, gluw = w_in_ab.astype(BF16), w_out_ab.astype(BF16), s5_glu_w.astype(BF16)
    win_c, wout_c = w_in_c.astype(BF16), w_out_c.astype(BF16)
    cache_c_t = jnp.transpose(cache_c_conv, (0, 2, 1, 3))
    wg, wu, wd = ffn_w_gate.astype(BF16), ffn_w_up.astype(BF16), ffn_w_down.astype(BF16)
    fcb3 = _rows3(ffn_conv_b)
    d3, glub3, sg3, sb3 = _rows3(s5_d), _rows3(s5_glu_b), _rows3(sgu_ln_g), _rows3(sgu_ln_b)
    sbt = jnp.swapaxes(sgu_b, 1, 2)
    ccb3, lcg3, lcb3 = _rows3(conv_c_b), _rows3(ln_c_g), _rows3(ln_c_b)
    lmg3, lmb3, lfg3, lfb3 = _rows3(ln_mix_g), _rows3(ln_mix_b), _rows3(ln_ffn_g), _rows3(ln_ffn_b)
    halo_p = jnp.zeros(((K_F - 1) * BATCH, D_FF), F32)

    sa_re_p, sa_im_p, sa_re_s, sa_im_s, sb_v_s = [], [], [], [], []
    cc_p, cc_s, cf_p, cf_s = [], [], [], []
    for l in range(DEPTH):
        if l % 2 == 0:
            e = l // 2
            lbr, lbi, bbr, bbi, cneg = _s5_prep(
                s5_lam_re[e], s5_lam_im[e], s5_log_dt[e],
                jnp.swapaxes(s5_b_re[e], 1, 2), jnp.swapaxes(s5_b_im[e], 1, 2), s5_c_im[e])
            bbd, cbd, lam = _block_diag_params(lbr, lbi, bbr, bbi, s5_c_re[e], cneg)
            if l > 0:
                xp = jnp.transpose(xp.reshape(SEQ, BATCH, D_MODEL), (1, 0, 2))
            shared = (win_ab, bbd, cbd, lam, d3, gluw, glub3, sg3, sb3)
            xp, sre, sim = _even_prompt(xp, e, l, *shared, sgu_w, sbt, wout_ab, lmg3, lmb3)
            sw0 = jnp.repeat(sgu_w[e][:, 0, 0], HD_B).reshape(1, D_B)
            sb0 = jnp.repeat(sgu_b[e][:, 0], HD_B).reshape(1, D_B)
            xs, sres, sims, vn = _even_sample(
                xs, state_a_re[e].reshape(DEC_BATCH, -1), state_a_im[e].reshape(DEC_BATCH, -1),
                e, l, *shared, sw0, sb0, wout_ab, lmg3, lmb3)
            sa_re_p.append(sre.reshape(BATCH, N_GA, P_A))
            sa_im_p.append(sim.reshape(BATCH, N_GA, P_A))
            sa_re_s.append(sres.reshape(DEC_BATCH, N_GA, P_A))
            sa_im_s.append(sims.reshape(DEC_BATCH, N_GA, P_A))
            sb_v_s.append(vn.reshape(DEC_BATCH, 1, D_B))
        else:
            o = l // 2
            rest = (win_c, conv_c_w, ccb3, lcg3, lcb3, wout_c, lmg3, lmb3)
            xp, cp = _odd_prompt(xp, o, l, *rest, tm=ODD_TM)
            xs, cs_t = _odd_sample(xs, cache_c_t, o, l, *rest)
            cc_p.append(jnp.transpose(cp.reshape(K_C - 1, BATCH, D_C), (1, 0, 2)))
            cc_s.append(jnp.transpose(cs_t, (1, 0, 2)))
        frest = (wg, wu, ffn_conv_w, fcb3, wd, lfg3, lfb3)
        xp, fcp = _ffn(xp, halo_p, l, *frest, tm=FFN_TM, shift=BATCH, batch_major_out=(l == DEPTH - 1))
        halo_s = jnp.transpose(cache_ffn_conv[l], (1, 0, 2)).reshape((K_F - 1) * DEC_BATCH, D_FF)
        xs, fcs = _ffn(xs, halo_s, l, *frest, tm=DEC_BATCH, shift=DEC_BATCH)
        cf_p.append(jnp.transpose(fcp.reshape(K_F - 1, BATCH, D_FF), (1, 0, 2)))
        cf_s.append(jnp.transpose(fcs.reshape(K_F - 1, DEC_BATCH, D_FF), (1, 0, 2)))

    ys = xs.reshape(DEC_BATCH, 1, D_MODEL)
    return (xp, ys,
            jnp.stack(sa_re_p), jnp.stack(sa_im_p), jnp.stack(sa_re_s), jnp.stack(sa_im_s),
            jnp.stack(sb_v_s),
            jnp.stack(cc_p), jnp.stack(cc_s),
            jnp.stack(cf_p), jnp.stack(cf_s))
```

```python
import functools

import jax
import jax.numpy as jnp
from jax import lax
from jax.experimental import pallas as pl
from jax.experimental.pallas import tpu as pltpu

D_MODEL = 1024
BATCH = 8
SEQ = 2048
DEPTH = 2
DEC_BATCH = 128
N_EVEN = (DEPTH + 1) // 2
N_ODD = DEPTH // 2
D_A = D_MODEL // 2
S5_GROUP = 16
N_GA = D_A // S5_GROUP
P_A = 64
D_B = D_MODEL // 2
N_HB = 4
HD_B = D_B // N_HB
CHUNK = 128
D_C = D_MODEL
K_C = 31
D_FF = ((8 * D_MODEL) // 3 + 127) // 128 * 128
K_F = 3
ALPHA = (2.0 * DEPTH) ** 0.25
LN_EPS = 1e-5

F32 = jnp.float32
BF16 = jnp.bfloat16

LANES = 128
MXU_N = 256
M_PROMPT = BATCH * SEQ
ROWS_PER_CHUNK = CHUNK * BATCH
N_CG = 4
G_PER_CG = N_GA // N_CG
CH_PER_CG = G_PER_CG * S5_GROUP
ST_PER_CG = G_PER_CG * P_A
FF_CHUNKS = ((0, 6 * MXU_N), (6 * MXU_N, D_FF))
FFN_TM = 512
ODD_TM = 1024
CONV_ROWS = 8 * BATCH
SAMPLE_TAPS = 5
VMEM_LIMIT = 56 * 1024 * 1024


def _ln(x, g, b):
    mu = jnp.mean(x, axis=-1, keepdims=True)
    xc = x - mu
    var = jnp.mean(xc * xc, axis=-1, keepdims=True)
    return xc * lax.rsqrt(var + LN_EPS) * g + b


def _dot(a, b):
    return jnp.dot(a, b, preferred_element_type=F32)


def _whole(arr):
    nd = arr.ndim
    return pl.BlockSpec(arr.shape, lambda *_: (0,) * nd, pipeline_mode=pl.Buffered(1))


def _layer(arr, l):
    nd = arr.ndim
    return pl.BlockSpec((None,) + arr.shape[1:], lambda *_: (l,) + (0,) * (nd - 1),
                        pipeline_mode=pl.Buffered(1))


def _rows3(v):
    return v.reshape(v.shape[0], 1, v.shape[1])


def _params(n_grid=1):
    return pltpu.CompilerParams(dimension_semantics=("arbitrary",) * n_grid,
                                vmem_limit_bytes=VMEM_LIMIT)


def _s5_prep_kernel(lr_ref, li_ref, ldt_ref, br_ref, bi_ref, ci_ref,
                    lbr_ref, lbi_ref, bbr_ref, bbi_ref, cneg_ref):
    lr = lr_ref[...]
    li = li_ref[...]
    dt = jnp.exp(ldt_ref[...])
    mag = jnp.exp(lr * dt)
    lbr = mag * jnp.cos(li * dt)
    lbi = mag * jnp.sin(li * dt)
    lbr_ref[...] = lbr
    lbi_ref[...] = lbi
    nr = lbr - 1.0
    ni = lbi
    den = lr * lr + li * li
    qr = (nr * lr + ni * li) / den
    qi = (ni * lr - nr * li) / den
    br = br_ref[...]
    bi = bi_ref[...]
    qr3 = qr[:, None, :]
    qi3 = qi[:, None, :]
    bbr_ref[...] = qr3 * br - qi3 * bi
    bbi_ref[...] = qr3 * bi + qi3 * br
    cneg_ref[...] = -ci_ref[...]


def _s5_prep(lam_re, lam_im, log_dt, b_re, b_im, c_im):
    gp = jax.ShapeDtypeStruct((N_GA, P_A), F32)
    gcp = jax.ShapeDtypeStruct((N_GA, S5_GROUP, P_A), F32)
    return pl.pallas_call(
        _s5_prep_kernel,
        out_shape=(gp, gp, gcp, gcp, gcp),
        name="s5_prep",
    )(lam_re, lam_im, log_dt.reshape(N_GA, 1), b_re, b_im, c_im)


def _block_diag_params(lbr, lbi, bbr, bbi, c_re, cneg):
    eye = jnp.eye(G_PER_CG, dtype=F32)

    def in_mat(b):
        b4 = b.reshape(N_CG, G_PER_CG, S5_GROUP, P_A)
        return jnp.einsum('jgcp,gh->jgchp', b4, eye).reshape(N_CG, CH_PER_CG, ST_PER_CG)

    def out_mat(c):
        c4 = c.reshape(N_CG, G_PER_CG, S5_GROUP, P_A)
        return jnp.einsum('jgcp,gh->jgphc', c4, eye).reshape(N_CG, ST_PER_CG, CH_PER_CG)

    bbd = jnp.concatenate([in_mat(bbr), in_mat(bbi)], axis=2).astype(BF16)
    cbd = jnp.concatenate([out_mat(c_re), out_mat(cneg)], axis=1).astype(BF16)
    lam = jnp.stack([lbr.reshape(-1), lbi.reshape(-1)])
    return bbd, cbd, lam


def _even_prompt_kernel(x_ref, win_ref, bbd_ref, cbd_ref, lam_ref, d_ref, gluw_ref, glub_ref,
                        sg_ref, sb_ref, sw_ref, sbt_ref, wout_ref, lng_ref, lnb_ref,
                        o_ref, sre_ref, sim_ref,
                        st_ref, xt_ref, scr_ref, cat_ref, vn_ref, gt_ref):
    @pl.when(pl.program_id(0) == 0)
    def _():
        st_ref[...] = jnp.zeros_like(st_ref)

    for c in range(D_MODEL // LANES):
        for b in range(BATCH):
            xt_ref.at[c][pl.ds(b, CHUNK, stride=BATCH), :] = x_ref[b, :, LANES * c:LANES * (c + 1)]
    xb = jnp.concatenate([xt_ref[c] for c in range(D_MODEL // LANES)], axis=-1)
    xbf = xb.astype(BF16)

    ua = _dot(xbf, win_ref[:, 0:D_A])
    uab = ua.astype(BF16)
    ys = []
    for j in range(N_CG):
        re_cols = slice(2 * ST_PER_CG * j, 2 * ST_PER_CG * j + ST_PER_CG)
        im_cols = slice(2 * ST_PER_CG * j + ST_PER_CG, 2 * ST_PER_CG * (j + 1))
        scr_ref[...] = _dot(uab[:, CH_PER_CG * j:CH_PER_CG * (j + 1)], bbd_ref[j])
        lr = jnp.broadcast_to(lam_ref[0:1, ST_PER_CG * j:ST_PER_CG * (j + 1)], (BATCH, ST_PER_CG))
        li = jnp.broadcast_to(lam_ref[1:2, ST_PER_CG * j:ST_PER_CG * (j + 1)], (BATCH, ST_PER_CG))

        def step(t, carry, lr=lr, li=li):
            sr, si = carry
            row = pl.multiple_of(t * BATCH, BATCH)
            nr = lr * sr - li * si + scr_ref[pl.ds(row, BATCH), 0:ST_PER_CG]
            ni = lr * si + li * sr + scr_ref[pl.ds(row, BATCH), ST_PER_CG:2 * ST_PER_CG]
            scr_ref[pl.ds(row, BATCH), 0:ST_PER_CG] = nr
            scr_ref[pl.ds(row, BATCH), ST_PER_CG:2 * ST_PER_CG] = ni
            return nr, ni

        sr, si = lax.fori_loop(0, CHUNK, step, (st_ref[:, re_cols], st_ref[:, im_cols]), unroll=4)
        st_ref[:, re_cols] = sr
        st_ref[:, im_cols] = si
        sre_ref[:, ST_PER_CG * j:ST_PER_CG * (j + 1)] = sr
        sim_ref[:, ST_PER_CG * j:ST_PER_CG * (j + 1)] = si
        ys.append(_dot(scr_ref[...].astype(BF16), cbd_ref[j]))
    y = jnp.concatenate(ys, axis=-1) + d_ref[...] * ua
    g = jax.nn.gelu(y, approximate=True)
    ya = g * jax.nn.sigmoid(_dot(g.astype(BF16), gluw_ref[...]) + glub_ref[...])
    cat_ref[:, 0:D_A] = ya.astype(BF16)

    ub = _dot(xbf, win_ref[:, D_A:D_A + D_B])
    vb = _dot(xbf, win_ref[:, D_A + D_B:D_A + 2 * D_B])
    vn = _ln(vb, sg_ref[...], sb_ref[...])
    for h in range(N_HB):
        vn_ref[h] = vn[:, HD_B * h:HD_B * (h + 1)]
    r_id = lax.broadcasted_iota(jnp.int32, (CHUNK, CHUNK), 0)
    c_id = lax.broadcasted_iota(jnp.int32, (CHUNK, CHUNK), 1)
    tril = (c_id <= r_id).astype(F32)
    wm = [(sw_ref[h] * tril).astype(BF16) for h in range(N_HB)]
    for b in range(0, BATCH, 2):
        for h in range(N_HB):
            v2 = jnp.concatenate([vn_ref.at[h][pl.ds(b, CHUNK, stride=BATCH), :],
                                  vn_ref.at[h][pl.ds(b + 1, CHUNK, stride=BATCH), :]], axis=-1)
            g2 = _dot(wm[h], v2.astype(BF16)) + sbt_ref[:, h:h + 1]
            gt_ref.at[h][pl.ds(b, CHUNK, stride=BATCH), :] = g2[:, 0:HD_B]
            gt_ref.at[h][pl.ds(b + 1, CHUNK, stride=BATCH), :] = g2[:, HD_B:]
    gate = jnp.concatenate([gt_ref[h] for h in range(N_HB)], axis=-1)
    cat_ref[:, D_A:] = (ub * gate).astype(BF16)

    out = _dot(cat_ref[...], wout_ref[...])
    o_ref[...] = _ln(ALPHA * xb + out, lng_ref[...], lnb_ref[...])


def _even_prompt(x, e, l, win, bbd, cbd, lam, d3, gluw, glub3, sg3, sb3, sw, sbt, wout, lng3, lnb3):
    rows = ROWS_PER_CHUNK
    n = SEQ // CHUNK
    in_specs = [pl.BlockSpec((BATCH, CHUNK, D_MODEL), lambda i: (0, i, 0)),
                _layer(win, e), _whole(bbd), _whole(cbd), _whole(lam), _layer(d3, e), _layer(gluw, e),
                _layer(glub3, e), _layer(sg3, e), _layer(sb3, e), _layer(sw, e), _layer(sbt, e),
                _layer(wout, e), _layer(lng3, l), _layer(lnb3, l)]
    st = jax.ShapeDtypeStruct((BATCH, N_GA * P_A), F32)
    st_spec = pl.BlockSpec((BATCH, N_GA * P_A), lambda i: (0, 0))
    return pl.pallas_call(
        _even_prompt_kernel,
        grid=(n,),
        in_specs=in_specs,
        out_specs=(pl.BlockSpec((rows, D_MODEL), lambda i: (i, 0)), st_spec, st_spec),
        out_shape=(jax.ShapeDtypeStruct((M_PROMPT, D_MODEL), F32), st, st),
        scratch_shapes=[pltpu.VMEM((BATCH, 2 * N_GA * P_A), F32),
                        pltpu.VMEM((D_MODEL // LANES, rows, LANES), F32),
                        pltpu.VMEM((rows, 2 * ST_PER_CG), F32),
                        pltpu.VMEM((rows, D_MODEL), BF16),
                        pltpu.VMEM((N_HB, rows, HD_B), F32),
                        pltpu.VMEM((N_HB, rows, HD_B), F32)],
        compiler_params=_params(),
        name="even_prompt",
    )(x, win, bbd, cbd, lam, d3, gluw, glub3, sg3, sb3, sw, sbt, wout, lng3, lnb3)


def _even_sample_kernel(x_ref, s0r_ref, s0i_ref, win_ref, bbd_ref, cbd_ref, lam_ref, d_ref,
                        gluw_ref, glub_ref, sg_ref, sb_ref, sw0_ref, sb0_ref, wout_ref, lng_ref, lnb_ref,
                        o_ref, sre_ref, sim_ref, vn_ref):
    xb = x_ref[...]
    xbf = xb.astype(BF16)
    ua = _dot(xbf, win_ref[:, 0:D_A])
    uab = ua.astype(BF16)
    ys = []
    for j in range(N_CG):
        cols = slice(ST_PER_CG * j, ST_PER_CG * (j + 1))
        bu = _dot(uab[:, CH_PER_CG * j:CH_PER_CG * (j + 1)], bbd_ref[j])
        lr = lam_ref[0:1, cols]
        li = lam_ref[1:2, cols]
        s0r = s0r_ref[:, cols]
        s0i = s0i_ref[:, cols]
        sr = lr * s0r - li * s0i + bu[:, 0:ST_PER_CG]
        si = lr * s0i + li * s0r + bu[:, ST_PER_CG:]
        sre_ref[:, cols] = sr
        sim_ref[:, cols] = si
        ys.append(_dot(jnp.concatenate([sr, si], axis=-1).astype(BF16), cbd_ref[j]))
    y = jnp.concatenate(ys, axis=-1) + d_ref[...] * ua
    g = jax.nn.gelu(y, approximate=True)
    ya = g * jax.nn.sigmoid(_dot(g.astype(BF16), gluw_ref[...]) + glub_ref[...])

    ub = _dot(xbf, win_ref[:, D_A:D_A + D_B])
    vb = _dot(xbf, win_ref[:, D_A + D_B:D_A + 2 * D_B])
    vn = _ln(vb, sg_ref[...], sb_ref[...])
    vn_ref[...] = vn
    yb = ub * (sw0_ref[...] * vn + sb0_ref[...])

    cat = jnp.concatenate([ya, yb], axis=-1).astype(BF16)
    out = _dot(cat, wout_ref[...])
    o_ref[...] = _ln(ALPHA * xb + out, lng_ref[...], lnb_ref[...])


def _even_sample(x, s0r, s0i, e, l, win, bbd, cbd, lam, d3, gluw, glub3, sg3, sb3, sw0, sb0, wout, lng3, lnb3):
    in_specs = [_whole(x), _whole(s0r), _whole(s0i),
                _layer(win, e), _whole(bbd), _whole(cbd), _whole(lam), _layer(d3, e), _layer(gluw, e),
                _layer(glub3, e), _layer(sg3, e), _layer(sb3, e), _whole(sw0), _whole(sb0),
                _layer(wout, e), _layer(lng3, l), _layer(lnb3, l)]
    shapes = ((DEC_BATCH, D_MODEL), (DEC_BATCH, N_GA * P_A), (DEC_BATCH, N_GA * P_A), (DEC_BATCH, D_B))
    return pl.pallas_call(
        _even_sample_kernel,
        grid=(1,),
        in_specs=in_specs,
        out_specs=tuple(pl.BlockSpec(s, lambda i: (0, 0)) for s in shapes),
        out_shape=tuple(jax.ShapeDtypeStruct(s, F32) for s in shapes),
        compiler_params=_params(),
        name="even_sample",
    )(x, s0r, s0i, win, bbd, cbd, lam, d3, gluw, glub3, sg3, sb3, sw0, sb0, wout, lng3, lnb3)


def _store_batch_major(y, o_ref, yt_ref, tm):
    for c in range(D_MODEL // LANES):
        yt_ref[c] = y[:, LANES * c:LANES * (c + 1)]
    for c in range(D_MODEL // LANES):
        for b in range(BATCH):
            o_ref[b, :, LANES * c:LANES * (c + 1)] = yt_ref.at[c][pl.ds(b, tm // BATCH, stride=BATCH), :]


def _ffn_tile(xb, gs_ref, rows, shift, wg_ref, wu_ref, cw_ref, cb_ref, wd_ref, lng_ref, lnb_ref):
    hs = (K_F - 1) * shift
    xbf = xb.astype(BF16)
    acc = None
    for c0, c1 in FF_CHUNKS:
        gs_ref[hs:hs + rows, c0:c1] = _dot(xbf, wg_ref[:, c0:c1])
        conv = cb_ref[:, c0:c1]
        for k in range(K_F):
            conv = conv + cw_ref[k:k + 1, c0:c1] * gs_ref[k * shift:k * shift + rows, c0:c1]
        up = _dot(xbf, wu_ref[:, c0:c1])
        h = (jax.nn.silu(conv) * up).astype(BF16)
        part = _dot(h, wd_ref[c0:c1, :])
        acc = part if acc is None else acc + part
    return _ln(ALPHA * xb + acc, lng_ref[...], lnb_ref[...])


def _ffn_kernel(x_ref, xs_ref, halo_s_ref, wg_ref, wu_ref, cw_ref, cb_ref, wd_ref, lng_ref, lnb_ref,
                o_ref, cache_ref, os_ref, cache_s_ref, gs_ref, *rest, tm, n, batch_major_out):
    i = pl.program_id(0)
    hs = (K_F - 1) * BATCH
    hs_s = (K_F - 1) * DEC_BATCH
    weights = (wg_ref, wu_ref, cw_ref, cb_ref, wd_ref, lng_ref, lnb_ref)

    @pl.when(i == 0)
    def _():
        gs_ref[0:hs, :] = jnp.zeros((hs, D_FF), F32)

    @pl.when(i < n)
    def _():
        y = _ffn_tile(x_ref[...], gs_ref, tm, BATCH, *weights)
        if batch_major_out:
            _store_batch_major(y, o_ref, rest[0], tm)
        else:
            o_ref[...] = y
        tail = gs_ref[tm:tm + hs, :]
        cache_ref[...] = tail
        gs_ref[0:hs, :] = tail

    @pl.when(i == n)
    def _():
        gs_ref[0:hs_s, :] = halo_s_ref[...]
        os_ref[...] = _ffn_tile(xs_ref[...], gs_ref, DEC_BATCH, DEC_BATCH, *weights)
        cache_s_ref[...] = gs_ref[DEC_BATCH:DEC_BATCH + hs_s, :]


def _ffn(x, xs, halo_s, l, wg, wu, cw, cb3, wd, lng3, lnb3, *, tm, batch_major_out=False):
    m = x.shape[0]
    n = m // tm
    hs = (K_F - 1) * BATCH
    hs_s = (K_F - 1) * DEC_BATCH
    in_specs = [pl.BlockSpec((tm, D_MODEL), lambda i: (jnp.minimum(i, n - 1), 0)), _whole(xs), _whole(halo_s),
                _layer(wg, l), _layer(wu, l), _layer(cw, l), _layer(cb3, l), _layer(wd, l),
                _layer(lng3, l), _layer(lnb3, l)]
    scratch = [pltpu.VMEM((max(hs + tm, hs_s + DEC_BATCH), D_FF), F32)]
    if batch_major_out:
        o_spec = pl.BlockSpec((BATCH, tm // BATCH, D_MODEL), lambda i: (0, jnp.minimum(i, n - 1), 0))
        o_shape = jax.ShapeDtypeStruct((BATCH, m // BATCH, D_MODEL), F32)
        scratch.append(pltpu.VMEM((D_MODEL // LANES, tm, LANES), F32))
    else:
        o_spec = pl.BlockSpec((tm, D_MODEL), lambda i: (jnp.minimum(i, n - 1), 0))
        o_shape = jax.ShapeDtypeStruct((m, D_MODEL), F32)
    return pl.pallas_call(
        functools.partial(_ffn_kernel, tm=tm, n=n, batch_major_out=batch_major_out),
        grid=(n + 1,),
        in_specs=in_specs,
        out_specs=(o_spec, pl.BlockSpec((hs, D_FF), lambda i: (0, 0)),
                   pl.BlockSpec((DEC_BATCH, D_MODEL), lambda i: (0, 0)),
                   pl.BlockSpec((hs_s, D_FF), lambda i: (0, 0))),
        out_shape=(o_shape, jax.ShapeDtypeStruct((hs, D_FF), F32),
                   jax.ShapeDtypeStruct((DEC_BATCH, D_MODEL), F32),
                   jax.ShapeDtypeStruct((hs_s, D_FF), F32)),
        scratch_shapes=scratch,
        compiler_params=_params(),
        name="conv_ffn",
    )(x, xs, halo_s, wg, wu, cw, cb3, wd, lng3, lnb3)


def _odd_prompt_kernel(x_ref, win_ref, cw_ref, cb_ref, lcg_ref, lcb_ref, wout_ref, lng_ref, lnb_ref,
                       o_ref, cache_ref, gs_ref, hc_ref, *, tm):
    hs = (K_C - 1) * BATCH

    @pl.when(pl.program_id(0) == 0)
    def _():
        gs_ref[0:hs, :] = jnp.zeros((hs, D_C), F32)

    xb = x_ref[...]
    xbf = xb.astype(BF16)
    z1 = _dot(xbf, win_ref[:, 0:D_C])
    z2 = _dot(xbf, win_ref[:, D_C:2 * D_C])
    gs_ref[hs:hs + tm, :] = z1 * jax.nn.sigmoid(z2)

    def conv_rows(i, _):
        r0 = pl.multiple_of(i * CONV_ROWS, CONV_ROWS)
        for l0 in range(0, D_C, LANES):
            cols = slice(l0, l0 + LANES)
            win = gs_ref[pl.ds(r0, CONV_ROWS + hs), cols]
            acc = jnp.broadcast_to(cb_ref[:, cols], (CONV_ROWS, LANES))
            for k in range(K_C):
                acc = acc + cw_ref[k:k + 1, cols] * win[k * BATCH:k * BATCH + CONV_ROWS, :]
            hc_ref[pl.ds(r0, CONV_ROWS), cols] = acc
        return 0

    lax.fori_loop(0, tm // CONV_ROWS, conv_rows, 0)
    h = jax.nn.silu(_ln(hc_ref[...], lcg_ref[...], lcb_ref[...]))
    out = _dot(h.astype(BF16), wout_ref[...])
    o_ref[...] = _ln(ALPHA * xb + out, lng_ref[...], lnb_ref[...])
    tail = gs_ref[tm:tm + hs, :]
    cache_ref[...] = tail
    gs_ref[0:hs, :] = tail


def _odd_prompt(x, o, l, win, cw, cb3, lcg3, lcb3, wout, lng3, lnb3, *, tm):
    hs = (K_C - 1) * BATCH
    in_specs = [pl.BlockSpec((tm, D_MODEL), lambda i: (i, 0)),
                _layer(win, o), _layer(cw, o), _layer(cb3, o), _layer(lcg3, o), _layer(lcb3, o),
                _layer(wout, o), _layer(lng3, l), _layer(lnb3, l)]
    return pl.pallas_call(
        functools.partial(_odd_prompt_kernel, tm=tm),
        grid=(M_PROMPT // tm,),
        in_specs=in_specs,
        out_specs=(pl.BlockSpec((tm, D_MODEL), lambda i: (i, 0)), pl.BlockSpec((hs, D_C), lambda i: (0, 0))),
        out_shape=(jax.ShapeDtypeStruct((M_PROMPT, D_MODEL), F32), jax.ShapeDtypeStruct((hs, D_C), F32)),
        scratch_shapes=[pltpu.VMEM((hs + tm, D_C), F32), pltpu.VMEM((tm, D_C), F32)],
        compiler_params=_params(),
        name="odd_prompt",
    )(x, win, cw, cb3, lcg3, lcb3, wout, lng3, lnb3)


def _odd_sample_kernel(x_ref, c_ref, cn_ref, win_ref, cw_ref, cb_ref, lcg_ref, lcb_ref, wout_ref, lng_ref, lnb_ref,
                       o_ref, oc_ref, g_ref, pc_ref):
    k = pl.program_id(0)
    last = (K_C - 1) // SAMPLE_TAPS - 1

    @pl.when(k == 0)
    def _():
        xbf = x_ref[...].astype(BF16)
        z1 = _dot(xbf, win_ref[:, 0:D_C])
        z2 = _dot(xbf, win_ref[:, D_C:2 * D_C])
        g_ref[...] = z1 * jax.nn.sigmoid(z2)
        pc_ref[...] = jnp.broadcast_to(cb_ref[...], pc_ref.shape)

    pc = pc_ref[...]
    for t in range(SAMPLE_TAPS):
        pc = pc + cw_ref[pl.ds(k * SAMPLE_TAPS + t, 1), :] * c_ref[t]
    pc_ref[...] = pc
    oc_ref[0:SAMPLE_TAPS - 1] = c_ref[1:SAMPLE_TAPS]

    @pl.when(k < last)
    def _():
        oc_ref[SAMPLE_TAPS - 1] = cn_ref[...]

    @pl.when(k == last)
    def _():
        g = g_ref[...]
        oc_ref[SAMPLE_TAPS - 1] = g
        hc = pc + cw_ref[K_C - 1:K_C, :] * g
        h = jax.nn.silu(_ln(hc, lcg_ref[...], lcb_ref[...]))
        out = _dot(h.astype(BF16), wout_ref[...])
        o_ref[...] = _ln(ALPHA * x_ref[...] + out, lng_ref[...], lnb_ref[...])


def _odd_sample(x, cache_t, o, l, win, cw, cb3, lcg3, lcb3, wout, lng3, lnb3):
    nk = K_C - 1
    steps = nk // SAMPLE_TAPS
    in_specs = [_whole(x),
                pl.BlockSpec((None, SAMPLE_TAPS, DEC_BATCH, D_C), lambda k: (o, k, 0, 0)),
                pl.BlockSpec((None, None, DEC_BATCH, D_C),
                             lambda k: (o, jnp.minimum((k + 1) * SAMPLE_TAPS, nk - 1), 0, 0)),
                _layer(win, o), _layer(cw, o), _layer(cb3, o), _layer(lcg3, o), _layer(lcb3, o),
                _layer(wout, o), _layer(lng3, l), _layer(lnb3, l)]
    return pl.pallas_call(
        _odd_sample_kernel,
        grid=(steps,),
        in_specs=in_specs,
        out_specs=(pl.BlockSpec((DEC_BATCH, D_MODEL), lambda k: (0, 0)),
                   pl.BlockSpec((SAMPLE_TAPS, DEC_BATCH, D_C), lambda k: (k, 0, 0))),
        out_shape=(jax.ShapeDtypeStruct((DEC_BATCH, D_MODEL), F32),
                   jax.ShapeDtypeStruct((nk, DEC_BATCH, D_C), F32)),
        scratch_shapes=[pltpu.VMEM((DEC_BATCH, D_C), F32), pltpu.VMEM((DEC_BATCH, D_C), F32)],
        compiler_params=_params(),
        name="odd_sample",
    )(x, cache_t, cache_t, win, cw, cb3, lcg3, lcb3, wout, lng3, lnb3)


def kernel(x_prompt, x_sample, state_a_re, state_a_im, cache_c_conv, cache_ffn_conv, w_in_ab, s5_lam_re, s5_lam_im, s5_log_dt, s5_b_re, s5_b_im, s5_c_re, s5_c_im, s5_d, s5_glu_w, s5_glu_b, sgu_ln_g, sgu_ln_b, sgu_w, sgu_b, w_out_ab, w_in_c, conv_c_w, conv_c_b, ln_c_g, ln_c_b, w_out_c, ffn_w_gate, ffn_w_up, ffn_conv_w, ffn_conv_b, ffn_w_down, ln_mix_g, ln_mix_b, ln_ffn_g, ln_ffn_b):
    xs = x_sample.reshape(DEC_BATCH, D_MODEL)
    xp = x_prompt

    win_ab, wout_ab, gluw = w_in_ab.astype(BF16), w_out_ab.astype(BF16), s5_glu_w.astype(BF16)
    win_c, wout_c = w_in_c.astype(BF16), w_out_c.astype(BF16)
    cache_c_t = jnp.transpose(cache_c_conv, (0, 2, 1, 3))
    wg, wu, wd = ffn_w_gate.astype(BF16), ffn_w_up.astype(BF16), ffn_w_down.astype(BF16)
    fcb3 = _rows3(ffn_conv_b)
    d3, glub3, sg3, sb3 = _rows3(s5_d), _rows3(s5_glu_b), _rows3(sgu_ln_g), _rows3(sgu_ln_b)
    sbt = jnp.swapaxes(sgu_b, 1, 2)
    ccb3, lcg3, lcb3 = _rows3(conv_c_b), _rows3(ln_c_g), _rows3(ln_c_b)
    lmg3, lmb3, lfg3, lfb3 = _rows3(ln_mix_g), _rows3(ln_mix_b), _rows3(ln_ffn_g), _rows3(ln_ffn_b)

    sa_re_p, sa_im_p, sa_re_s, sa_im_s, sb_v_s = [], [], [], [], []
    cc_p, cc_s, cf_p, cf_s = [], [], [], []
    for l in range(DEPTH):
        if l % 2 == 0:
            e = l // 2
            lbr, lbi, bbr, bbi, cneg = _s5_prep(
                s5_lam_re[e], s5_lam_im[e], s5_log_dt[e],
                jnp.swapaxes(s5_b_re[e], 1, 2), jnp.swapaxes(s5_b_im[e], 1, 2), s5_c_im[e])
            bbd, cbd, lam = _block_diag_params(lbr, lbi, bbr, bbi, s5_c_re[e], cneg)
            if l > 0:
                xp = jnp.transpose(xp.reshape(SEQ, BATCH, D_MODEL), (1, 0, 2))
            shared = (win_ab, bbd, cbd, lam, d3, gluw, glub3, sg3, sb3)
            xp, sre, sim = _even_prompt(xp, e, l, *shared, sgu_w, sbt, wout_ab, lmg3, lmb3)
            sw0 = jnp.repeat(sgu_w[e][:, 0, 0], HD_B).reshape(1, D_B)
            sb0 = jnp.repeat(sgu_b[e][:, 0], HD_B).reshape(1, D_B)
            xs, sres, sims, vn = _even_sample(
                xs, state_a_re[e].reshape(DEC_BATCH, -1), state_a_im[e].reshape(DEC_BATCH, -1),
                e, l, *shared, sw0, sb0, wout_ab, lmg3, lmb3)
            sa_re_p.append(sre.reshape(BATCH, N_GA, P_A))
            sa_im_p.append(sim.reshape(BATCH, N_GA, P_A))
            sa_re_s.append(sres.reshape(DEC_BATCH, N_GA, P_A))
            sa_im_s.append(sims.reshape(DEC_BATCH, N_GA, P_A))
            sb_v_s.append(vn.reshape(DEC_BATCH, 1, D_B))
        else:
            o = l // 2
            rest = (win_c, conv_c_w, ccb3, lcg3, lcb3, wout_c, lmg3, lmb3)
            xp, cp = _odd_prompt(xp, o, l, *rest, tm=ODD_TM)
            xs, cs_t = _odd_sample(xs, cache_c_t, o, l, *rest)
            cc_p.append(jnp.transpose(cp.reshape(K_C - 1, BATCH, D_C), (1, 0, 2)))
            cc_s.append(jnp.transpose(cs_t, (1, 0, 2)))
        frest = (wg, wu, ffn_conv_w, fcb3, wd, lfg3, lfb3)
        halo_s = jnp.transpose(cache_ffn_conv[l], (1, 0, 2)).reshape((K_F - 1) * DEC_BATCH, D_FF)
        xp, fcp, xs, fcs = _ffn(xp, xs, halo_s, l, *frest, tm=FFN_TM, batch_major_out=(l == DEPTH - 1))
        cf_p.append(jnp.transpose(fcp.reshape(K_F - 1, BATCH, D_FF), (1, 0, 2)))
        cf_s.append(jnp.transpose(fcs.reshape(K_F - 1, DEC_BATCH, D_FF), (1, 0, 2)))

    ys = xs.reshape(DEC_BATCH, 1, D_MODEL)
    return (xp, ys,
            jnp.stack(sa_re_p), jnp.stack(sa_im_p), jnp.stack(sa_re_s), jnp.stack(sa_im_s),
            jnp.stack(sb_v_s),
            jnp.stack(cc_p), jnp.stack(cc_s),
            jnp.stack(cf_p), jnp.stack(cf_s))
```

```python
import functools

import jax
import jax.numpy as jnp
from jax import lax
from jax.experimental import pallas as pl
from jax.experimental.pallas import tpu as pltpu

D_MODEL = 1024
BATCH = 8
SEQ = 2048
DEPTH = 2
DEC_BATCH = 128
N_EVEN = (DEPTH + 1) // 2
N_ODD = DEPTH // 2
D_A = D_MODEL // 2
S5_GROUP = 16
N_GA = D_A // S5_GROUP
P_A = 64
D_B = D_MODEL // 2
N_HB = 4
HD_B = D_B // N_HB
CHUNK = 128
D_C = D_MODEL
K_C = 31
D_FF = ((8 * D_MODEL) // 3 + 127) // 128 * 128
K_F = 3
ALPHA = (2.0 * DEPTH) ** 0.25
LN_EPS = 1e-5

F32 = jnp.float32
BF16 = jnp.bfloat16

LANES = 128
MXU_N = 256
M_PROMPT = BATCH * SEQ
ROWS_PER_CHUNK = CHUNK * BATCH
N_CG = 4
G_PER_CG = N_GA // N_CG
CH_PER_CG = G_PER_CG * S5_GROUP
ST_PER_CG = G_PER_CG * P_A
FF_CHUNKS = ((0, 6 * MXU_N), (6 * MXU_N, D_FF))
FFN_TM = 512
ODD_TM = 1024
CONV_ROWS = 8 * BATCH
SAMPLE_TAPS = 5
W_ROWS = 256
VMEM_LIMIT = 56 * 1024 * 1024


def _ln(x, g, b):
    mu = jnp.mean(x, axis=-1, keepdims=True)
    xc = x - mu
    var = jnp.mean(xc * xc, axis=-1, keepdims=True)
    return xc * lax.rsqrt(var + LN_EPS) * g + b


def _dot(a, b):
    return jnp.dot(a, b, preferred_element_type=F32)


def _whole(arr):
    nd = arr.ndim
    return pl.BlockSpec(arr.shape, lambda *_: (0,) * nd, pipeline_mode=pl.Buffered(1))


def _layer(arr, l):
    nd = arr.ndim
    return pl.BlockSpec((None,) + arr.shape[1:], lambda *_: (l,) + (0,) * (nd - 1),
                        pipeline_mode=pl.Buffered(1))


def _rows3(v):
    return v.reshape(v.shape[0], 1, v.shape[1])


def _params(n_grid=1):
    return pltpu.CompilerParams(dimension_semantics=("arbitrary",) * n_grid,
                                vmem_limit_bytes=VMEM_LIMIT)


def _s5_prep_kernel(lr_ref, li_ref, ldt_ref, br_ref, bi_ref, ci_ref,
                    lbr_ref, lbi_ref, bbr_ref, bbi_ref, cneg_ref):
    lr = lr_ref[...]
    li = li_ref[...]
    dt = jnp.exp(ldt_ref[...])
    mag = jnp.exp(lr * dt)
    lbr = mag * jnp.cos(li * dt)
    lbi = mag * jnp.sin(li * dt)
    lbr_ref[...] = lbr
    lbi_ref[...] = lbi
    nr = lbr - 1.0
    ni = lbi
    den = lr * lr + li * li
    qr = (nr * lr + ni * li) / den
    qi = (ni * lr - nr * li) / den
    br = br_ref[...]
    bi = bi_ref[...]
    qr3 = qr[:, None, :]
    qi3 = qi[:, None, :]
    bbr_ref[...] = qr3 * br - qi3 * bi
    bbi_ref[...] = qr3 * bi + qi3 * br
    cneg_ref[...] = -ci_ref[...]


def _s5_prep(lam_re, lam_im, log_dt, b_re, b_im, c_im):
    gp = jax.ShapeDtypeStruct((N_GA, P_A), F32)
    gcp = jax.ShapeDtypeStruct((N_GA, S5_GROUP, P_A), F32)
    return pl.pallas_call(
        _s5_prep_kernel,
        out_shape=(gp, gp, gcp, gcp, gcp),
        name="s5_prep",
    )(lam_re, lam_im, log_dt.reshape(N_GA, 1), b_re, b_im, c_im)


def _block_diag_params(lbr, lbi, bbr, bbi, c_re, cneg):
    eye = jnp.eye(G_PER_CG, dtype=F32)

    def in_mat(b):
        b4 = b.reshape(N_CG, G_PER_CG, S5_GROUP, P_A)
        return jnp.einsum('jgcp,gh->jgchp', b4, eye).reshape(N_CG, CH_PER_CG, ST_PER_CG)

    def out_mat(c):
        c4 = c.reshape(N_CG, G_PER_CG, S5_GROUP, P_A)
        return jnp.einsum('jgcp,gh->jgphc', c4, eye).reshape(N_CG, ST_PER_CG, CH_PER_CG)

    bbd = jnp.concatenate([in_mat(bbr), in_mat(bbi)], axis=2).astype(BF16)
    cbd = jnp.concatenate([out_mat(c_re), out_mat(cneg)], axis=1).astype(BF16)
    lam = jnp.stack([lbr.reshape(-1), lbi.reshape(-1)])
    return bbd, cbd, lam


def _even_prompt_kernel(x_ref, win_ref, bbd_ref, cbd_ref, lam_ref, d_ref, gluw_ref, glub_ref,
                        sg_ref, sb_ref, sw_ref, sbt_ref, wout_ref, lng_ref, lnb_ref,
                        o_ref, sre_ref, sim_ref,
                        st_ref, xt_ref, scr_ref, cat_ref, vn_ref, gt_ref):
    @pl.when(pl.program_id(0) == 0)
    def _():
        st_ref[...] = jnp.zeros_like(st_ref)

    for c in range(D_MODEL // LANES):
        for b in range(BATCH):
            xt_ref.at[c][pl.ds(b, CHUNK, stride=BATCH), :] = x_ref[b, :, LANES * c:LANES * (c + 1)]
    xb = jnp.concatenate([xt_ref[c] for c in range(D_MODEL // LANES)], axis=-1)
    xbf = xb.astype(BF16)

    ua = _dot(xbf, win_ref[:, 0:D_A])
    uab = ua.astype(BF16)
    ys = []
    for j in range(N_CG):
        re_cols = slice(2 * ST_PER_CG * j, 2 * ST_PER_CG * j + ST_PER_CG)
        im_cols = slice(2 * ST_PER_CG * j + ST_PER_CG, 2 * ST_PER_CG * (j + 1))
        scr_ref[...] = _dot(uab[:, CH_PER_CG * j:CH_PER_CG * (j + 1)], bbd_ref[j])
        lr = jnp.broadcast_to(lam_ref[0:1, ST_PER_CG * j:ST_PER_CG * (j + 1)], (BATCH, ST_PER_CG))
        li = jnp.broadcast_to(lam_ref[1:2, ST_PER_CG * j:ST_PER_CG * (j + 1)], (BATCH, ST_PER_CG))

        def step(t, carry, lr=lr, li=li):
            sr, si = carry
            row = pl.multiple_of(t * BATCH, BATCH)
            nr = lr * sr - li * si + scr_ref[pl.ds(row, BATCH), 0:ST_PER_CG]
            ni = lr * si + li * sr + scr_ref[pl.ds(row, BATCH), ST_PER_CG:2 * ST_PER_CG]
            scr_ref[pl.ds(row, BATCH), 0:ST_PER_CG] = nr
            scr_ref[pl.ds(row, BATCH), ST_PER_CG:2 * ST_PER_CG] = ni
            return nr, ni

        sr, si = lax.fori_loop(0, CHUNK, step, (st_ref[:, re_cols], st_ref[:, im_cols]), unroll=4)
        st_ref[:, re_cols] = sr
        st_ref[:, im_cols] = si
        sre_ref[:, ST_PER_CG * j:ST_PER_CG * (j + 1)] = sr
        sim_ref[:, ST_PER_CG * j:ST_PER_CG * (j + 1)] = si
        ys.append(_dot(scr_ref[...].astype(BF16), cbd_ref[j]))
    y = jnp.concatenate(ys, axis=-1) + d_ref[...] * ua
    g = jax.nn.gelu(y, approximate=True)
    ya = g * jax.nn.sigmoid(_dot(g.astype(BF16), gluw_ref[...]) + glub_ref[...])
    cat_ref[:, 0:D_A] = ya.astype(BF16)

    ub = _dot(xbf, win_ref[:, D_A:D_A + D_B])
    vb = _dot(xbf, win_ref[:, D_A + D_B:D_A + 2 * D_B])
    vn = _ln(vb, sg_ref[...], sb_ref[...])
    for h in range(N_HB):
        vn_ref[h] = vn[:, HD_B * h:HD_B * (h + 1)]
    r_id = lax.broadcasted_iota(jnp.int32, (CHUNK, CHUNK), 0)
    c_id = lax.broadcasted_iota(jnp.int32, (CHUNK, CHUNK), 1)
    tril = (c_id <= r_id).astype(F32)
    wm = [(sw_ref[h] * tril).astype(BF16) for h in range(N_HB)]
    for b in range(0, BATCH, 2):
        for h in range(N_HB):
            v2 = jnp.concatenate([vn_ref.at[h][pl.ds(b, CHUNK, stride=BATCH), :],
                                  vn_ref.at[h][pl.ds(b + 1, CHUNK, stride=BATCH), :]], axis=-1)
            g2 = _dot(wm[h], v2.astype(BF16)) + sbt_ref[:, h:h + 1]
            gt_ref.at[h][pl.ds(b, CHUNK, stride=BATCH), :] = g2[:, 0:HD_B]
            gt_ref.at[h][pl.ds(b + 1, CHUNK, stride=BATCH), :] = g2[:, HD_B:]
    gate = jnp.concatenate([gt_ref[h] for h in range(N_HB)], axis=-1)
    cat_ref[:, D_A:] = (ub * gate).astype(BF16)

    out = _dot(cat_ref[...], wout_ref[...])
    o_ref[...] = _ln(ALPHA * xb + out, lng_ref[...], lnb_ref[...])


def _even_prompt(x, e, l, win, bbd, cbd, lam, d3, gluw, glub3, sg3, sb3, sw, sbt, wout, lng3, lnb3):
    rows = ROWS_PER_CHUNK
    n = SEQ // CHUNK
    in_specs = [pl.BlockSpec((BATCH, CHUNK, D_MODEL), lambda i: (0, i, 0)),
                _layer(win, e), _whole(bbd), _whole(cbd), _whole(lam), _layer(d3, e), _layer(gluw, e),
                _layer(glub3, e), _layer(sg3, e), _layer(sb3, e), _layer(sw, e), _layer(sbt, e),
                _layer(wout, e), _layer(lng3, l), _layer(lnb3, l)]
    st = jax.ShapeDtypeStruct((BATCH, N_GA * P_A), F32)
    st_spec = pl.BlockSpec((BATCH, N_GA * P_A), lambda i: (0, 0))
    return pl.pallas_call(
        _even_prompt_kernel,
        grid=(n,),
        in_specs=in_specs,
        out_specs=(pl.BlockSpec((rows, D_MODEL), lambda i: (i, 0)), st_spec, st_spec),
        out_shape=(jax.ShapeDtypeStruct((M_PROMPT, D_MODEL), F32), st, st),
        scratch_shapes=[pltpu.VMEM((BATCH, 2 * N_GA * P_A), F32),
                        pltpu.VMEM((D_MODEL // LANES, rows, LANES), F32),
                        pltpu.VMEM((rows, 2 * ST_PER_CG), F32),
                        pltpu.VMEM((rows, D_MODEL), BF16),
                        pltpu.VMEM((N_HB, rows, HD_B), F32),
                        pltpu.VMEM((N_HB, rows, HD_B), F32)],
        compiler_params=_params(),
        name="even_prompt",
    )(x, win, bbd, cbd, lam, d3, gluw, glub3, sg3, sb3, sw, sbt, wout, lng3, lnb3)


def _even_sample_kernel(x_ref, s0r_ref, s0i_ref, win_ref, bbd_ref, cbd_ref, lam_ref, d_ref,
                        gluw_ref, glub_ref, sg_ref, sb_ref, sw0_ref, sb0_ref, wout_ref, lng_ref, lnb_ref,
                        o_ref, sre_ref, sim_ref, vn_ref):
    xb = x_ref[...]
    xbf = xb.astype(BF16)
    ua = _dot(xbf, win_ref[:, 0:D_A])
    uab = ua.astype(BF16)
    ys = []
    for j in range(N_CG):
        cols = slice(ST_PER_CG * j, ST_PER_CG * (j + 1))
        bu = _dot(uab[:, CH_PER_CG * j:CH_PER_CG * (j + 1)], bbd_ref[j])
        lr = lam_ref[0:1, cols]
        li = lam_ref[1:2, cols]
        s0r = s0r_ref[:, cols]
        s0i = s0i_ref[:, cols]
        sr = lr * s0r - li * s0i + bu[:, 0:ST_PER_CG]
        si = lr * s0i + li * s0r + bu[:, ST_PER_CG:]
        sre_ref[:, cols] = sr
        sim_ref[:, cols] = si
        ys.append(_dot(jnp.concatenate([sr, si], axis=-1).astype(BF16), cbd_ref[j]))
    y = jnp.concatenate(ys, axis=-1) + d_ref[...] * ua
    g = jax.nn.gelu(y, approximate=True)
    ya = g * jax.nn.sigmoid(_dot(g.astype(BF16), gluw_ref[...]) + glub_ref[...])

    ub = _dot(xbf, win_ref[:, D_A:D_A + D_B])
    vb = _dot(xbf, win_ref[:, D_A + D_B:D_A + 2 * D_B])
    vn = _ln(vb, sg_ref[...], sb_ref[...])
    vn_ref[...] = vn
    yb = ub * (sw0_ref[...] * vn + sb0_ref[...])

    cat = jnp.concatenate([ya, yb], axis=-1).astype(BF16)
    out = _dot(cat, wout_ref[...])
    o_ref[...] = _ln(ALPHA * xb + out, lng_ref[...], lnb_ref[...])


def _even_sample(x, s0r, s0i, e, l, win, bbd, cbd, lam, d3, gluw, glub3, sg3, sb3, sw0, sb0, wout, lng3, lnb3):
    in_specs = [_whole(x), _whole(s0r), _whole(s0i),
                _layer(win, e), _whole(bbd), _whole(cbd), _whole(lam), _layer(d3, e), _layer(gluw, e),
                _layer(glub3, e), _layer(sg3, e), _layer(sb3, e), _whole(sw0), _whole(sb0),
                _layer(wout, e), _layer(lng3, l), _layer(lnb3, l)]
    shapes = ((DEC_BATCH, D_MODEL), (DEC_BATCH, N_GA * P_A), (DEC_BATCH, N_GA * P_A), (DEC_BATCH, D_B))
    return pl.pallas_call(
        _even_sample_kernel,
        grid=(1,),
        in_specs=in_specs,
        out_specs=tuple(pl.BlockSpec(s, lambda i: (0, 0)) for s in shapes),
        out_shape=tuple(jax.ShapeDtypeStruct(s, F32) for s in shapes),
        compiler_params=_params(),
        name="even_sample",
    )(x, s0r, s0i, win, bbd, cbd, lam, d3, gluw, glub3, sg3, sb3, sw0, sb0, wout, lng3, lnb3)


def _store_batch_major(y, o_ref, yt_ref, tm):
    for c in range(D_MODEL // LANES):
        yt_ref[c] = y[:, LANES * c:LANES * (c + 1)]
    for c in range(D_MODEL // LANES):
        for b in range(BATCH):
            o_ref[b, :, LANES * c:LANES * (c + 1)] = yt_ref.at[c][pl.ds(b, tm // BATCH, stride=BATCH), :]


def _ffn_tile(xb, gs_ref, rows, shift, wg_ref, wu_ref, cw_ref, cb_ref, wd_ref, lng_ref, lnb_ref):
    hs = (K_F - 1) * shift
    xbf = xb.astype(BF16)
    acc = None
    for c0, c1 in FF_CHUNKS:
        gs_ref[hs:hs + rows, c0:c1] = _dot(xbf, wg_ref[:, c0:c1])
        conv = cb_ref[:, c0:c1]
        for k in range(K_F):
            conv = conv + cw_ref[k:k + 1, c0:c1] * gs_ref[k * shift:k * shift + rows, c0:c1]
        up = _dot(xbf, wu_ref[:, c0:c1])
        h = (jax.nn.silu(conv) * up).astype(BF16)
        part = _dot(h, wd_ref[c0:c1, :])
        acc = part if acc is None else acc + part
    return _ln(ALPHA * xb + acc, lng_ref[...], lnb_ref[...])


def _stage_bf16(w_hbm, dst_ref, stage_ref, sem):
    n_chunks = w_hbm.shape[0] // W_ROWS

    def chunk_copy(c):
        return pltpu.make_async_copy(w_hbm.at[pl.ds(c * W_ROWS, W_ROWS), :], stage_ref.at[c % 2], sem.at[c % 2])

    chunk_copy(0).start()
    for c in range(n_chunks):
        if c + 1 < n_chunks:
            chunk_copy(c + 1).start()
        chunk_copy(c).wait()
        dst_ref[c * W_ROWS:(c + 1) * W_ROWS, :] = stage_ref[c % 2].astype(BF16)


def _ffn_kernel(x_ref, xs_ref, halo_s_ref, wg_hbm, wu_hbm, cw_ref, cb_ref, wd_hbm, lng_ref, lnb_ref,
                o_ref, cache_ref, os_ref, cache_s_ref,
                gs_ref, wg_ref, wu_ref, wd_ref, stage_in_ref, stage_out_ref, sem_in, sem_out, *rest,
                l, tm, n, batch_major_out):
    i = pl.program_id(0)
    hs = (K_F - 1) * BATCH
    hs_s = (K_F - 1) * DEC_BATCH
    weights = (wg_ref, wu_ref, cw_ref, cb_ref, wd_ref, lng_ref, lnb_ref)

    @pl.when(i == 0)
    def _():
        _stage_bf16(wg_hbm.at[l], wg_ref, stage_in_ref, sem_in)
        _stage_bf16(wu_hbm.at[l], wu_ref, stage_in_ref, sem_in)
        _stage_bf16(wd_hbm.at[l], wd_ref, stage_out_ref, sem_out)
        gs_ref[0:hs, :] = jnp.zeros((hs, D_FF), F32)

    @pl.when(i < n)
    def _():
        y = _ffn_tile(x_ref[...], gs_ref, tm, BATCH, *weights)
        if batch_major_out:
            _store_batch_major(y, o_ref, rest[0], tm)
        else:
            o_ref[...] = y
        tail = gs_ref[tm:tm + hs, :]
        cache_ref[...] = tail
        gs_ref[0:hs, :] = tail

    @pl.when(i == n)
    def _():
        gs_ref[0:hs_s, :] = halo_s_ref[...]
        os_ref[...] = _ffn_tile(xs_ref[...], gs_ref, DEC_BATCH, DEC_BATCH, *weights)
        cache_s_ref[...] = gs_ref[DEC_BATCH:DEC_BATCH + hs_s, :]


def _ffn(x, xs, halo_s, l, wg, wu, cw, cb3, wd, lng3, lnb3, *, tm, batch_major_out=False):
    m = x.shape[0]
    n = m // tm
    hs = (K_F - 1) * BATCH
    hs_s = (K_F - 1) * DEC_BATCH
    hbm = pl.BlockSpec(memory_space=pl.ANY)
    in_specs = [pl.BlockSpec((tm, D_MODEL), lambda i: (jnp.minimum(i, n - 1), 0)), _whole(xs), _whole(halo_s),
                hbm, hbm, _layer(cw, l), _layer(cb3, l), hbm,
                _layer(lng3, l), _layer(lnb3, l)]
    scratch = [pltpu.VMEM((max(hs + tm, hs_s + DEC_BATCH), D_FF), F32),
               pltpu.VMEM((D_MODEL, D_FF), BF16), pltpu.VMEM((D_MODEL, D_FF), BF16), pltpu.VMEM((D_FF, D_MODEL), BF16),
               pltpu.VMEM((2, W_ROWS, D_FF), F32), pltpu.VMEM((2, W_ROWS, D_MODEL), F32),
               pltpu.SemaphoreType.DMA((2,)), pltpu.SemaphoreType.DMA((2,))]
    if batch_major_out:
        o_spec = pl.BlockSpec((BATCH, tm // BATCH, D_MODEL), lambda i: (0, jnp.minimum(i, n - 1), 0))
        o_shape = jax.ShapeDtypeStruct((BATCH, m // BATCH, D_MODEL), F32)
        scratch.append(pltpu.VMEM((D_MODEL // LANES, tm, LANES), F32))
    else:
        o_spec = pl.BlockSpec((tm, D_MODEL), lambda i: (jnp.minimum(i, n - 1), 0))
        o_shape = jax.ShapeDtypeStruct((m, D_MODEL), F32)
    return pl.pallas_call(
        functools.partial(_ffn_kernel, l=l, tm=tm, n=n, batch_major_out=batch_major_out),
        grid=(n + 1,),
        in_specs=in_specs,
        out_specs=(o_spec, pl.BlockSpec((hs, D_FF), lambda i: (0, 0)),
                   pl.BlockSpec((DEC_BATCH, D_MODEL), lambda i: (0, 0)),
                   pl.BlockSpec((hs_s, D_FF), lambda i: (0, 0))),
        out_shape=(o_shape, jax.ShapeDtypeStruct((hs, D_FF), F32),
                   jax.ShapeDtypeStruct((DEC_BATCH, D_MODEL), F32),
                   jax.ShapeDtypeStruct((hs_s, D_FF), F32)),
        scratch_shapes=scratch,
        compiler_params=_params(),
        name="conv_ffn",
    )(x, xs, halo_s, wg, wu, cw, cb3, wd, lng3, lnb3)


def _odd_prompt_kernel(x_ref, win_ref, cw_ref, cb_ref, lcg_ref, lcb_ref, wout_ref, lng_ref, lnb_ref,
                       o_ref, cache_ref, gs_ref, hc_ref, *, tm):
    hs = (K_C - 1) * BATCH

    @pl.when(pl.program_id(0) == 0)
    def _():
        gs_ref[0:hs, :] = jnp.zeros((hs, D_C), F32)

    xb = x_ref[...]
    xbf = xb.astype(BF16)
    z1 = _dot(xbf, win_ref[:, 0:D_C])
    z2 = _dot(xbf, win_ref[:, D_C:2 * D_C])
    gs_ref[hs:hs + tm, :] = z1 * jax.nn.sigmoid(z2)

    def conv_rows(i, _):
        r0 = pl.multiple_of(i * CONV_ROWS, CONV_ROWS)
        for l0 in range(0, D_C, LANES):
            cols = slice(l0, l0 + LANES)
            win = gs_ref[pl.ds(r0, CONV_ROWS + hs), cols]
            acc = jnp.broadcast_to(cb_ref[:, cols], (CONV_ROWS, LANES))
            for k in range(K_C):
                acc = acc + cw_ref[k:k + 1, cols] * win[k * BATCH:k * BATCH + CONV_ROWS, :]
            hc_ref[pl.ds(r0, CONV_ROWS), cols] = acc
        return 0

    lax.fori_loop(0, tm // CONV_ROWS, conv_rows, 0)
    h = jax.nn.silu(_ln(hc_ref[...], lcg_ref[...], lcb_ref[...]))
    out = _dot(h.astype(BF16), wout_ref[...])
    o_ref[...] = _ln(ALPHA * xb + out, lng_ref[...], lnb_ref[...])
    tail = gs_ref[tm:tm + hs, :]
    cache_ref[...] = tail
    gs_ref[0:hs, :] = tail


def _odd_prompt(x, o, l, win, cw, cb3, lcg3, lcb3, wout, lng3, lnb3, *, tm):
    hs = (K_C - 1) * BATCH
    in_specs = [pl.BlockSpec((tm, D_MODEL), lambda i: (i, 0)),
                _layer(win, o), _layer(cw, o), _layer(cb3, o), _layer(lcg3, o), _layer(lcb3, o),
                _layer(wout, o), _layer(lng3, l), _layer(lnb3, l)]
    return pl.pallas_call(
        functools.partial(_odd_prompt_kernel, tm=tm),
        grid=(M_PROMPT // tm,),
        in_specs=in_specs,
        out_specs=(pl.BlockSpec((tm, D_MODEL), lambda i: (i, 0)), pl.BlockSpec((hs, D_C), lambda i: (0, 0))),
        out_shape=(jax.ShapeDtypeStruct((M_PROMPT, D_MODEL), F32), jax.ShapeDtypeStruct((hs, D_C), F32)),
        scratch_shapes=[pltpu.VMEM((hs + tm, D_C), F32), pltpu.VMEM((tm, D_C), F32)],
        compiler_params=_params(),
        name="odd_prompt",
    )(x, win, cw, cb3, lcg3, lcb3, wout, lng3, lnb3)


def _odd_sample_kernel(x_ref, c_ref, cn_ref, win_ref, cw_ref, cb_ref, lcg_ref, lcb_ref, wout_ref, lng_ref, lnb_ref,
                       o_ref, oc_ref, g_ref, pc_ref):
    k = pl.program_id(0)
    last = (K_C - 1) // SAMPLE_TAPS - 1

    @pl.when(k == 0)
    def _():
        xbf = x_ref[...].astype(BF16)
        z1 = _dot(xbf, win_ref[:, 0:D_C])
        z2 = _dot(xbf, win_ref[:, D_C:2 * D_C])
        g_ref[...] = z1 * jax.nn.sigmoid(z2)
        pc_ref[...] = jnp.broadcast_to(cb_ref[...], pc_ref.shape)

    pc = pc_ref[...]
    for t in range(SAMPLE_TAPS):
        pc = pc + cw_ref[pl.ds(k * SAMPLE_TAPS + t, 1), :] * c_ref[t]
    pc_ref[...] = pc
    oc_ref[0:SAMPLE_TAPS - 1] = c_ref[1:SAMPLE_TAPS]

    @pl.when(k < last)
    def _():
        oc_ref[SAMPLE_TAPS - 1] = cn_ref[...]

    @pl.when(k == last)
    def _():
        g = g_ref[...]
        oc_ref[SAMPLE_TAPS - 1] = g
        hc = pc + cw_ref[K_C - 1:K_C, :] * g
        h = jax.nn.silu(_ln(hc, lcg_ref[...], lcb_ref[...]))
        out = _dot(h.astype(BF16), wout_ref[...])
        o_ref[...] = _ln(ALPHA * x_ref[...] + out, lng_ref[...], lnb_ref[...])


def _odd_sample(x, cache_t, o, l, win, cw, cb3, lcg3, lcb3, wout, lng3, lnb3):
    nk = K_C - 1
    steps = nk // SAMPLE_TAPS
    in_specs = [_whole(x),
                pl.BlockSpec((None, SAMPLE_TAPS, DEC_BATCH, D_C), lambda k: (o, k, 0, 0)),
                pl.BlockSpec((None, None, DEC_BATCH, D_C),
                             lambda k: (o, jnp.minimum((k + 1) * SAMPLE_TAPS, nk - 1), 0, 0)),
                _layer(win, o), _layer(cw, o), _layer(cb3, o), _layer(lcg3, o), _layer(lcb3, o),
                _layer(wout, o), _layer(lng3, l), _layer(lnb3, l)]
    return pl.pallas_call(
        _odd_sample_kernel,
        grid=(steps,),
        in_specs=in_specs,
        out_specs=(pl.BlockSpec((DEC_BATCH, D_MODEL), lambda k: (0, 0)),
                   pl.BlockSpec((SAMPLE_TAPS, DEC_BATCH, D_C), lambda k: (k, 0, 0))),
        out_shape=(jax.ShapeDtypeStruct((DEC_BATCH, D_MODEL), F32),
                   jax.ShapeDtypeStruct((nk, DEC_BATCH, D_C), F32)),
        scratch_shapes=[pltpu.VMEM((DEC_BATCH, D_C), F32), pltpu.VMEM((DEC_BATCH, D_C), F32)],
        compiler_params=_params(),
        name="odd_sample",
    )(x, cache_t, cache_t, win, cw, cb3, lcg3, lcb3, wout, lng3, lnb3)


def kernel(x_prompt, x_sample, state_a_re, state_a_im, cache_c_conv, cache_ffn_conv, w_in_ab, s5_lam_re, s5_lam_im, s5_log_dt, s5_b_re, s5_b_im, s5_c_re, s5_c_im, s5_d, s5_glu_w, s5_glu_b, sgu_ln_g, sgu_ln_b, sgu_w, sgu_b, w_out_ab, w_in_c, conv_c_w, conv_c_b, ln_c_g, ln_c_b, w_out_c, ffn_w_gate, ffn_w_up, ffn_conv_w, ffn_conv_b, ffn_w_down, ln_mix_g, ln_mix_b, ln_ffn_g, ln_ffn_b):
    xs = x_sample.reshape(DEC_BATCH, D_MODEL)
    xp = x_prompt

    win_ab, wout_ab, gluw = w_in_ab.astype(BF16), w_out_ab.astype(BF16), s5_glu_w.astype(BF16)
    win_c, wout_c = w_in_c.astype(BF16), w_out_c.astype(BF16)
    cache_c_t = jnp.transpose(cache_c_conv, (0, 2, 1, 3))
    fcb3 = _rows3(ffn_conv_b)
    d3, glub3, sg3, sb3 = _rows3(s5_d), _rows3(s5_glu_b), _rows3(sgu_ln_g), _rows3(sgu_ln_b)
    sbt = jnp.swapaxes(sgu_b, 1, 2)
    ccb3, lcg3, lcb3 = _rows3(conv_c_b), _rows3(ln_c_g), _rows3(ln_c_b)
    lmg3, lmb3, lfg3, lfb3 = _rows3(ln_mix_g), _rows3(ln_mix_b), _rows3(ln_ffn_g), _rows3(ln_ffn_b)

    sa_re_p, sa_im_p, sa_re_s, sa_im_s, sb_v_s = [], [], [], [], []
    cc_p, cc_s, cf_p, cf_s = [], [], [], []
    for l in range(DEPTH):
        if l % 2 == 0:
            e = l // 2
            lbr, lbi, bbr, bbi, cneg = _s5_prep(
                s5_lam_re[e], s5_lam_im[e], s5_log_dt[e],
                jnp.swapaxes(s5_b_re[e], 1, 2), jnp.swapaxes(s5_b_im[e], 1, 2), s5_c_im[e])
            bbd, cbd, lam = _block_diag_params(lbr, lbi, bbr, bbi, s5_c_re[e], cneg)
            if l > 0:
                xp = jnp.transpose(xp.reshape(SEQ, BATCH, D_MODEL), (1, 0, 2))
            shared = (win_ab, bbd, cbd, lam, d3, gluw, glub3, sg3, sb3)
            xp, sre, sim = _even_prompt(xp, e, l, *shared, sgu_w, sbt, wout_ab, lmg3, lmb3)
            sw0 = jnp.repeat(sgu_w[e][:, 0, 0], HD_B).reshape(1, D_B)
            sb0 = jnp.repeat(sgu_b[e][:, 0], HD_B).reshape(1, D_B)
            xs, sres, sims, vn = _even_sample(
                xs, state_a_re[e].reshape(DEC_BATCH, -1), state_a_im[e].reshape(DEC_BATCH, -1),
                e, l, *shared, sw0, sb0, wout_ab, lmg3, lmb3)
            sa_re_p.append(sre.reshape(BATCH, N_GA, P_A))
            sa_im_p.append(sim.reshape(BATCH, N_GA, P_A))
            sa_re_s.append(sres.reshape(DEC_BATCH, N_GA, P_A))
            sa_im_s.append(sims.reshape(DEC_BATCH, N_GA, P_A))
            sb_v_s.append(vn.reshape(DEC_BATCH, 1, D_B))
        else:
            o = l // 2
            rest = (win_c, conv_c_w, ccb3, lcg3, lcb3, wout_c, lmg3, lmb3)
            xp, cp = _odd_prompt(xp, o, l, *rest, tm=ODD_TM)
            xs, cs_t = _odd_sample(xs, cache_c_t, o, l, *rest)
            cc_p.append(jnp.transpose(cp.reshape(K_C - 1, BATCH, D_C), (1, 0, 2)))
            cc_s.append(jnp.transpose(cs_t, (1, 0, 2)))
        frest = (ffn_w_gate, ffn_w_up, ffn_conv_w, fcb3, ffn_w_down, lfg3, lfb3)
        halo_s = jnp.transpose(cache_ffn_conv[l], (1, 0, 2)).reshape((K_F - 1) * DEC_BATCH, D_FF)
        xp, fcp, xs, fcs = _ffn(xp, xs, halo_s, l, *frest, tm=FFN_TM, batch_major_out=(l == DEPTH - 1))
        cf_p.append(jnp.transpose(fcp.reshape(K_F - 1, BATCH, D_FF), (1, 0, 2)))
        cf_s.append(jnp.transpose(fcs.reshape(K_F - 1, DEC_BATCH, D_FF), (1, 0, 2)))

    ys = xs.reshape(DEC_BATCH, 1, D_MODEL)
    return (xp, ys,
            jnp.stack(sa_re_p), jnp.stack(sa_im_p), jnp.stack(sa_re_s), jnp.stack(sa_im_s),
            jnp.stack(sb_v_s),
            jnp.stack(cc_p), jnp.stack(cc_s),
            jnp.stack(cf_p), jnp.stack(cf_s))
```

```python
import functools

import jax
import jax.numpy as jnp
from jax import lax
from jax.experimental import pallas as pl
from jax.experimental.pallas import tpu as pltpu

D_MODEL = 1024
BATCH = 8
SEQ = 2048
DEPTH = 2
DEC_BATCH = 128
N_EVEN = (DEPTH + 1) // 2
N_ODD = DEPTH // 2
D_A = D_MODEL // 2
S5_GROUP = 16
N_GA = D_A // S5_GROUP
P_A = 64
D_B = D_MODEL // 2
N_HB = 4
HD_B = D_B // N_HB
CHUNK = 128
D_C = D_MODEL
K_C = 31
D_FF = ((8 * D_MODEL) // 3 + 127) // 128 * 128
K_F = 3
ALPHA = (2.0 * DEPTH) ** 0.25
LN_EPS = 1e-5

F32 = jnp.float32
BF16 = jnp.bfloat16

LANES = 128
MXU_N = 256
M_PROMPT = BATCH * SEQ
ROWS_PER_CHUNK = CHUNK * BATCH
N_CG = 4
G_PER_CG = N_GA // N_CG
CH_PER_CG = G_PER_CG * S5_GROUP
ST_PER_CG = G_PER_CG * P_A
FF_CHUNKS = ((0, 6 * MXU_N), (6 * MXU_N, D_FF))
FFN_TM = 512
ODD_TM = 1024
CONV_ROWS = 8 * BATCH
SAMPLE_TAPS = 5
N_CAST_CHUNKS = 16
VMEM_LIMIT = 56 * 1024 * 1024


def _ln(x, g, b):
    mu = jnp.mean(x, axis=-1, keepdims=True)
    xc = x - mu
    var = jnp.mean(xc * xc, axis=-1, keepdims=True)
    return xc * lax.rsqrt(var + LN_EPS) * g + b


def _dot(a, b):
    return jnp.dot(a, b, preferred_element_type=F32)


def _whole(arr):
    nd = arr.ndim
    return pl.BlockSpec(arr.shape, lambda *_: (0,) * nd, pipeline_mode=pl.Buffered(1))


def _layer(arr, l):
    nd = arr.ndim
    return pl.BlockSpec((None,) + arr.shape[1:], lambda *_: (l,) + (0,) * (nd - 1),
                        pipeline_mode=pl.Buffered(1))


def _rows3(v):
    return v.reshape(v.shape[0], 1, v.shape[1])


def _params(n_grid=1):
    return pltpu.CompilerParams(dimension_semantics=("arbitrary",) * n_grid,
                                vmem_limit_bytes=VMEM_LIMIT)


def _cast_specs(jobs, n_steps):
    def chunk(i):
        return jnp.minimum(i * N_CAST_CHUNKS // n_steps, N_CAST_CHUNKS - 1)

    in_specs, out_specs, out_shapes = [], [], []
    for w, l in jobs:
        _, r, c = w.shape
        rows = r // N_CAST_CHUNKS
        in_specs.append(pl.BlockSpec((None, rows, c), lambda i, l=l: (l, chunk(i), 0)))
        out_specs.append(pl.BlockSpec((rows, c), lambda i: (chunk(i), 0)))
        out_shapes.append(jax.ShapeDtypeStruct((r, c), BF16))
    return in_specs, out_specs, out_shapes


def _cast_chunks(in_refs, out_refs):
    for w_ref, o_ref in zip(in_refs, out_refs):
        o_ref[...] = w_ref[...].astype(BF16)


def _s5_prep_kernel(lr_ref, li_ref, ldt_ref, br_ref, bi_ref, ci_ref,
                    lbr_ref, lbi_ref, bbr_ref, bbi_ref, cneg_ref):
    lr = lr_ref[...]
    li = li_ref[...]
    dt = jnp.exp(ldt_ref[...])
    mag = jnp.exp(lr * dt)
    lbr = mag * jnp.cos(li * dt)
    lbi = mag * jnp.sin(li * dt)
    lbr_ref[...] = lbr
    lbi_ref[...] = lbi
    nr = lbr - 1.0
    ni = lbi
    den = lr * lr + li * li
    qr = (nr * lr + ni * li) / den
    qi = (ni * lr - nr * li) / den
    br = br_ref[...]
    bi = bi_ref[...]
    qr3 = qr[:, None, :]
    qi3 = qi[:, None, :]
    bbr_ref[...] = qr3 * br - qi3 * bi
    bbi_ref[...] = qr3 * bi + qi3 * br
    cneg_ref[...] = -ci_ref[...]


def _s5_prep(lam_re, lam_im, log_dt, b_re, b_im, c_im):
    gp = jax.ShapeDtypeStruct((N_GA, P_A), F32)
    gcp = jax.ShapeDtypeStruct((N_GA, S5_GROUP, P_A), F32)
    return pl.pallas_call(
        _s5_prep_kernel,
        out_shape=(gp, gp, gcp, gcp, gcp),
        name="s5_prep",
    )(lam_re, lam_im, log_dt.reshape(N_GA, 1), b_re, b_im, c_im)


def _block_diag_params(lbr, lbi, bbr, bbi, c_re, cneg):
    eye = jnp.eye(G_PER_CG, dtype=F32)

    def in_mat(b):
        b4 = b.reshape(N_CG, G_PER_CG, S5_GROUP, P_A)
        return jnp.einsum('jgcp,gh->jgchp', b4, eye).reshape(N_CG, CH_PER_CG, ST_PER_CG)

    def out_mat(c):
        c4 = c.reshape(N_CG, G_PER_CG, S5_GROUP, P_A)
        return jnp.einsum('jgcp,gh->jgphc', c4, eye).reshape(N_CG, ST_PER_CG, CH_PER_CG)

    bbd = jnp.concatenate([in_mat(bbr), in_mat(bbi)], axis=2).astype(BF16)
    cbd = jnp.concatenate([out_mat(c_re), out_mat(cneg)], axis=1).astype(BF16)
    lam = jnp.stack([lbr.reshape(-1), lbi.reshape(-1)])
    return bbd, cbd, lam


def _even_prompt_kernel(x_ref, win_ref, bbd_ref, cbd_ref, lam_ref, d_ref, gluw_ref, glub_ref,
                        sg_ref, sb_ref, sw_ref, sbt_ref, wout_ref, lng_ref, lnb_ref, *rest, n_cast):
    cast_in, rest = rest[:n_cast], rest[n_cast:]
    o_ref, sre_ref, sim_ref = rest[:3]
    cast_out, rest = rest[3:3 + n_cast], rest[3 + n_cast:]
    st_ref, xt_ref, scr_ref, cat_ref, vn_ref, gt_ref = rest

    @pl.when(pl.program_id(0) == 0)
    def _():
        st_ref[...] = jnp.zeros_like(st_ref)

    _cast_chunks(cast_in, cast_out)

    for c in range(D_MODEL // LANES):
        for b in range(BATCH):
            xt_ref.at[c][pl.ds(b, CHUNK, stride=BATCH), :] = x_ref[b, :, LANES * c:LANES * (c + 1)]
    xb = jnp.concatenate([xt_ref[c] for c in range(D_MODEL // LANES)], axis=-1)
    xbf = xb.astype(BF16)

    ua = _dot(xbf, win_ref[:, 0:D_A])
    uab = ua.astype(BF16)
    ys = []
    for j in range(N_CG):
        re_cols = slice(2 * ST_PER_CG * j, 2 * ST_PER_CG * j + ST_PER_CG)
        im_cols = slice(2 * ST_PER_CG * j + ST_PER_CG, 2 * ST_PER_CG * (j + 1))
        scr_ref[...] = _dot(uab[:, CH_PER_CG * j:CH_PER_CG * (j + 1)], bbd_ref[j])
        lr = jnp.broadcast_to(lam_ref[0:1, ST_PER_CG * j:ST_PER_CG * (j + 1)], (BATCH, ST_PER_CG))
        li = jnp.broadcast_to(lam_ref[1:2, ST_PER_CG * j:ST_PER_CG * (j + 1)], (BATCH, ST_PER_CG))

        def step(t, carry, lr=lr, li=li):
            sr, si = carry
            row = pl.multiple_of(t * BATCH, BATCH)
            nr = lr * sr - li * si + scr_ref[pl.ds(row, BATCH), 0:ST_PER_CG]
            ni = lr * si + li * sr + scr_ref[pl.ds(row, BATCH), ST_PER_CG:2 * ST_PER_CG]
            scr_ref[pl.ds(row, BATCH), 0:ST_PER_CG] = nr
            scr_ref[pl.ds(row, BATCH), ST_PER_CG:2 * ST_PER_CG] = ni
            return nr, ni

        sr, si = lax.fori_loop(0, CHUNK, step, (st_ref[:, re_cols], st_ref[:, im_cols]), unroll=4)
        st_ref[:, re_cols] = sr
        st_ref[:, im_cols] = si
        sre_ref[:, ST_PER_CG * j:ST_PER_CG * (j + 1)] = sr
        sim_ref[:, ST_PER_CG * j:ST_PER_CG * (j + 1)] = si
        ys.append(_dot(scr_ref[...].astype(BF16), cbd_ref[j]))
    y = jnp.concatenate(ys, axis=-1) + d_ref[...] * ua
    g = jax.nn.gelu(y, approximate=True)
    ya = g * jax.nn.sigmoid(_dot(g.astype(BF16), gluw_ref[...]) + glub_ref[...])
    cat_ref[:, 0:D_A] = ya.astype(BF16)

    ub = _dot(xbf, win_ref[:, D_A:D_A + D_B])
    vb = _dot(xbf, win_ref[:, D_A + D_B:D_A + 2 * D_B])
    vn = _ln(vb, sg_ref[...], sb_ref[...])
    for h in range(N_HB):
        vn_ref[h] = vn[:, HD_B * h:HD_B * (h + 1)]
    r_id = lax.broadcasted_iota(jnp.int32, (CHUNK, CHUNK), 0)
    c_id = lax.broadcasted_iota(jnp.int32, (CHUNK, CHUNK), 1)
    tril = (c_id <= r_id).astype(F32)
    wm = [(sw_ref[h] * tril).astype(BF16) for h in range(N_HB)]
    for b in range(0, BATCH, 2):
        for h in range(N_HB):
            v2 = jnp.concatenate([vn_ref.at[h][pl.ds(b, CHUNK, stride=BATCH), :],
                                  vn_ref.at[h][pl.ds(b + 1, CHUNK, stride=BATCH), :]], axis=-1)
            g2 = _dot(wm[h], v2.astype(BF16)) + sbt_ref[:, h:h + 1]
            gt_ref.at[h][pl.ds(b, CHUNK, stride=BATCH), :] = g2[:, 0:HD_B]
            gt_ref.at[h][pl.ds(b + 1, CHUNK, stride=BATCH), :] = g2[:, HD_B:]
    gate = jnp.concatenate([gt_ref[h] for h in range(N_HB)], axis=-1)
    cat_ref[:, D_A:] = (ub * gate).astype(BF16)

    out = _dot(cat_ref[...], wout_ref[...])
    o_ref[...] = _ln(ALPHA * xb + out, lng_ref[...], lnb_ref[...])


def _even_prompt(x, e, l, win, bbd, cbd, lam, d3, gluw, glub3, sg3, sb3, sw, sbt, wout, lng3, lnb3, *, cast_jobs=()):
    rows = ROWS_PER_CHUNK
    n = SEQ // CHUNK
    c_in, c_out, c_shapes = _cast_specs(cast_jobs, n)
    in_specs = [pl.BlockSpec((BATCH, CHUNK, D_MODEL), lambda i: (0, i, 0)),
                _layer(win, e), _whole(bbd), _whole(cbd), _whole(lam), _layer(d3, e), _layer(gluw, e),
                _layer(glub3, e), _layer(sg3, e), _layer(sb3, e), _layer(sw, e), _layer(sbt, e),
                _layer(wout, e), _layer(lng3, l), _layer(lnb3, l)] + c_in
    st = jax.ShapeDtypeStruct((BATCH, N_GA * P_A), F32)
    st_spec = pl.BlockSpec((BATCH, N_GA * P_A), lambda i: (0, 0))
    outs = pl.pallas_call(
        functools.partial(_even_prompt_kernel, n_cast=len(cast_jobs)),
        grid=(n,),
        in_specs=in_specs,
        out_specs=(pl.BlockSpec((rows, D_MODEL), lambda i: (i, 0)), st_spec, st_spec, *c_out),
        out_shape=(jax.ShapeDtypeStruct((M_PROMPT, D_MODEL), F32), st, st, *c_shapes),
        scratch_shapes=[pltpu.VMEM((BATCH, 2 * N_GA * P_A), F32),
                        pltpu.VMEM((D_MODEL // LANES, rows, LANES), F32),
                        pltpu.VMEM((rows, 2 * ST_PER_CG), F32),
                        pltpu.VMEM((rows, D_MODEL), BF16),
                        pltpu.VMEM((N_HB, rows, HD_B), F32),
                        pltpu.VMEM((N_HB, rows, HD_B), F32)],
        compiler_params=_params(),
        name="even_prompt",
    )(x, win, bbd, cbd, lam, d3, gluw, glub3, sg3, sb3, sw, sbt, wout, lng3, lnb3, *[w for w, _ in cast_jobs])
    return outs[:3], outs[3:]


def _even_sample_kernel(x_ref, s0r_ref, s0i_ref, win_ref, bbd_ref, cbd_ref, lam_ref, d_ref,
                        gluw_ref, glub_ref, sg_ref, sb_ref, sw0_ref, sb0_ref, wout_ref, lng_ref, lnb_ref,
                        o_ref, sre_ref, sim_ref, vn_ref):
    xb = x_ref[...]
    xbf = xb.astype(BF16)
    ua = _dot(xbf, win_ref[:, 0:D_A])
    uab = ua.astype(BF16)
    ys = []
    for j in range(N_CG):
        cols = slice(ST_PER_CG * j, ST_PER_CG * (j + 1))
        bu = _dot(uab[:, CH_PER_CG * j:CH_PER_CG * (j + 1)], bbd_ref[j])
        lr = lam_ref[0:1, cols]
        li = lam_ref[1:2, cols]
        s0r = s0r_ref[:, cols]
        s0i = s0i_ref[:, cols]
        sr = lr * s0r - li * s0i + bu[:, 0:ST_PER_CG]
        si = lr * s0i + li * s0r + bu[:, ST_PER_CG:]
        sre_ref[:, cols] = sr
        sim_ref[:, cols] = si
        ys.append(_dot(jnp.concatenate([sr, si], axis=-1).astype(BF16), cbd_ref[j]))
    y = jnp.concatenate(ys, axis=-1) + d_ref[...] * ua
    g = jax.nn.gelu(y, approximate=True)
    ya = g * jax.nn.sigmoid(_dot(g.astype(BF16), gluw_ref[...]) + glub_ref[...])

    ub = _dot(xbf, win_ref[:, D_A:D_A + D_B])
    vb = _dot(xbf, win_ref[:, D_A + D_B:D_A + 2 * D_B])
    vn = _ln(vb, sg_ref[...], sb_ref[...])
    vn_ref[...] = vn
    yb = ub * (sw0_ref[...] * vn + sb0_ref[...])

    cat = jnp.concatenate([ya, yb], axis=-1).astype(BF16)
    out = _dot(cat, wout_ref[...])
    o_ref[...] = _ln(ALPHA * xb + out, lng_ref[...], lnb_ref[...])


def _even_sample(x, s0r, s0i, e, l, win, bbd, cbd, lam, d3, gluw, glub3, sg3, sb3, sw0, sb0, wout, lng3, lnb3):
    in_specs = [_whole(x), _whole(s0r), _whole(s0i),
                _layer(win, e), _whole(bbd), _whole(cbd), _whole(lam), _layer(d3, e), _layer(gluw, e),
                _layer(glub3, e), _layer(sg3, e), _layer(sb3, e), _whole(sw0), _whole(sb0),
                _layer(wout, e), _layer(lng3, l), _layer(lnb3, l)]
    shapes = ((DEC_BATCH, D_MODEL), (DEC_BATCH, N_GA * P_A), (DEC_BATCH, N_GA * P_A), (DEC_BATCH, D_B))
    return pl.pallas_call(
        _even_sample_kernel,
        grid=(1,),
        in_specs=in_specs,
        out_specs=tuple(pl.BlockSpec(s, lambda i: (0, 0)) for s in shapes),
        out_shape=tuple(jax.ShapeDtypeStruct(s, F32) for s in shapes),
        compiler_params=_params(),
        name="even_sample",
    )(x, s0r, s0i, win, bbd, cbd, lam, d3, gluw, glub3, sg3, sb3, sw0, sb0, wout, lng3, lnb3)


def _store_batch_major(y, o_ref, yt_ref, tm):
    for c in range(D_MODEL // LANES):
        yt_ref[c] = y[:, LANES * c:LANES * (c + 1)]
    for c in range(D_MODEL // LANES):
        for b in range(BATCH):
            o_ref[b, :, LANES * c:LANES * (c + 1)] = yt_ref.at[c][pl.ds(b, tm // BATCH, stride=BATCH), :]


def _ffn_tile(xb, gs_ref, rows, shift, wg_ref, wu_ref, cw_ref, cb_ref, wd_ref, lng_ref, lnb_ref):
    hs = (K_F - 1) * shift
    xbf = xb.astype(BF16)
    acc = None
    for c0, c1 in FF_CHUNKS:
        gs_ref[hs:hs + rows, c0:c1] = _dot(xbf, wg_ref[:, c0:c1])
        conv = cb_ref[:, c0:c1]
        for k in range(K_F):
            conv = conv + cw_ref[k:k + 1, c0:c1] * gs_ref[k * shift:k * shift + rows, c0:c1]
        up = _dot(xbf, wu_ref[:, c0:c1])
        h = (jax.nn.silu(conv) * up).astype(BF16)
        part = _dot(h, wd_ref[c0:c1, :])
        acc = part if acc is None else acc + part
    return _ln(ALPHA * xb + acc, lng_ref[...], lnb_ref[...])


def _ffn_kernel(x_ref, xs_ref, halo_s_ref, wg_ref, wu_ref, cw_ref, cb_ref, wd_ref, lng_ref, lnb_ref, *rest,
                tm, n, n_cast, batch_major_out):
    cast_in, rest = rest[:n_cast], rest[n_cast:]
    o_ref, cache_ref, os_ref, cache_s_ref = rest[:4]
    cast_out, rest = rest[4:4 + n_cast], rest[4 + n_cast:]
    gs_ref = rest[0]
    i = pl.program_id(0)
    hs = (K_F - 1) * BATCH
    hs_s = (K_F - 1) * DEC_BATCH
    weights = (wg_ref, wu_ref, cw_ref, cb_ref, wd_ref, lng_ref, lnb_ref)

    @pl.when(i == 0)
    def _():
        gs_ref[0:hs, :] = jnp.zeros((hs, D_FF), F32)

    @pl.when(i < n)
    def _():
        _cast_chunks(cast_in, cast_out)
        y = _ffn_tile(x_ref[...], gs_ref, tm, BATCH, *weights)
        if batch_major_out:
            _store_batch_major(y, o_ref, rest[1], tm)
        else:
            o_ref[...] = y
        tail = gs_ref[tm:tm + hs, :]
        cache_ref[...] = tail
        gs_ref[0:hs, :] = tail

    @pl.when(i == n)
    def _():
        gs_ref[0:hs_s, :] = halo_s_ref[...]
        os_ref[...] = _ffn_tile(xs_ref[...], gs_ref, DEC_BATCH, DEC_BATCH, *weights)
        cache_s_ref[...] = gs_ref[DEC_BATCH:DEC_BATCH + hs_s, :]


def _ffn(x, xs, halo_s, l, wg, wu, cw, cb3, wd, lng3, lnb3, *, tm, batch_major_out=False, cast_jobs=()):
    m = x.shape[0]
    n = m // tm
    hs = (K_F - 1) * BATCH
    hs_s = (K_F - 1) * DEC_BATCH
    c_in, c_out, c_shapes = _cast_specs(cast_jobs, n)
    in_specs = [pl.BlockSpec((tm, D_MODEL), lambda i: (jnp.minimum(i, n - 1), 0)), _whole(xs), _whole(halo_s),
                _whole(wg), _whole(wu), _layer(cw, l), _layer(cb3, l), _whole(wd),
                _layer(lng3, l), _layer(lnb3, l)] + c_in
    scratch = [pltpu.VMEM((max(hs + tm, hs_s + DEC_BATCH), D_FF), F32)]
    if batch_major_out:
        o_spec = pl.BlockSpec((BATCH, tm // BATCH, D_MODEL), lambda i: (0, jnp.minimum(i, n - 1), 0))
        o_shape = jax.ShapeDtypeStruct((BATCH, m // BATCH, D_MODEL), F32)
        scratch.append(pltpu.VMEM((D_MODEL // LANES, tm, LANES), F32))
    else:
        o_spec = pl.BlockSpec((tm, D_MODEL), lambda i: (jnp.minimum(i, n - 1), 0))
        o_shape = jax.ShapeDtypeStruct((m, D_MODEL), F32)
    outs = pl.pallas_call(
        functools.partial(_ffn_kernel, tm=tm, n=n, n_cast=len(cast_jobs), batch_major_out=batch_major_out),
        grid=(n + 1,),
        in_specs=in_specs,
        out_specs=(o_spec, pl.BlockSpec((hs, D_FF), lambda i: (0, 0)),
                   pl.BlockSpec((DEC_BATCH, D_MODEL), lambda i: (0, 0)),
                   pl.BlockSpec((hs_s, D_FF), lambda i: (0, 0)), *c_out),
        out_shape=(o_shape, jax.ShapeDtypeStruct((hs, D_FF), F32),
                   jax.ShapeDtypeStruct((DEC_BATCH, D_MODEL), F32),
                   jax.ShapeDtypeStruct((hs_s, D_FF), F32), *c_shapes),
        scratch_shapes=scratch,
        compiler_params=_params(),
        name="conv_ffn",
    )(x, xs, halo_s, wg, wu, cw, cb3, wd, lng3, lnb3, *[w for w, _ in cast_jobs])
    return outs[:4], outs[4:]


def _odd_prompt_kernel(x_ref, win_ref, cw_ref, cb_ref, lcg_ref, lcb_ref, wout_ref, lng_ref, lnb_ref,
                       o_ref, cache_ref, gs_ref, hc_ref, *, tm):
    hs = (K_C - 1) * BATCH

    @pl.when(pl.program_id(0) == 0)
    def _():
        gs_ref[0:hs, :] = jnp.zeros((hs, D_C), F32)

    xb = x_ref[...]
    xbf = xb.astype(BF16)
    z1 = _dot(xbf, win_ref[:, 0:D_C])
    z2 = _dot(xbf, win_ref[:, D_C:2 * D_C])
    gs_ref[hs:hs + tm, :] = z1 * jax.nn.sigmoid(z2)

    def conv_rows(i, _):
        r0 = pl.multiple_of(i * CONV_ROWS, CONV_ROWS)
        for l0 in range(0, D_C, LANES):
            cols = slice(l0, l0 + LANES)
            win = gs_ref[pl.ds(r0, CONV_ROWS + hs), cols]
            acc = jnp.broadcast_to(cb_ref[:, cols], (CONV_ROWS, LANES))
            for k in range(K_C):
                acc = acc + cw_ref[k:k + 1, cols] * win[k * BATCH:k * BATCH + CONV_ROWS, :]
            hc_ref[pl.ds(r0, CONV_ROWS), cols] = acc
        return 0

    lax.fori_loop(0, tm // CONV_ROWS, conv_rows, 0)
    h = jax.nn.silu(_ln(hc_ref[...], lcg_ref[...], lcb_ref[...]))
    out = _dot(h.astype(BF16), wout_ref[...])
    o_ref[...] = _ln(ALPHA * xb + out, lng_ref[...], lnb_ref[...])
    tail = gs_ref[tm:tm + hs, :]
    cache_ref[...] = tail
    gs_ref[0:hs, :] = tail


def _odd_prompt(x, o, l, win, cw, cb3, lcg3, lcb3, wout, lng3, lnb3, *, tm):
    hs = (K_C - 1) * BATCH
    in_specs = [pl.BlockSpec((tm, D_MODEL), lambda i: (i, 0)),
                _whole(win), _layer(cw, o), _layer(cb3, o), _layer(lcg3, o), _layer(lcb3, o),
                _whole(wout), _layer(lng3, l), _layer(lnb3, l)]
    return pl.pallas_call(
        functools.partial(_odd_prompt_kernel, tm=tm),
        grid=(M_PROMPT // tm,),
        in_specs=in_specs,
        out_specs=(pl.BlockSpec((tm, D_MODEL), lambda i: (i, 0)), pl.BlockSpec((hs, D_C), lambda i: (0, 0))),
        out_shape=(jax.ShapeDtypeStruct((M_PROMPT, D_MODEL), F32), jax.ShapeDtypeStruct((hs, D_C), F32)),
        scratch_shapes=[pltpu.VMEM((hs + tm, D_C), F32), pltpu.VMEM((tm, D_C), F32)],
        compiler_params=_params(),
        name="odd_prompt",
    )(x, win, cw, cb3, lcg3, lcb3, wout, lng3, lnb3)


def _odd_sample_kernel(x_ref, c_ref, cn_ref, win_ref, cw_ref, cb_ref, lcg_ref, lcb_ref, wout_ref, lng_ref, lnb_ref,
                       o_ref, oc_ref, g_ref, pc_ref):
    k = pl.program_id(0)
    last = (K_C - 1) // SAMPLE_TAPS - 1

    @pl.when(k == 0)
    def _():
        xbf = x_ref[...].astype(BF16)
        z1 = _dot(xbf, win_ref[:, 0:D_C])
        z2 = _dot(xbf, win_ref[:, D_C:2 * D_C])
        g_ref[...] = z1 * jax.nn.sigmoid(z2)
        pc_ref[...] = jnp.broadcast_to(cb_ref[...], pc_ref.shape)

    pc = pc_ref[...]
    for t in range(SAMPLE_TAPS):
        pc = pc + cw_ref[pl.ds(k * SAMPLE_TAPS + t, 1), :] * c_ref[t]
    pc_ref[...] = pc
    oc_ref[0:SAMPLE_TAPS - 1] = c_ref[1:SAMPLE_TAPS]

    @pl.when(k < last)
    def _():
        oc_ref[SAMPLE_TAPS - 1] = cn_ref[...]

    @pl.when(k == last)
    def _():
        g = g_ref[...]
        oc_ref[SAMPLE_TAPS - 1] = g
        hc = pc + cw_ref[K_C - 1:K_C, :] * g
        h = jax.nn.silu(_ln(hc, lcg_ref[...], lcb_ref[...]))
        out = _dot(h.astype(BF16), wout_ref[...])
        o_ref[...] = _ln(ALPHA * x_ref[...] + out, lng_ref[...], lnb_ref[...])


def _odd_sample(x, cache_t, o, l, win, cw, cb3, lcg3, lcb3, wout, lng3, lnb3):
    nk = K_C - 1
    steps = nk // SAMPLE_TAPS
    in_specs = [_whole(x),
                pl.BlockSpec((None, SAMPLE_TAPS, DEC_BATCH, D_C), lambda k: (o, k, 0, 0)),
                pl.BlockSpec((None, None, DEC_BATCH, D_C),
                             lambda k: (o, jnp.minimum((k + 1) * SAMPLE_TAPS, nk - 1), 0, 0)),
                _whole(win), _layer(cw, o), _layer(cb3, o), _layer(lcg3, o), _layer(lcb3, o),
                _whole(wout), _layer(lng3, l), _layer(lnb3, l)]
    return pl.pallas_call(
        _odd_sample_kernel,
        grid=(steps,),
        in_specs=in_specs,
        out_specs=(pl.BlockSpec((DEC_BATCH, D_MODEL), lambda k: (0, 0)),
                   pl.BlockSpec((SAMPLE_TAPS, DEC_BATCH, D_C), lambda k: (k, 0, 0))),
        out_shape=(jax.ShapeDtypeStruct((DEC_BATCH, D_MODEL), F32),
                   jax.ShapeDtypeStruct((nk, DEC_BATCH, D_C), F32)),
        scratch_shapes=[pltpu.VMEM((DEC_BATCH, D_C), F32), pltpu.VMEM((DEC_BATCH, D_C), F32)],
        compiler_params=_params(),
        name="odd_sample",
    )(x, cache_t, cache_t, win, cw, cb3, lcg3, lcb3, wout, lng3, lnb3)


def kernel(x_prompt, x_sample, state_a_re, state_a_im, cache_c_conv, cache_ffn_conv, w_in_ab, s5_lam_re, s5_lam_im, s5_log_dt, s5_b_re, s5_b_im, s5_c_re, s5_c_im, s5_d, s5_glu_w, s5_glu_b, sgu_ln_g, sgu_ln_b, sgu_w, sgu_b, w_out_ab, w_in_c, conv_c_w, conv_c_b, ln_c_g, ln_c_b, w_out_c, ffn_w_gate, ffn_w_up, ffn_conv_w, ffn_conv_b, ffn_w_down, ln_mix_g, ln_mix_b, ln_ffn_g, ln_ffn_b):
    xs = x_sample.reshape(DEC_BATCH, D_MODEL)
    xp = x_prompt

    assert N_EVEN == 1, "side-stream casts below are laid out for one even layer followed by odd layers"
    win_ab, wout_ab, gluw = w_in_ab.astype(BF16), w_out_ab.astype(BF16), s5_glu_w.astype(BF16)
    cache_c_t = jnp.transpose(cache_c_conv, (0, 2, 1, 3))
    fcb3 = _rows3(ffn_conv_b)
    d3, glub3, sg3, sb3 = _rows3(s5_d), _rows3(s5_glu_b), _rows3(sgu_ln_g), _rows3(sgu_ln_b)
    sbt = jnp.swapaxes(sgu_b, 1, 2)
    ccb3, lcg3, lcb3 = _rows3(conv_c_b), _rows3(ln_c_g), _rows3(ln_c_b)
    lmg3, lmb3, lfg3, lfb3 = _rows3(ln_mix_g), _rows3(ln_mix_b), _rows3(ln_ffn_g), _rows3(ln_ffn_b)

    def ffn_jobs(l):
        return [(ffn_w_gate, l), (ffn_w_up, l), (ffn_w_down, l)]

    sa_re_p, sa_im_p, sa_re_s, sa_im_s, sb_v_s = [], [], [], [], []
    cc_p, cc_s, cf_p, cf_s = [], [], [], []
    for l in range(DEPTH):
        if l % 2 == 0:
            e = l // 2
            lbr, lbi, bbr, bbi, cneg = _s5_prep(
                s5_lam_re[e], s5_lam_im[e], s5_log_dt[e],
                jnp.swapaxes(s5_b_re[e], 1, 2), jnp.swapaxes(s5_b_im[e], 1, 2), s5_c_im[e])
            bbd, cbd, lam = _block_diag_params(lbr, lbi, bbr, bbi, s5_c_re[e], cneg)
            shared = (win_ab, bbd, cbd, lam, d3, gluw, glub3, sg3, sb3)
            (xp, sre, sim), ffn_w = _even_prompt(xp, e, l, *shared, sgu_w, sbt, wout_ab, lmg3, lmb3,
                                                 cast_jobs=ffn_jobs(l))
            sw0 = jnp.repeat(sgu_w[e][:, 0, 0], HD_B).reshape(1, D_B)
            sb0 = jnp.repeat(sgu_b[e][:, 0], HD_B).reshape(1, D_B)
            xs, sres, sims, vn = _even_sample(
                xs, state_a_re[e].reshape(DEC_BATCH, -1), state_a_im[e].reshape(DEC_BATCH, -1),
                e, l, *shared, sw0, sb0, wout_ab, lmg3, lmb3)
            sa_re_p.append(sre.reshape(BATCH, N_GA, P_A))
            sa_im_p.append(sim.reshape(BATCH, N_GA, P_A))
            sa_re_s.append(sres.reshape(DEC_BATCH, N_GA, P_A))
            sa_im_s.append(sims.reshape(DEC_BATCH, N_GA, P_A))
            sb_v_s.append(vn.reshape(DEC_BATCH, 1, D_B))
        else:
            o = l // 2
            win_c, wout_c = odd_w
            rest = (win_c, conv_c_w, ccb3, lcg3, lcb3, wout_c, lmg3, lmb3)
            xp, cp = _odd_prompt(xp, o, l, *rest, tm=ODD_TM)
            xs, cs_t = _odd_sample(xs, cache_c_t, o, l, *rest)
            cc_p.append(jnp.transpose(cp.reshape(K_C - 1, BATCH, D_C), (1, 0, 2)))
            cc_s.append(jnp.transpose(cs_t, (1, 0, 2)))
        wg, wu, wd = ffn_w
        jobs = [(w_in_c, (l + 1) // 2), (w_out_c, (l + 1) // 2)] + ffn_jobs(l + 1) if l + 1 < DEPTH else []
        halo_s = jnp.transpose(cache_ffn_conv[l], (1, 0, 2)).reshape((K_F - 1) * DEC_BATCH, D_FF)
        (xp, fcp, xs, fcs), casts = _ffn(xp, xs, halo_s, l, wg, wu, ffn_conv_w, fcb3, wd, lfg3, lfb3, tm=FFN_TM,
                                         batch_major_out=(l == DEPTH - 1), cast_jobs=jobs)
        if jobs:
            odd_w, ffn_w = casts[:2], casts[2:]
        cf_p.append(jnp.transpose(fcp.reshape(K_F - 1, BATCH, D_FF), (1, 0, 2)))
        cf_s.append(jnp.transpose(fcs.reshape(K_F - 1, DEC_BATCH, D_FF), (1, 0, 2)))

    ys = xs.reshape(DEC_BATCH, 1, D_MODEL)
    return (xp, ys,
            jnp.stack(sa_re_p), jnp.stack(sa_im_p), jnp.stack(sa_re_s), jnp.stack(sa_im_s),
            jnp.stack(sb_v_s),
            jnp.stack(cc_p), jnp.stack(cc_s),
            jnp.stack(cf_p), jnp.stack(cf_s))
```

```python
import functools

import jax
import jax.numpy as jnp
from jax import lax
from jax.experimental import pallas as pl
from jax.experimental.pallas import tpu as pltpu

D_MODEL = 1024
BATCH = 8
SEQ = 2048
DEPTH = 2
DEC_BATCH = 128
N_EVEN = (DEPTH + 1) // 2
N_ODD = DEPTH // 2
D_A = D_MODEL // 2
S5_GROUP = 16
N_GA = D_A // S5_GROUP
P_A = 64
D_B = D_MODEL // 2
N_HB = 4
HD_B = D_B // N_HB
CHUNK = 128
D_C = D_MODEL
K_C = 31
D_FF = ((8 * D_MODEL) // 3 + 127) // 128 * 128
K_F = 3
ALPHA = (2.0 * DEPTH) ** 0.25
LN_EPS = 1e-5

F32 = jnp.float32
BF16 = jnp.bfloat16

LANES = 128
MXU_N = 256
M_PROMPT = BATCH * SEQ
ROWS_PER_CHUNK = CHUNK * BATCH
N_CG = 4
G_PER_CG = N_GA // N_CG
CH_PER_CG = G_PER_CG * S5_GROUP
ST_PER_CG = G_PER_CG * P_A
FF_CHUNKS = ((0, 6 * MXU_N), (6 * MXU_N, D_FF))
FFN_TM = 512
ODD_TM = 1024
CONV_ROWS = 8 * BATCH
SAMPLE_TAPS = 10
N_CAST_CHUNKS = 16
VMEM_LIMIT = 56 * 1024 * 1024


def _ln(x, g, b):
    mu = jnp.mean(x, axis=-1, keepdims=True)
    xc = x - mu
    var = jnp.mean(xc * xc, axis=-1, keepdims=True)
    return xc * lax.rsqrt(var + LN_EPS) * g + b


def _dot(a, b):
    return jnp.dot(a, b, preferred_element_type=F32)


def _whole(arr):
    nd = arr.ndim
    return pl.BlockSpec(arr.shape, lambda *_: (0,) * nd, pipeline_mode=pl.Buffered(1))


def _layer(arr, l):
    nd = arr.ndim
    return pl.BlockSpec((None,) + arr.shape[1:], lambda *_: (l,) + (0,) * (nd - 1),
                        pipeline_mode=pl.Buffered(1))


def _rows3(v):
    return v.reshape(v.shape[0], 1, v.shape[1])


def _params(n_grid=1):
    return pltpu.CompilerParams(dimension_semantics=("arbitrary",) * n_grid,
                                vmem_limit_bytes=VMEM_LIMIT)


def _cast_specs(jobs, n_steps):
    def chunk(i):
        return jnp.minimum(i * N_CAST_CHUNKS // n_steps, N_CAST_CHUNKS - 1)

    in_specs, out_specs, out_shapes = [], [], []
    for w, l in jobs:
        _, r, c = w.shape
        rows = r // N_CAST_CHUNKS
        in_specs.append(pl.BlockSpec((None, rows, c), lambda i, l=l: (l, chunk(i), 0)))
        out_specs.append(pl.BlockSpec((rows, c), lambda i: (chunk(i), 0)))
        out_shapes.append(jax.ShapeDtypeStruct((r, c), BF16))
    return in_specs, out_specs, out_shapes


def _cast_chunks(in_refs, out_refs):
    for w_ref, o_ref in zip(in_refs, out_refs):
        o_ref[...] = w_ref[...].astype(BF16)


def _s5_prep_kernel(lr_ref, li_ref, ldt_ref, br_ref, bi_ref, ci_ref,
                    lbr_ref, lbi_ref, bbr_ref, bbi_ref, cneg_ref):
    lr = lr_ref[...]
    li = li_ref[...]
    dt = jnp.exp(ldt_ref[...])
    mag = jnp.exp(lr * dt)
    lbr = mag * jnp.cos(li * dt)
    lbi = mag * jnp.sin(li * dt)
    lbr_ref[...] = lbr
    lbi_ref[...] = lbi
    nr = lbr - 1.0
    ni = lbi
    den = lr * lr + li * li
    qr = (nr * lr + ni * li) / den
    qi = (ni * lr - nr * li) / den
    br = br_ref[...]
    bi = bi_ref[...]
    qr3 = qr[:, None, :]
    qi3 = qi[:, None, :]
    bbr_ref[...] = qr3 * br - qi3 * bi
    bbi_ref[...] = qr3 * bi + qi3 * br
    cneg_ref[...] = -ci_ref[...]


def _s5_prep(lam_re, lam_im, log_dt, b_re, b_im, c_im):
    gp = jax.ShapeDtypeStruct((N_GA, P_A), F32)
    gcp = jax.ShapeDtypeStruct((N_GA, S5_GROUP, P_A), F32)
    return pl.pallas_call(
        _s5_prep_kernel,
        out_shape=(gp, gp, gcp, gcp, gcp),
        name="s5_prep",
    )(lam_re, lam_im, log_dt.reshape(N_GA, 1), b_re, b_im, c_im)


def _block_diag_params(lbr, lbi, bbr, bbi, c_re, cneg):
    eye = jnp.eye(G_PER_CG, dtype=F32)

    def in_mat(b):
        b4 = b.reshape(N_CG, G_PER_CG, S5_GROUP, P_A)
        return jnp.einsum('jgcp,gh->jgchp', b4, eye).reshape(N_CG, CH_PER_CG, ST_PER_CG)

    def out_mat(c):
        c4 = c.reshape(N_CG, G_PER_CG, S5_GROUP, P_A)
        return jnp.einsum('jgcp,gh->jgphc', c4, eye).reshape(N_CG, ST_PER_CG, CH_PER_CG)

    bbd = jnp.concatenate([in_mat(bbr), in_mat(bbi)], axis=2).astype(BF16)
    cbd = jnp.concatenate([out_mat(c_re), out_mat(cneg)], axis=1).astype(BF16)
    lam = jnp.stack([lbr.reshape(-1), lbi.reshape(-1)])
    return bbd, cbd, lam


def _even_prompt_kernel(x_ref, win_ref, bbd_ref, cbd_ref, lam_ref, d_ref, gluw_ref, glub_ref,
                        sg_ref, sb_ref, sw_ref, sbt_ref, wout_ref, lng_ref, lnb_ref, *rest, n_cast):
    cast_in, rest = rest[:n_cast], rest[n_cast:]
    o_ref, sre_ref, sim_ref = rest[:3]
    cast_out, rest = rest[3:3 + n_cast], rest[3 + n_cast:]
    st_ref, xt_ref, scr_ref, cat_ref, vn_ref, gt_ref = rest

    @pl.when(pl.program_id(0) == 0)
    def _():
        st_ref[...] = jnp.zeros_like(st_ref)

    _cast_chunks(cast_in, cast_out)

    for c in range(D_MODEL // LANES):
        for b in range(BATCH):
            xt_ref.at[c][pl.ds(b, CHUNK, stride=BATCH), :] = x_ref[b, :, LANES * c:LANES * (c + 1)]
    xb = jnp.concatenate([xt_ref[c] for c in range(D_MODEL // LANES)], axis=-1)
    xbf = xb.astype(BF16)

    ua = _dot(xbf, win_ref[:, 0:D_A])
    uab = ua.astype(BF16)
    ys = []
    for j in range(N_CG):
        re_cols = slice(2 * ST_PER_CG * j, 2 * ST_PER_CG * j + ST_PER_CG)
        im_cols = slice(2 * ST_PER_CG * j + ST_PER_CG, 2 * ST_PER_CG * (j + 1))
        scr_ref[...] = _dot(uab[:, CH_PER_CG * j:CH_PER_CG * (j + 1)], bbd_ref[j])
        lr = jnp.broadcast_to(lam_ref[0:1, ST_PER_CG * j:ST_PER_CG * (j + 1)], (BATCH, ST_PER_CG))
        li = jnp.broadcast_to(lam_ref[1:2, ST_PER_CG * j:ST_PER_CG * (j + 1)], (BATCH, ST_PER_CG))

        def step(t, carry, lr=lr, li=li):
            sr, si = carry
            row = pl.multiple_of(t * BATCH, BATCH)
            nr = lr * sr - li * si + scr_ref[pl.ds(row, BATCH), 0:ST_PER_CG]
            ni = lr * si + li * sr + scr_ref[pl.ds(row, BATCH), ST_PER_CG:2 * ST_PER_CG]
            scr_ref[pl.ds(row, BATCH), 0:ST_PER_CG] = nr
            scr_ref[pl.ds(row, BATCH), ST_PER_CG:2 * ST_PER_CG] = ni
            return nr, ni

        sr, si = lax.fori_loop(0, CHUNK, step, (st_ref[:, re_cols], st_ref[:, im_cols]), unroll=4)
        st_ref[:, re_cols] = sr
        st_ref[:, im_cols] = si
        sre_ref[:, ST_PER_CG * j:ST_PER_CG * (j + 1)] = sr
        sim_ref[:, ST_PER_CG * j:ST_PER_CG * (j + 1)] = si
        ys.append(_dot(scr_ref[...].astype(BF16), cbd_ref[j]))
    y = jnp.concatenate(ys, axis=-1) + d_ref[...] * ua
    g = jax.nn.gelu(y, approximate=True)
    ya = g * jax.nn.sigmoid(_dot(g.astype(BF16), gluw_ref[...]) + glub_ref[...])
    cat_ref[:, 0:D_A] = ya.astype(BF16)

    ub = _dot(xbf, win_ref[:, D_A:D_A + D_B])
    vb = _dot(xbf, win_ref[:, D_A + D_B:D_A + 2 * D_B])
    vn = _ln(vb, sg_ref[...], sb_ref[...])
    for h in range(N_HB):
        vn_ref[h] = vn[:, HD_B * h:HD_B * (h + 1)]
    r_id = lax.broadcasted_iota(jnp.int32, (CHUNK, CHUNK), 0)
    c_id = lax.broadcasted_iota(jnp.int32, (CHUNK, CHUNK), 1)
    tril = (c_id <= r_id).astype(F32)
    wm = [(sw_ref[h] * tril).astype(BF16) for h in range(N_HB)]
    for b in range(0, BATCH, 2):
        for h in range(N_HB):
            v2 = jnp.concatenate([vn_ref.at[h][pl.ds(b, CHUNK, stride=BATCH), :],
                                  vn_ref.at[h][pl.ds(b + 1, CHUNK, stride=BATCH), :]], axis=-1)
            g2 = _dot(wm[h], v2.astype(BF16)) + sbt_ref[:, h:h + 1]
            gt_ref.at[h][pl.ds(b, CHUNK, stride=BATCH), :] = g2[:, 0:HD_B]
            gt_ref.at[h][pl.ds(b + 1, CHUNK, stride=BATCH), :] = g2[:, HD_B:]
    gate = jnp.concatenate([gt_ref[h] for h in range(N_HB)], axis=-1)
    cat_ref[:, D_A:] = (ub * gate).astype(BF16)

    out = _dot(cat_ref[...], wout_ref[...])
    o_ref[...] = _ln(ALPHA * xb + out, lng_ref[...], lnb_ref[...])


def _even_prompt(x, e, l, win, bbd, cbd, lam, d3, gluw, glub3, sg3, sb3, sw, sbt, wout, lng3, lnb3, *, cast_jobs=()):
    rows = ROWS_PER_CHUNK
    n = SEQ // CHUNK
    c_in, c_out, c_shapes = _cast_specs(cast_jobs, n)
    in_specs = [pl.BlockSpec((BATCH, CHUNK, D_MODEL), lambda i: (0, i, 0)),
                _layer(win, e), _whole(bbd), _whole(cbd), _whole(lam), _layer(d3, e), _layer(gluw, e),
                _layer(glub3, e), _layer(sg3, e), _layer(sb3, e), _layer(sw, e), _layer(sbt, e),
                _layer(wout, e), _layer(lng3, l), _layer(lnb3, l)] + c_in
    st = jax.ShapeDtypeStruct((BATCH, N_GA * P_A), F32)
    st_spec = pl.BlockSpec((BATCH, N_GA * P_A), lambda i: (0, 0))
    outs = pl.pallas_call(
        functools.partial(_even_prompt_kernel, n_cast=len(cast_jobs)),
        grid=(n,),
        in_specs=in_specs,
        out_specs=(pl.BlockSpec((rows, D_MODEL), lambda i: (i, 0)), st_spec, st_spec, *c_out),
        out_shape=(jax.ShapeDtypeStruct((M_PROMPT, D_MODEL), F32), st, st, *c_shapes),
        scratch_shapes=[pltpu.VMEM((BATCH, 2 * N_GA * P_A), F32),
                        pltpu.VMEM((D_MODEL // LANES, rows, LANES), F32),
                        pltpu.VMEM((rows, 2 * ST_PER_CG), F32),
                        pltpu.VMEM((rows, D_MODEL), BF16),
                        pltpu.VMEM((N_HB, rows, HD_B), F32),
                        pltpu.VMEM((N_HB, rows, HD_B), F32)],
        compiler_params=_params(),
        name="even_prompt",
    )(x, win, bbd, cbd, lam, d3, gluw, glub3, sg3, sb3, sw, sbt, wout, lng3, lnb3, *[w for w, _ in cast_jobs])
    return outs[:3], outs[3:]


def _even_sample_kernel(x_ref, s0r_ref, s0i_ref, win_ref, bbd_ref, cbd_ref, lam_ref, d_ref,
                        gluw_ref, glub_ref, sg_ref, sb_ref, sw0_ref, sb0_ref, wout_ref, lng_ref, lnb_ref,
                        o_ref, sre_ref, sim_ref, vn_ref):
    xb = x_ref[...]
    xbf = xb.astype(BF16)
    ua = _dot(xbf, win_ref[:, 0:D_A])
    uab = ua.astype(BF16)
    ys = []
    for j in range(N_CG):
        cols = slice(ST_PER_CG * j, ST_PER_CG * (j + 1))
        bu = _dot(uab[:, CH_PER_CG * j:CH_PER_CG * (j + 1)], bbd_ref[j])
        lr = lam_ref[0:1, cols]
        li = lam_ref[1:2, cols]
        s0r = s0r_ref[:, cols]
        s0i = s0i_ref[:, cols]
        sr = lr * s0r - li * s0i + bu[:, 0:ST_PER_CG]
        si = lr * s0i + li * s0r + bu[:, ST_PER_CG:]
        sre_ref[:, cols] = sr
        sim_ref[:, cols] = si
        ys.append(_dot(jnp.concatenate([sr, si], axis=-1).astype(BF16), cbd_ref[j]))
    y = jnp.concatenate(ys, axis=-1) + d_ref[...] * ua
    g = jax.nn.gelu(y, approximate=True)
    ya = g * jax.nn.sigmoid(_dot(g.astype(BF16), gluw_ref[...]) + glub_ref[...])

    ub = _dot(xbf, win_ref[:, D_A:D_A + D_B])
    vb = _dot(xbf, win_ref[:, D_A + D_B:D_A + 2 * D_B])
    vn = _ln(vb, sg_ref[...], sb_ref[...])
    vn_ref[...] = vn
    yb = ub * (sw0_ref[...] * vn + sb0_ref[...])

    cat = jnp.concatenate([ya, yb], axis=-1).astype(BF16)
    out = _dot(cat, wout_ref[...])
    o_ref[...] = _ln(ALPHA * xb + out, lng_ref[...], lnb_ref[...])


def _even_sample(x, s0r, s0i, e, l, win, bbd, cbd, lam, d3, gluw, glub3, sg3, sb3, sw0, sb0, wout, lng3, lnb3):
    in_specs = [_whole(x), _whole(s0r), _whole(s0i),
                _layer(win, e), _whole(bbd), _whole(cbd), _whole(lam), _layer(d3, e), _layer(gluw, e),
                _layer(glub3, e), _layer(sg3, e), _layer(sb3, e), _whole(sw0), _whole(sb0),
                _layer(wout, e), _layer(lng3, l), _layer(lnb3, l)]
    shapes = ((DEC_BATCH, D_MODEL), (DEC_BATCH, N_GA * P_A), (DEC_BATCH, N_GA * P_A), (DEC_BATCH, D_B))
    return pl.pallas_call(
        _even_sample_kernel,
        grid=(1,),
        in_specs=in_specs,
        out_specs=tuple(pl.BlockSpec(s, lambda i: (0, 0)) for s in shapes),
        out_shape=tuple(jax.ShapeDtypeStruct(s, F32) for s in shapes),
        compiler_params=_params(),
        name="even_sample",
    )(x, s0r, s0i, win, bbd, cbd, lam, d3, gluw, glub3, sg3, sb3, sw0, sb0, wout, lng3, lnb3)


def _store_batch_major(y, o_ref, yt_ref, tm):
    for c in range(D_MODEL // LANES):
        yt_ref[c] = y[:, LANES * c:LANES * (c + 1)]
    for c in range(D_MODEL // LANES):
        for b in range(BATCH):
            o_ref[b, :, LANES * c:LANES * (c + 1)] = yt_ref.at[c][pl.ds(b, tm // BATCH, stride=BATCH), :]


def _ffn_tile(xb, gs_ref, rows, shift, wg_ref, wu_ref, cw_ref, cb_ref, wd_ref, lng_ref, lnb_ref):
    hs = (K_F - 1) * shift
    xbf = xb.astype(BF16)
    acc = None
    for c0, c1 in FF_CHUNKS:
        gs_ref[hs:hs + rows, c0:c1] = _dot(xbf, wg_ref[:, c0:c1])
        conv = cb_ref[:, c0:c1]
        for k in range(K_F):
            conv = conv + cw_ref[k:k + 1, c0:c1] * gs_ref[k * shift:k * shift + rows, c0:c1]
        up = _dot(xbf, wu_ref[:, c0:c1])
        h = (jax.nn.silu(conv) * up).astype(BF16)
        part = _dot(h, wd_ref[c0:c1, :])
        acc = part if acc is None else acc + part
    return _ln(ALPHA * xb + acc, lng_ref[...], lnb_ref[...])


def _ffn_kernel(x_ref, xs_ref, halo_s_ref, wg_ref, wu_ref, cw_ref, cb_ref, wd_ref, lng_ref, lnb_ref, *rest,
                tm, n, n_cast, batch_major_out):
    cast_in, rest = rest[:n_cast], rest[n_cast:]
    o_ref, cache_ref, os_ref, cache_s_ref = rest[:4]
    cast_out, rest = rest[4:4 + n_cast], rest[4 + n_cast:]
    gs_ref = rest[0]
    i = pl.program_id(0)
    hs = (K_F - 1) * BATCH
    hs_s = (K_F - 1) * DEC_BATCH
    weights = (wg_ref, wu_ref, cw_ref, cb_ref, wd_ref, lng_ref, lnb_ref)

    @pl.when(i == 0)
    def _():
        gs_ref[0:hs, :] = jnp.zeros((hs, D_FF), F32)

    @pl.when(i < n)
    def _():
        _cast_chunks(cast_in, cast_out)
        y = _ffn_tile(x_ref[...], gs_ref, tm, BATCH, *weights)
        if batch_major_out:
            _store_batch_major(y, o_ref, rest[1], tm)
        else:
            o_ref[...] = y
        tail = gs_ref[tm:tm + hs, :]
        cache_ref[...] = tail
        gs_ref[0:hs, :] = tail

    @pl.when(i == n)
    def _():
        gs_ref[0:hs_s, :] = halo_s_ref[...]
        os_ref[...] = _ffn_tile(xs_ref[...], gs_ref, DEC_BATCH, DEC_BATCH, *weights)
        cache_s_ref[...] = gs_ref[DEC_BATCH:DEC_BATCH + hs_s, :]


def _ffn(x, xs, halo_s, l, wg, wu, cw, cb3, wd, lng3, lnb3, *, tm, batch_major_out=False, cast_jobs=()):
    m = x.shape[0]
    n = m // tm
    hs = (K_F - 1) * BATCH
    hs_s = (K_F - 1) * DEC_BATCH
    c_in, c_out, c_shapes = _cast_specs(cast_jobs, n)
    in_specs = [pl.BlockSpec((tm, D_MODEL), lambda i: (jnp.minimum(i, n - 1), 0)), _whole(xs), _whole(halo_s),
                _whole(wg), _whole(wu), _layer(cw, l), _layer(cb3, l), _whole(wd),
                _layer(lng3, l), _layer(lnb3, l)] + c_in
    scratch = [pltpu.VMEM((max(hs + tm, hs_s + DEC_BATCH), D_FF), F32)]
    if batch_major_out:
        o_spec = pl.BlockSpec((BATCH, tm // BATCH, D_MODEL), lambda i: (0, jnp.minimum(i, n - 1), 0))
        o_shape = jax.ShapeDtypeStruct((BATCH, m // BATCH, D_MODEL), F32)
        scratch.append(pltpu.VMEM((D_MODEL // LANES, tm, LANES), F32))
    else:
        o_spec = pl.BlockSpec((tm, D_MODEL), lambda i: (jnp.minimum(i, n - 1), 0))
        o_shape = jax.ShapeDtypeStruct((m, D_MODEL), F32)
    outs = pl.pallas_call(
        functools.partial(_ffn_kernel, tm=tm, n=n, n_cast=len(cast_jobs), batch_major_out=batch_major_out),
        grid=(n + 1,),
        in_specs=in_specs,
        out_specs=(o_spec, pl.BlockSpec((hs, D_FF), lambda i: (0, 0)),
                   pl.BlockSpec((DEC_BATCH, D_MODEL), lambda i: (0, 0)),
                   pl.BlockSpec((hs_s, D_FF), lambda i: (0, 0)), *c_out),
        out_shape=(o_shape, jax.ShapeDtypeStruct((hs, D_FF), F32),
                   jax.ShapeDtypeStruct((DEC_BATCH, D_MODEL), F32),
                   jax.ShapeDtypeStruct((hs_s, D_FF), F32), *c_shapes),
        scratch_shapes=scratch,
        compiler_params=_params(),
        name="conv_ffn",
    )(x, xs, halo_s, wg, wu, cw, cb3, wd, lng3, lnb3, *[w for w, _ in cast_jobs])
    return outs[:4], outs[4:]


def _odd_prompt_kernel(x_ref, win_ref, cw_ref, cb_ref, lcg_ref, lcb_ref, wout_ref, lng_ref, lnb_ref, *rest,
                       tm, n_cast):
    cast_in, rest = rest[:n_cast], rest[n_cast:]
    o_ref, cache_ref = rest[:2]
    cast_out, rest = rest[2:2 + n_cast], rest[2 + n_cast:]
    gs_ref, hc_ref = rest
    hs = (K_C - 1) * BATCH

    @pl.when(pl.program_id(0) == 0)
    def _():
        gs_ref[0:hs, :] = jnp.zeros((hs, D_C), F32)

    _cast_chunks(cast_in, cast_out)
    xb = x_ref[...]
    xbf = xb.astype(BF16)
    z1 = _dot(xbf, win_ref[:, 0:D_C])
    z2 = _dot(xbf, win_ref[:, D_C:2 * D_C])
    gs_ref[hs:hs + tm, :] = z1 * jax.nn.sigmoid(z2)

    def conv_rows(i, _):
        r0 = pl.multiple_of(i * CONV_ROWS, CONV_ROWS)
        for l0 in range(0, D_C, LANES):
            cols = slice(l0, l0 + LANES)
            win = gs_ref[pl.ds(r0, CONV_ROWS + hs), cols]
            acc = jnp.broadcast_to(cb_ref[:, cols], (CONV_ROWS, LANES))
            for k in range(K_C):
                acc = acc + cw_ref[k:k + 1, cols] * win[k * BATCH:k * BATCH + CONV_ROWS, :]
            hc_ref[pl.ds(r0, CONV_ROWS), cols] = acc
        return 0

    lax.fori_loop(0, tm // CONV_ROWS, conv_rows, 0)
    h = jax.nn.silu(_ln(hc_ref[...], lcg_ref[...], lcb_ref[...]))
    out = _dot(h.astype(BF16), wout_ref[...])
    o_ref[...] = _ln(ALPHA * xb + out, lng_ref[...], lnb_ref[...])
    tail = gs_ref[tm:tm + hs, :]
    cache_ref[...] = tail
    gs_ref[0:hs, :] = tail


def _odd_prompt(x, o, l, win, cw, cb3, lcg3, lcb3, wout, lng3, lnb3, *, tm, cast_jobs=()):
    hs = (K_C - 1) * BATCH
    n = M_PROMPT // tm
    c_in, c_out, c_shapes = _cast_specs(cast_jobs, n)
    in_specs = [pl.BlockSpec((tm, D_MODEL), lambda i: (i, 0)),
                _whole(win), _layer(cw, o), _layer(cb3, o), _layer(lcg3, o), _layer(lcb3, o),
                _whole(wout), _layer(lng3, l), _layer(lnb3, l)] + c_in
    outs = pl.pallas_call(
        functools.partial(_odd_prompt_kernel, tm=tm, n_cast=len(cast_jobs)),
        grid=(n,),
        in_specs=in_specs,
        out_specs=(pl.BlockSpec((tm, D_MODEL), lambda i: (i, 0)), pl.BlockSpec((hs, D_C), lambda i: (0, 0)), *c_out),
        out_shape=(jax.ShapeDtypeStruct((M_PROMPT, D_MODEL), F32), jax.ShapeDtypeStruct((hs, D_C), F32), *c_shapes),
        scratch_shapes=[pltpu.VMEM((hs + tm, D_C), F32), pltpu.VMEM((tm, D_C), F32)],
        compiler_params=_params(),
        name="odd_prompt",
    )(x, win, cw, cb3, lcg3, lcb3, wout, lng3, lnb3, *[w for w, _ in cast_jobs])
    return outs[:2], outs[2:]


def _odd_sample_kernel(x_ref, c_ref, cn_ref, win_ref, cw_ref, cb_ref, lcg_ref, lcb_ref, wout_ref, lng_ref, lnb_ref,
                       o_ref, oc_ref, g_ref, pc_ref):
    k = pl.program_id(0)
    last = (K_C - 1) // SAMPLE_TAPS - 1

    @pl.when(k == 0)
    def _():
        xbf = x_ref[...].astype(BF16)
        z1 = _dot(xbf, win_ref[:, 0:D_C])
        z2 = _dot(xbf, win_ref[:, D_C:2 * D_C])
        g_ref[...] = z1 * jax.nn.sigmoid(z2)
        pc_ref[...] = jnp.broadcast_to(cb_ref[...], pc_ref.shape)

    pc = pc_ref[...]
    for t in range(SAMPLE_TAPS):
        pc = pc + cw_ref[pl.ds(k * SAMPLE_TAPS + t, 1), :] * c_ref[t]
    pc_ref[...] = pc
    oc_ref[0:SAMPLE_TAPS - 1] = c_ref[1:SAMPLE_TAPS]

    @pl.when(k < last)
    def _():
        oc_ref[SAMPLE_TAPS - 1] = cn_ref[...]

    @pl.when(k == last)
    def _():
        g = g_ref[...]
        oc_ref[SAMPLE_TAPS - 1] = g
        hc = pc + cw_ref[K_C - 1:K_C, :] * g
        h = jax.nn.silu(_ln(hc, lcg_ref[...], lcb_ref[...]))
        out = _dot(h.astype(BF16), wout_ref[...])
        o_ref[...] = _ln(ALPHA * x_ref[...] + out, lng_ref[...], lnb_ref[...])


def _odd_sample(x, cache_t, o, l, win, cw, cb3, lcg3, lcb3, wout, lng3, lnb3):
    nk = K_C - 1
    steps = nk // SAMPLE_TAPS
    in_specs = [_whole(x),
                pl.BlockSpec((None, SAMPLE_TAPS, DEC_BATCH, D_C), lambda k: (o, k, 0, 0)),
                pl.BlockSpec((None, None, DEC_BATCH, D_C),
                             lambda k: (o, jnp.minimum((k + 1) * SAMPLE_TAPS, nk - 1), 0, 0)),
                _whole(win), _layer(cw, o), _layer(cb3, o), _layer(lcg3, o), _layer(lcb3, o),
                _whole(wout), _layer(lng3, l), _layer(lnb3, l)]
    return pl.pallas_call(
        _odd_sample_kernel,
        grid=(steps,),
        in_specs=in_specs,
        out_specs=(pl.BlockSpec((DEC_BATCH, D_MODEL), lambda k: (0, 0)),
                   pl.BlockSpec((SAMPLE_TAPS, DEC_BATCH, D_C), lambda k: (k, 0, 0))),
        out_shape=(jax.ShapeDtypeStruct((DEC_BATCH, D_MODEL), F32),
                   jax.ShapeDtypeStruct((nk, DEC_BATCH, D_C), F32)),
        scratch_shapes=[pltpu.VMEM((DEC_BATCH, D_C), F32), pltpu.VMEM((DEC_BATCH, D_C), F32)],
        compiler_params=_params(),
        name="odd_sample",
    )(x, cache_t, cache_t, win, cw, cb3, lcg3, lcb3, wout, lng3, lnb3)


def kernel(x_prompt, x_sample, state_a_re, state_a_im, cache_c_conv, cache_ffn_conv, w_in_ab, s5_lam_re, s5_lam_im, s5_log_dt, s5_b_re, s5_b_im, s5_c_re, s5_c_im, s5_d, s5_glu_w, s5_glu_b, sgu_ln_g, sgu_ln_b, sgu_w, sgu_b, w_out_ab, w_in_c, conv_c_w, conv_c_b, ln_c_g, ln_c_b, w_out_c, ffn_w_gate, ffn_w_up, ffn_conv_w, ffn_conv_b, ffn_w_down, ln_mix_g, ln_mix_b, ln_ffn_g, ln_ffn_b):
    xs = x_sample.reshape(DEC_BATCH, D_MODEL)
    xp = x_prompt

    assert N_EVEN == 1, "side-stream casts below are laid out for one even layer followed by odd layers"
    win_ab, wout_ab, gluw = w_in_ab.astype(BF16), w_out_ab.astype(BF16), s5_glu_w.astype(BF16)
    cache_c_t = jnp.transpose(cache_c_conv, (0, 2, 1, 3))
    fcb3 = _rows3(ffn_conv_b)
    d3, glub3, sg3, sb3 = _rows3(s5_d), _rows3(s5_glu_b), _rows3(sgu_ln_g), _rows3(sgu_ln_b)
    sbt = jnp.swapaxes(sgu_b, 1, 2)
    ccb3, lcg3, lcb3 = _rows3(conv_c_b), _rows3(ln_c_g), _rows3(ln_c_b)
    lmg3, lmb3, lfg3, lfb3 = _rows3(ln_mix_g), _rows3(ln_mix_b), _rows3(ln_ffn_g), _rows3(ln_ffn_b)

    def ffn_jobs(l):
        return [(ffn_w_gate, l), (ffn_w_up, l), (ffn_w_down, l)]

    sa_re_p, sa_im_p, sa_re_s, sa_im_s, sb_v_s = [], [], [], [], []
    cc_p, cc_s, cf_p, cf_s = [], [], [], []
    for l in range(DEPTH):
        if l % 2 == 0:
            e = l // 2
            lbr, lbi, bbr, bbi, cneg = _s5_prep(
                s5_lam_re[e], s5_lam_im[e], s5_log_dt[e],
                jnp.swapaxes(s5_b_re[e], 1, 2), jnp.swapaxes(s5_b_im[e], 1, 2), s5_c_im[e])
            bbd, cbd, lam = _block_diag_params(lbr, lbi, bbr, bbi, s5_c_re[e], cneg)
            shared = (win_ab, bbd, cbd, lam, d3, gluw, glub3, sg3, sb3)
            (xp, sre, sim), ffn_w = _even_prompt(xp, e, l, *shared, sgu_w, sbt, wout_ab, lmg3, lmb3,
                                                 cast_jobs=ffn_jobs(l))
            sw0 = jnp.repeat(sgu_w[e][:, 0, 0], HD_B).reshape(1, D_B)
            sb0 = jnp.repeat(sgu_b[e][:, 0], HD_B).reshape(1, D_B)
            xs, sres, sims, vn = _even_sample(
                xs, state_a_re[e].reshape(DEC_BATCH, -1), state_a_im[e].reshape(DEC_BATCH, -1),
                e, l, *shared, sw0, sb0, wout_ab, lmg3, lmb3)
            sa_re_p.append(sre.reshape(BATCH, N_GA, P_A))
            sa_im_p.append(sim.reshape(BATCH, N_GA, P_A))
            sa_re_s.append(sres.reshape(DEC_BATCH, N_GA, P_A))
            sa_im_s.append(sims.reshape(DEC_BATCH, N_GA, P_A))
            sb_v_s.append(vn.reshape(DEC_BATCH, 1, D_B))
        else:
            o = l // 2
            win_c, wout_c = odd_w
            rest = (win_c, conv_c_w, ccb3, lcg3, lcb3, wout_c, lmg3, lmb3)
            (xp, cp), ffn_w = _odd_prompt(xp, o, l, *rest, tm=ODD_TM, cast_jobs=ffn_jobs(l))
            xs, cs_t = _odd_sample(xs, cache_c_t, o, l, *rest)
            cc_p.append(jnp.transpose(cp.reshape(K_C - 1, BATCH, D_C), (1, 0, 2)))
            cc_s.append(jnp.transpose(cs_t, (1, 0, 2)))
        wg, wu, wd = ffn_w
        jobs = [(w_in_c, (l + 1) // 2), (w_out_c, (l + 1) // 2)] if l + 1 < DEPTH else []
        halo_s = jnp.transpose(cache_ffn_conv[l], (1, 0, 2)).reshape((K_F - 1) * DEC_BATCH, D_FF)
        (xp, fcp, xs, fcs), casts = _ffn(xp, xs, halo_s, l, wg, wu, ffn_conv_w, fcb3, wd, lfg3, lfb3, tm=FFN_TM,
                                         batch_major_out=(l == DEPTH - 1), cast_jobs=jobs)
        if jobs:
            odd_w = casts
        cf_p.append(jnp.transpose(fcp.reshape(K_F - 1, BATCH, D_FF), (1, 0, 2)))
        cf_s.append(jnp.transpose(fcs.reshape(K_F - 1, DEC_BATCH, D_FF), (1, 0, 2)))

    ys = xs.reshape(DEC_BATCH, 1, D_MODEL)
    return (xp, ys,
            jnp.stack(sa_re_p), jnp.stack(sa_im_p), jnp.stack(sa_re_s), jnp.stack(sa_im_s),
            jnp.stack(sb_v_s),
            jnp.stack(cc_p), jnp.stack(cc_s),
            jnp.stack(cf_p), jnp.stack(cf_s))
```

```python
import functools

import jax
import jax.numpy as jnp
from jax import lax
from jax.experimental import pallas as pl
from jax.experimental.pallas import tpu as pltpu

D_MODEL = 1024
BATCH = 8
SEQ = 2048
DEPTH = 2
DEC_BATCH = 128
N_EVEN = (DEPTH + 1) // 2
N_ODD = DEPTH // 2
D_A = D_MODEL // 2
S5_GROUP = 16
N_GA = D_A // S5_GROUP
P_A = 64
D_B = D_MODEL // 2
N_HB = 4
HD_B = D_B // N_HB
CHUNK = 128
D_C = D_MODEL
K_C = 31
D_FF = ((8 * D_MODEL) // 3 + 127) // 128 * 128
K_F = 3
ALPHA = (2.0 * DEPTH) ** 0.25
LN_EPS = 1e-5

F32 = jnp.float32
BF16 = jnp.bfloat16

LANES = 128
MXU_N = 256
M_PROMPT = BATCH * SEQ
ROWS_PER_CHUNK = CHUNK * BATCH
N_CG = 4
G_PER_CG = N_GA // N_CG
CH_PER_CG = G_PER_CG * S5_GROUP
ST_PER_CG = G_PER_CG * P_A
FF_CHUNKS = ((0, 6 * MXU_N), (6 * MXU_N, D_FF))
FFN_TM = 512
ODD_TM = 1024
CONV_ROWS = 8 * BATCH
SAMPLE_TAPS = 5
N_CAST_CHUNKS = 16
VMEM_LIMIT = 56 * 1024 * 1024


def _ln(x, g, b):
    mu = jnp.mean(x, axis=-1, keepdims=True)
    xc = x - mu
    var = jnp.mean(xc * xc, axis=-1, keepdims=True)
    return xc * lax.rsqrt(var + LN_EPS) * g + b


def _dot(a, b):
    return jnp.dot(a, b, preferred_element_type=F32)


def _whole(arr):
    nd = arr.ndim
    return pl.BlockSpec(arr.shape, lambda *_: (0,) * nd, pipeline_mode=pl.Buffered(1))


def _layer(arr, l):
    nd = arr.ndim
    return pl.BlockSpec((None,) + arr.shape[1:], lambda *_: (l,) + (0,) * (nd - 1),
                        pipeline_mode=pl.Buffered(1))


def _rows3(v):
    return v.reshape(v.shape[0], 1, v.shape[1])


def _params(n_grid=1):
    return pltpu.CompilerParams(dimension_semantics=("arbitrary",) * n_grid,
                                vmem_limit_bytes=VMEM_LIMIT)


def _cast_specs(jobs, n_steps):
    def chunk(i):
        return jnp.minimum(i * N_CAST_CHUNKS // n_steps, N_CAST_CHUNKS - 1)

    in_specs, out_specs, out_shapes = [], [], []
    for w, l in jobs:
        _, r, c = w.shape
        rows = r // N_CAST_CHUNKS
        in_specs.append(pl.BlockSpec((None, rows, c), lambda i, l=l: (l, chunk(i), 0)))
        out_specs.append(pl.BlockSpec((rows, c), lambda i: (chunk(i), 0)))
        out_shapes.append(jax.ShapeDtypeStruct((r, c), BF16))
    return in_specs, out_specs, out_shapes


def _cast_chunks(in_refs, out_refs):
    for w_ref, o_ref in zip(in_refs, out_refs):
        o_ref[...] = w_ref[...].astype(BF16)


def _s5_prep_kernel(lr_ref, li_ref, ldt_ref, br_ref, bi_ref, ci_ref,
                    lbr_ref, lbi_ref, bbr_ref, bbi_ref, cneg_ref):
    lr = lr_ref[...]
    li = li_ref[...]
    dt = jnp.exp(ldt_ref[...])
    mag = jnp.exp(lr * dt)
    lbr = mag * jnp.cos(li * dt)
    lbi = mag * jnp.sin(li * dt)
    lbr_ref[...] = lbr
    lbi_ref[...] = lbi
    nr = lbr - 1.0
    ni = lbi
    den = lr * lr + li * li
    qr = (nr * lr + ni * li) / den
    qi = (ni * lr - nr * li) / den
    br = br_ref[...]
    bi = bi_ref[...]
    qr3 = qr[:, None, :]
    qi3 = qi[:, None, :]
    bbr_ref[...] = qr3 * br - qi3 * bi
    bbi_ref[...] = qr3 * bi + qi3 * br
    cneg_ref[...] = -ci_ref[...]


def _s5_prep(lam_re, lam_im, log_dt, b_re, b_im, c_im):
    gp = jax.ShapeDtypeStruct((N_GA, P_A), F32)
    gcp = jax.ShapeDtypeStruct((N_GA, S5_GROUP, P_A), F32)
    return pl.pallas_call(
        _s5_prep_kernel,
        out_shape=(gp, gp, gcp, gcp, gcp),
        name="s5_prep",
    )(lam_re, lam_im, log_dt.reshape(N_GA, 1), b_re, b_im, c_im)


def _block_diag_params(lbr, lbi, bbr, bbi, c_re, cneg):
    eye = jnp.eye(G_PER_CG, dtype=F32)

    def in_mat(b):
        b4 = b.reshape(N_CG, G_PER_CG, S5_GROUP, P_A)
        return jnp.einsum('jgcp,gh->jgchp', b4, eye).reshape(N_CG, CH_PER_CG, ST_PER_CG)

    def out_mat(c):
        c4 = c.reshape(N_CG, G_PER_CG, S5_GROUP, P_A)
        return jnp.einsum('jgcp,gh->jgphc', c4, eye).reshape(N_CG, ST_PER_CG, CH_PER_CG)

    bbd = jnp.concatenate([in_mat(bbr), in_mat(bbi)], axis=2).astype(BF16)
    cbd = jnp.concatenate([out_mat(c_re), out_mat(cneg)], axis=1).astype(BF16)
    lam = jnp.stack([lbr.reshape(-1), lbi.reshape(-1)])
    return bbd, cbd, lam


def _even_prompt_kernel(x_ref, win_ref, bbd_ref, cbd_ref, lam_ref, d_ref, gluw_ref, glub_ref,
                        sg_ref, sb_ref, sw_ref, sbt_ref, wout_ref, lng_ref, lnb_ref, *rest, n_cast):
    cast_in, rest = rest[:n_cast], rest[n_cast:]
    o_ref, sre_ref, sim_ref = rest[:3]
    cast_out, rest = rest[3:3 + n_cast], rest[3 + n_cast:]
    st_ref, xt_ref, scr_ref, cat_ref, vn_ref, gt_ref = rest

    @pl.when(pl.program_id(0) == 0)
    def _():
        st_ref[...] = jnp.zeros_like(st_ref)

    _cast_chunks(cast_in, cast_out)

    for c in range(D_MODEL // LANES):
        for b in range(BATCH):
            xt_ref.at[c][pl.ds(b, CHUNK, stride=BATCH), :] = x_ref[b, :, LANES * c:LANES * (c + 1)]
    xb = jnp.concatenate([xt_ref[c] for c in range(D_MODEL // LANES)], axis=-1)
    xbf = xb.astype(BF16)

    ua = _dot(xbf, win_ref[:, 0:D_A])
    uab = ua.astype(BF16)
    ys = []
    for j in range(N_CG):
        re_cols = slice(2 * ST_PER_CG * j, 2 * ST_PER_CG * j + ST_PER_CG)
        im_cols = slice(2 * ST_PER_CG * j + ST_PER_CG, 2 * ST_PER_CG * (j + 1))
        scr_ref[...] = _dot(uab[:, CH_PER_CG * j:CH_PER_CG * (j + 1)], bbd_ref[j])
        lr = jnp.broadcast_to(lam_ref[0:1, ST_PER_CG * j:ST_PER_CG * (j + 1)], (BATCH, ST_PER_CG))
        li = jnp.broadcast_to(lam_ref[1:2, ST_PER_CG * j:ST_PER_CG * (j + 1)], (BATCH, ST_PER_CG))

        def step(t, carry, lr=lr, li=li):
            sr, si = carry
            row = pl.multiple_of(t * BATCH, BATCH)
            nr = lr * sr - li * si + scr_ref[pl.ds(row, BATCH), 0:ST_PER_CG]
            ni = lr * si + li * sr + scr_ref[pl.ds(row, BATCH), ST_PER_CG:2 * ST_PER_CG]
            scr_ref[pl.ds(row, BATCH), 0:ST_PER_CG] = nr
            scr_ref[pl.ds(row, BATCH), ST_PER_CG:2 * ST_PER_CG] = ni
            return nr, ni

        sr, si = lax.fori_loop(0, CHUNK, step, (st_ref[:, re_cols], st_ref[:, im_cols]), unroll=4)
        st_ref[:, re_cols] = sr
        st_ref[:, im_cols] = si
        sre_ref[:, ST_PER_CG * j:ST_PER_CG * (j + 1)] = sr
        sim_ref[:, ST_PER_CG * j:ST_PER_CG * (j + 1)] = si
        ys.append(_dot(scr_ref[...].astype(BF16), cbd_ref[j]))
    y = jnp.concatenate(ys, axis=-1) + d_ref[...] * ua
    g = jax.nn.gelu(y, approximate=True)
    ya = g * jax.nn.sigmoid(_dot(g.astype(BF16), gluw_ref[...]) + glub_ref[...])
    cat_ref[:, 0:D_A] = ya.astype(BF16)

    ub = _dot(xbf, win_ref[:, D_A:D_A + D_B])
    vb = _dot(xbf, win_ref[:, D_A + D_B:D_A + 2 * D_B])
    vn = _ln(vb, sg_ref[...], sb_ref[...])
    for h in range(N_HB):
        vn_ref[h] = vn[:, HD_B * h:HD_B * (h + 1)]
    r_id = lax.broadcasted_iota(jnp.int32, (CHUNK, CHUNK), 0)
    c_id = lax.broadcasted_iota(jnp.int32, (CHUNK, CHUNK), 1)
    tril = (c_id <= r_id).astype(F32)
    wm = [(sw_ref[h] * tril).astype(BF16) for h in range(N_HB)]
    for b in range(0, BATCH, 2):
        for h in range(N_HB):
            v2 = jnp.concatenate([vn_ref.at[h][pl.ds(b, CHUNK, stride=BATCH), :],
                                  vn_ref.at[h][pl.ds(b + 1, CHUNK, stride=BATCH), :]], axis=-1)
            g2 = _dot(wm[h], v2.astype(BF16)) + sbt_ref[:, h:h + 1]
            gt_ref.at[h][pl.ds(b, CHUNK, stride=BATCH), :] = g2[:, 0:HD_B]
            gt_ref.at[h][pl.ds(b + 1, CHUNK, stride=BATCH), :] = g2[:, HD_B:]
    gate = jnp.concatenate([gt_ref[h] for h in range(N_HB)], axis=-1)
    cat_ref[:, D_A:] = (ub * gate).astype(BF16)

    out = _dot(cat_ref[...], wout_ref[...])
    o_ref[...] = _ln(ALPHA * xb + out, lng_ref[...], lnb_ref[...])


def _even_prompt(x, e, l, win, bbd, cbd, lam, d3, gluw, glub3, sg3, sb3, sw, sbt, wout, lng3, lnb3, *, cast_jobs=()):
    rows = ROWS_PER_CHUNK
    n = SEQ // CHUNK
    c_in, c_out, c_shapes = _cast_specs(cast_jobs, n)
    in_specs = [pl.BlockSpec((BATCH, CHUNK, D_MODEL), lambda i: (0, i, 0)),
                _layer(win, e), _whole(bbd), _whole(cbd), _whole(lam), _layer(d3, e), _layer(gluw, e),
                _layer(glub3, e), _layer(sg3, e), _layer(sb3, e), _layer(sw, e), _layer(sbt, e),
                _layer(wout, e), _layer(lng3, l), _layer(lnb3, l)] + c_in
    st = jax.ShapeDtypeStruct((BATCH, N_GA * P_A), F32)
    st_spec = pl.BlockSpec((BATCH, N_GA * P_A), lambda i: (0, 0))
    outs = pl.pallas_call(
        functools.partial(_even_prompt_kernel, n_cast=len(cast_jobs)),
        grid=(n,),
        in_specs=in_specs,
        out_specs=(pl.BlockSpec((rows, D_MODEL), lambda i: (i, 0)), st_spec, st_spec, *c_out),
        out_shape=(jax.ShapeDtypeStruct((M_PROMPT, D_MODEL), F32), st, st, *c_shapes),
        scratch_shapes=[pltpu.VMEM((BATCH, 2 * N_GA * P_A), F32),
                        pltpu.VMEM((D_MODEL // LANES, rows, LANES), F32),
                        pltpu.VMEM((rows, 2 * ST_PER_CG), F32),
                        pltpu.VMEM((rows, D_MODEL), BF16),
                        pltpu.VMEM((N_HB, rows, HD_B), F32),
                        pltpu.VMEM((N_HB, rows, HD_B), F32)],
        compiler_params=_params(),
        name="even_prompt",
    )(x, win, bbd, cbd, lam, d3, gluw, glub3, sg3, sb3, sw, sbt, wout, lng3, lnb3, *[w for w, _ in cast_jobs])
    return outs[:3], outs[3:]


def _even_sample_kernel(x_ref, s0r_ref, s0i_ref, win_ref, bbd_ref, cbd_ref, lam_ref, d_ref,
                        gluw_ref, glub_ref, sg_ref, sb_ref, sw0_ref, sb0_ref, wout_ref, lng_ref, lnb_ref,
                        o_ref, sre_ref, sim_ref, vn_ref):
    xb = x_ref[...]
    xbf = xb.astype(BF16)
    ua = _dot(xbf, win_ref[:, 0:D_A])
    uab = ua.astype(BF16)
    ys = []
    for j in range(N_CG):
        cols = slice(ST_PER_CG * j, ST_PER_CG * (j + 1))
        bu = _dot(uab[:, CH_PER_CG * j:CH_PER_CG * (j + 1)], bbd_ref[j])
        lr = lam_ref[0:1, cols]
        li = lam_ref[1:2, cols]
        s0r = s0r_ref[:, cols]
        s0i = s0i_ref[:, cols]
        sr = lr * s0r - li * s0i + bu[:, 0:ST_PER_CG]
        si = lr * s0i + li * s0r + bu[:, ST_PER_CG:]
        sre_ref[:, cols] = sr
        sim_ref[:, cols] = si
        ys.append(_dot(jnp.concatenate([sr, si], axis=-1).astype(BF16), cbd_ref[j]))
    y = jnp.concatenate(ys, axis=-1) + d_ref[...] * ua
    g = jax.nn.gelu(y, approximate=True)
    ya = g * jax.nn.sigmoid(_dot(g.astype(BF16), gluw_ref[...]) + glub_ref[...])

    ub = _dot(xbf, win_ref[:, D_A:D_A + D_B])
    vb = _dot(xbf, win_ref[:, D_A + D_B:D_A + 2 * D_B])
    vn = _ln(vb, sg_ref[...], sb_ref[...])
    vn_ref[...] = vn
    yb = ub * (sw0_ref[...] * vn + sb0_ref[...])

    cat = jnp.concatenate([ya, yb], axis=-1).astype(BF16)
    out = _dot(cat, wout_ref[...])
    o_ref[...] = _ln(ALPHA * xb + out, lng_ref[...], lnb_ref[...])


def _even_sample(x, s0r, s0i, e, l, win, bbd, cbd, lam, d3, gluw, glub3, sg3, sb3, sw0, sb0, wout, lng3, lnb3):
    in_specs = [_whole(x), _whole(s0r), _whole(s0i),
                _layer(win, e), _whole(bbd), _whole(cbd), _whole(lam), _layer(d3, e), _layer(gluw, e),
                _layer(glub3, e), _layer(sg3, e), _layer(sb3, e), _whole(sw0), _whole(sb0),
                _layer(wout, e), _layer(lng3, l), _layer(lnb3, l)]
    shapes = ((DEC_BATCH, D_MODEL), (DEC_BATCH, N_GA * P_A), (DEC_BATCH, N_GA * P_A), (DEC_BATCH, D_B))
    return pl.pallas_call(
        _even_sample_kernel,
        grid=(1,),
        in_specs=in_specs,
        out_specs=tuple(pl.BlockSpec(s, lambda i: (0, 0)) for s in shapes),
        out_shape=tuple(jax.ShapeDtypeStruct(s, F32) for s in shapes),
        compiler_params=_params(),
        name="even_sample",
    )(x, s0r, s0i, win, bbd, cbd, lam, d3, gluw, glub3, sg3, sb3, sw0, sb0, wout, lng3, lnb3)


def _store_batch_major(y, o_ref, yt_ref, tm):
    for c in range(D_MODEL // LANES):
        yt_ref[c] = y[:, LANES * c:LANES * (c + 1)]
    for c in range(D_MODEL // LANES):
        for b in range(BATCH):
            o_ref[b, :, LANES * c:LANES * (c + 1)] = yt_ref.at[c][pl.ds(b, tm // BATCH, stride=BATCH), :]


def _ffn_core(xb, gs_ref, rows, shift, wg_ref, wu_ref, cw_ref, cb_ref, wd_ref):
    hs = (K_F - 1) * shift
    xbf = xb.astype(BF16)
    acc = None
    for c0, c1 in FF_CHUNKS:
        gs_ref[hs:hs + rows, c0:c1] = _dot(xbf, wg_ref[:, c0:c1])
        conv = cb_ref[:, c0:c1]
        for k in range(K_F):
            conv = conv + cw_ref[k:k + 1, c0:c1] * gs_ref[k * shift:k * shift + rows, c0:c1]
        up = _dot(xbf, wu_ref[:, c0:c1])
        h = (jax.nn.silu(conv) * up).astype(BF16)
        part = _dot(h, wd_ref[c0:c1, :])
        acc = part if acc is None else acc + part
    return acc


def _ffn_kernel(x_ref, xs_ref, halo_s_ref, wg_ref, wu_ref, cw_ref, cb_ref, wd_ref, lng_ref, lnb_ref, *rest,
                tm, n, n_cast, batch_major_out):
    cast_in, rest = rest[:n_cast], rest[n_cast:]
    o_ref, cache_ref, os_ref, cache_s_ref = rest[:4]
    cast_out, rest = rest[4:4 + n_cast], rest[4 + n_cast:]
    gs_ref, xp_ref, ap_ref = rest[:3]
    i = pl.program_id(0)
    hs = (K_F - 1) * BATCH
    hs_s = (K_F - 1) * DEC_BATCH
    weights = (wg_ref, wu_ref, cw_ref, cb_ref, wd_ref)

    @pl.when(i == 0)
    def _():
        gs_ref[0:hs, :] = jnp.zeros((hs, D_FF), F32)
        xp_ref[...] = jnp.zeros_like(xp_ref)
        ap_ref[...] = jnp.zeros_like(ap_ref)

    def finish_previous_tile():
        y = _ln(ALPHA * xp_ref[...] + ap_ref[...], lng_ref[...], lnb_ref[...])
        if batch_major_out:
            _store_batch_major(y, o_ref, rest[3], tm)
        else:
            o_ref[...] = y

    @pl.when(i < n)
    def _():
        _cast_chunks(cast_in, cast_out)
        finish_previous_tile()
        xb = x_ref[...]
        acc = _ffn_core(xb, gs_ref, tm, BATCH, *weights)
        xp_ref[...] = xb
        ap_ref[...] = acc
        tail = gs_ref[tm:tm + hs, :]
        cache_ref[...] = tail
        gs_ref[0:hs, :] = tail

    @pl.when(i == n)
    def _():
        finish_previous_tile()
        gs_ref[0:hs_s, :] = halo_s_ref[...]
        xs = xs_ref[...]
        acc = _ffn_core(xs, gs_ref, DEC_BATCH, DEC_BATCH, *weights)
        os_ref[...] = _ln(ALPHA * xs + acc, lng_ref[...], lnb_ref[...])
        cache_s_ref[...] = gs_ref[DEC_BATCH:DEC_BATCH + hs_s, :]


def _ffn(x, xs, halo_s, l, wg, wu, cw, cb3, wd, lng3, lnb3, *, tm, batch_major_out=False, cast_jobs=()):
    m = x.shape[0]
    n = m // tm
    hs = (K_F - 1) * BATCH
    hs_s = (K_F - 1) * DEC_BATCH
    c_in, c_out, c_shapes = _cast_specs(cast_jobs, n)
    in_specs = [pl.BlockSpec((tm, D_MODEL), lambda i: (jnp.minimum(i, n - 1), 0)), _whole(xs), _whole(halo_s),
                _whole(wg), _whole(wu), _layer(cw, l), _layer(cb3, l), _whole(wd),
                _layer(lng3, l), _layer(lnb3, l)] + c_in
    scratch = [pltpu.VMEM((max(hs + tm, hs_s + DEC_BATCH), D_FF), F32),
               pltpu.VMEM((tm, D_MODEL), F32), pltpu.VMEM((tm, D_MODEL), F32)]
    if batch_major_out:
        o_spec = pl.BlockSpec((BATCH, tm // BATCH, D_MODEL), lambda i: (0, jnp.maximum(i - 1, 0), 0))
        o_shape = jax.ShapeDtypeStruct((BATCH, m // BATCH, D_MODEL), F32)
        scratch.append(pltpu.VMEM((D_MODEL // LANES, tm, LANES), F32))
    else:
        o_spec = pl.BlockSpec((tm, D_MODEL), lambda i: (jnp.maximum(i - 1, 0), 0))
        o_shape = jax.ShapeDtypeStruct((m, D_MODEL), F32)
    outs = pl.pallas_call(
        functools.partial(_ffn_kernel, tm=tm, n=n, n_cast=len(cast_jobs), batch_major_out=batch_major_out),
        grid=(n + 1,),
        in_specs=in_specs,
        out_specs=(o_spec, pl.BlockSpec((hs, D_FF), lambda i: (0, 0)),
                   pl.BlockSpec((DEC_BATCH, D_MODEL), lambda i: (0, 0)),
                   pl.BlockSpec((hs_s, D_FF), lambda i: (0, 0)), *c_out),
        out_shape=(o_shape, jax.ShapeDtypeStruct((hs, D_FF), F32),
                   jax.ShapeDtypeStruct((DEC_BATCH, D_MODEL), F32),
                   jax.ShapeDtypeStruct((hs_s, D_FF), F32), *c_shapes),
        scratch_shapes=scratch,
        compiler_params=_params(),
        name="conv_ffn",
    )(x, xs, halo_s, wg, wu, cw, cb3, wd, lng3, lnb3, *[w for w, _ in cast_jobs])
    return outs[:4], outs[4:]


def _odd_prompt_kernel(x_ref, win_ref, cw_ref, cb_ref, lcg_ref, lcb_ref, wout_ref, lng_ref, lnb_ref,
                       o_ref, cache_ref, gs_ref, hc_ref, *, tm):
    hs = (K_C - 1) * BATCH

    @pl.when(pl.program_id(0) == 0)
    def _():
        gs_ref[0:hs, :] = jnp.zeros((hs, D_C), F32)

    xb = x_ref[...]
    xbf = xb.astype(BF16)
    z1 = _dot(xbf, win_ref[:, 0:D_C])
    z2 = _dot(xbf, win_ref[:, D_C:2 * D_C])
    gs_ref[hs:hs + tm, :] = z1 * jax.nn.sigmoid(z2)

    def conv_rows(i, _):
        r0 = pl.multiple_of(i * CONV_ROWS, CONV_ROWS)
        for l0 in range(0, D_C, LANES):
            cols = slice(l0, l0 + LANES)
            win = gs_ref[pl.ds(r0, CONV_ROWS + hs), cols]
            acc = jnp.broadcast_to(cb_ref[:, cols], (CONV_ROWS, LANES))
            for k in range(K_C):
                acc = acc + cw_ref[k:k + 1, cols] * win[k * BATCH:k * BATCH + CONV_ROWS, :]
            hc_ref[pl.ds(r0, CONV_ROWS), cols] = acc
        return 0

    lax.fori_loop(0, tm // CONV_ROWS, conv_rows, 0)
    h = jax.nn.silu(_ln(hc_ref[...], lcg_ref[...], lcb_ref[...]))
    out = _dot(h.astype(BF16), wout_ref[...])
    o_ref[...] = _ln(ALPHA * xb + out, lng_ref[...], lnb_ref[...])
    tail = gs_ref[tm:tm + hs, :]
    cache_ref[...] = tail
    gs_ref[0:hs, :] = tail


def _odd_prompt(x, o, l, win, cw, cb3, lcg3, lcb3, wout, lng3, lnb3, *, tm):
    hs = (K_C - 1) * BATCH
    in_specs = [pl.BlockSpec((tm, D_MODEL), lambda i: (i, 0)),
                _whole(win), _layer(cw, o), _layer(cb3, o), _layer(lcg3, o), _layer(lcb3, o),
                _whole(wout), _layer(lng3, l), _layer(lnb3, l)]
    return pl.pallas_call(
        functools.partial(_odd_prompt_kernel, tm=tm),
        grid=(M_PROMPT // tm,),
        in_specs=in_specs,
        out_specs=(pl.BlockSpec((tm, D_MODEL), lambda i: (i, 0)), pl.BlockSpec((hs, D_C), lambda i: (0, 0))),
        out_shape=(jax.ShapeDtypeStruct((M_PROMPT, D_MODEL), F32), jax.ShapeDtypeStruct((hs, D_C), F32)),
        scratch_shapes=[pltpu.VMEM((hs + tm, D_C), F32), pltpu.VMEM((tm, D_C), F32)],
        compiler_params=_params(),
        name="odd_prompt",
    )(x, win, cw, cb3, lcg3, lcb3, wout, lng3, lnb3)


def _odd_sample_kernel(x_ref, c_ref, cn_ref, win_ref, cw_ref, cb_ref, lcg_ref, lcb_ref, wout_ref, lng_ref, lnb_ref,
                       o_ref, oc_ref, g_ref, pc_ref):
    k = pl.program_id(0)
    last = (K_C - 1) // SAMPLE_TAPS - 1

    @pl.when(k == 0)
    def _():
        xbf = x_ref[...].astype(BF16)
        z1 = _dot(xbf, win_ref[:, 0:D_C])
        z2 = _dot(xbf, win_ref[:, D_C:2 * D_C])
        g_ref[...] = z1 * jax.nn.sigmoid(z2)
        pc_ref[...] = jnp.broadcast_to(cb_ref[...], pc_ref.shape)

    pc = pc_ref[...]
    for t in range(SAMPLE_TAPS):
        pc = pc + cw_ref[pl.ds(k * SAMPLE_TAPS + t, 1), :] * c_ref[t]
    pc_ref[...] = pc
    oc_ref[0:SAMPLE_TAPS - 1] = c_ref[1:SAMPLE_TAPS]

    @pl.when(k < last)
    def _():
        oc_ref[SAMPLE_TAPS - 1] = cn_ref[...]

    @pl.when(k == last)
    def _():
        g = g_ref[...]
        oc_ref[SAMPLE_TAPS - 1] = g
        hc = pc + cw_ref[K_C - 1:K_C, :] * g
        h = jax.nn.silu(_ln(hc, lcg_ref[...], lcb_ref[...]))
        out = _dot(h.astype(BF16), wout_ref[...])
        o_ref[...] = _ln(ALPHA * x_ref[...] + out, lng_ref[...], lnb_ref[...])


def _odd_sample(x, cache_t, o, l, win, cw, cb3, lcg3, lcb3, wout, lng3, lnb3):
    nk = K_C - 1
    steps = nk // SAMPLE_TAPS
    in_specs = [_whole(x),
                pl.BlockSpec((None, SAMPLE_TAPS, DEC_BATCH, D_C), lambda k: (o, k, 0, 0)),
                pl.BlockSpec((None, None, DEC_BATCH, D_C),
                             lambda k: (o, jnp.minimum((k + 1) * SAMPLE_TAPS, nk - 1), 0, 0)),
                _whole(win), _layer(cw, o), _layer(cb3, o), _layer(lcg3, o), _layer(lcb3, o),
                _whole(wout), _layer(lng3, l), _layer(lnb3, l)]
    return pl.pallas_call(
        _odd_sample_kernel,
        grid=(steps,),
        in_specs=in_specs,
        out_specs=(pl.BlockSpec((DEC_BATCH, D_MODEL), lambda k: (0, 0)),
                   pl.BlockSpec((SAMPLE_TAPS, DEC_BATCH, D_C), lambda k: (k, 0, 0))),
        out_shape=(jax.ShapeDtypeStruct((DEC_BATCH, D_MODEL), F32),
                   jax.ShapeDtypeStruct((nk, DEC_BATCH, D_C), F32)),
        scratch_shapes=[pltpu.VMEM((DEC_BATCH, D_C), F32), pltpu.VMEM((DEC_BATCH, D_C), F32)],
        compiler_params=_params(),
        name="odd_sample",
    )(x, cache_t, cache_t, win, cw, cb3, lcg3, lcb3, wout, lng3, lnb3)


def kernel(x_prompt, x_sample, state_a_re, state_a_im, cache_c_conv, cache_ffn_conv, w_in_ab, s5_lam_re, s5_lam_im, s5_log_dt, s5_b_re, s5_b_im, s5_c_re, s5_c_im, s5_d, s5_glu_w, s5_glu_b, sgu_ln_g, sgu_ln_b, sgu_w, sgu_b, w_out_ab, w_in_c, conv_c_w, conv_c_b, ln_c_g, ln_c_b, w_out_c, ffn_w_gate, ffn_w_up, ffn_conv_w, ffn_conv_b, ffn_w_down, ln_mix_g, ln_mix_b, ln_ffn_g, ln_ffn_b):
    xs = x_sample.reshape(DEC_BATCH, D_MODEL)
    xp = x_prompt

    assert N_EVEN == 1, "side-stream casts below are laid out for one even layer followed by odd layers"
    win_ab, wout_ab, gluw = w_in_ab.astype(BF16), w_out_ab.astype(BF16), s5_glu_w.astype(BF16)
    cache_c_t = jnp.transpose(cache_c_conv, (0, 2, 1, 3))
    fcb3 = _rows3(ffn_conv_b)
    d3, glub3, sg3, sb3 = _rows3(s5_d), _rows3(s5_glu_b), _rows3(sgu_ln_g), _rows3(sgu_ln_b)
    sbt = jnp.swapaxes(sgu_b, 1, 2)
    ccb3, lcg3, lcb3 = _rows3(conv_c_b), _rows3(ln_c_g), _rows3(ln_c_b)
    lmg3, lmb3, lfg3, lfb3 = _rows3(ln_mix_g), _rows3(ln_mix_b), _rows3(ln_ffn_g), _rows3(ln_ffn_b)

    def ffn_jobs(l):
        return [(ffn_w_gate, l), (ffn_w_up, l), (ffn_w_down, l)]

    sa_re_p, sa_im_p, sa_re_s, sa_im_s, sb_v_s = [], [], [], [], []
    cc_p, cc_s, cf_p, cf_s = [], [], [], []
    for l in range(DEPTH):
        if l % 2 == 0:
            e = l // 2
            lbr, lbi, bbr, bbi, cneg = _s5_prep(
                s5_lam_re[e], s5_lam_im[e], s5_log_dt[e],
                jnp.swapaxes(s5_b_re[e], 1, 2), jnp.swapaxes(s5_b_im[e], 1, 2), s5_c_im[e])
            bbd, cbd, lam = _block_diag_params(lbr, lbi, bbr, bbi, s5_c_re[e], cneg)
            shared = (win_ab, bbd, cbd, lam, d3, gluw, glub3, sg3, sb3)
            (xp, sre, sim), ffn_w = _even_prompt(xp, e, l, *shared, sgu_w, sbt, wout_ab, lmg3, lmb3,
                                                 cast_jobs=ffn_jobs(l))
            sw0 = jnp.repeat(sgu_w[e][:, 0, 0], HD_B).reshape(1, D_B)
            sb0 = jnp.repeat(sgu_b[e][:, 0], HD_B).reshape(1, D_B)
            xs, sres, sims, vn = _even_sample(
                xs, state_a_re[e].reshape(DEC_BATCH, -1), state_a_im[e].reshape(DEC_BATCH, -1),
                e, l, *shared, sw0, sb0, wout_ab, lmg3, lmb3)
            sa_re_p.append(sre.reshape(BATCH, N_GA, P_A))
            sa_im_p.append(sim.reshape(BATCH, N_GA, P_A))
            sa_re_s.append(sres.reshape(DEC_BATCH, N_GA, P_A))
            sa_im_s.append(sims.reshape(DEC_BATCH, N_GA, P_A))
            sb_v_s.append(vn.reshape(DEC_BATCH, 1, D_B))
        else:
            o = l // 2
            win_c, wout_c = odd_w
            rest = (win_c, conv_c_w, ccb3, lcg3, lcb3, wout_c, lmg3, lmb3)
            xp, cp = _odd_prompt(xp, o, l, *rest, tm=ODD_TM)
            xs, cs_t = _odd_sample(xs, cache_c_t, o, l, *rest)
            cc_p.append(jnp.transpose(cp.reshape(K_C - 1, BATCH, D_C), (1, 0, 2)))
            cc_s.append(jnp.transpose(cs_t, (1, 0, 2)))
        wg, wu, wd = ffn_w
        jobs = [(w_in_c, (l + 1) // 2), (w_out_c, (l + 1) // 2)] + ffn_jobs(l + 1) if l + 1 < DEPTH else []
        halo_s = jnp.transpose(cache_ffn_conv[l], (1, 0, 2)).reshape((K_F - 1) * DEC_BATCH, D_FF)
        (xp, fcp, xs, fcs), casts = _ffn(xp, xs, halo_s, l, wg, wu, ffn_conv_w, fcb3, wd, lfg3, lfb3, tm=FFN_TM,
                                         batch_major_out=(l == DEPTH - 1), cast_jobs=jobs)
        if jobs:
            odd_w, ffn_w = casts[:2], casts[2:]
        cf_p.append(jnp.transpose(fcp.reshape(K_F - 1, BATCH, D_FF), (1, 0, 2)))
        cf_s.append(jnp.transpose(fcs.reshape(K_F - 1, DEC_BATCH, D_FF), (1, 0, 2)))

    ys = xs.reshape(DEC_BATCH, 1, D_MODEL)
    return (xp, ys,
            jnp.stack(sa_re_p), jnp.stack(sa_im_p), jnp.stack(sa_re_s), jnp.stack(sa_im_s),
            jnp.stack(sb_v_s),
            jnp.stack(cc_p), jnp.stack(cc_s),
            jnp.stack(cf_p), jnp.stack(cf_s))
```

```python
import functools

import jax
import jax.numpy as jnp
from jax import lax
from jax.experimental import pallas as pl
from jax.experimental.pallas import tpu as pltpu

D_MODEL = 1024
BATCH = 8
SEQ = 2048
DEPTH = 2
DEC_BATCH = 128
N_EVEN = (DEPTH + 1) // 2
N_ODD = DEPTH // 2
D_A = D_MODEL // 2
S5_GROUP = 16
N_GA = D_A // S5_GROUP
P_A = 64
D_B = D_MODEL // 2
N_HB = 4
HD_B = D_B // N_HB
CHUNK = 128
D_C = D_MODEL
K_C = 31
D_FF = ((8 * D_MODEL) // 3 + 127) // 128 * 128
K_F = 3
ALPHA = (2.0 * DEPTH) ** 0.25
LN_EPS = 1e-5

F32 = jnp.float32
BF16 = jnp.bfloat16

LANES = 128
MXU_N = 256
M_PROMPT = BATCH * SEQ
ROWS_PER_CHUNK = CHUNK * BATCH
N_CG = 4
G_PER_CG = N_GA // N_CG
CH_PER_CG = G_PER_CG * S5_GROUP
ST_PER_CG = G_PER_CG * P_A
FF_CHUNKS = ((0, 6 * MXU_N), (6 * MXU_N, D_FF))
FFN_TM = 512
ODD_TM = 1024
CONV_ROWS = 8 * BATCH
SAMPLE_TAPS = 5
N_CAST_CHUNKS = 16
VMEM_LIMIT = 56 * 1024 * 1024


def _ln(x, g, b):
    mu = jnp.mean(x, axis=-1, keepdims=True)
    xc = x - mu
    var = jnp.mean(xc * xc, axis=-1, keepdims=True)
    return xc * lax.rsqrt(var + LN_EPS) * g + b


def _dot(a, b):
    return jnp.dot(a, b, preferred_element_type=F32)


def _whole(arr):
    nd = arr.ndim
    return pl.BlockSpec(arr.shape, lambda *_: (0,) * nd, pipeline_mode=pl.Buffered(1))


def _layer(arr, l):
    nd = arr.ndim
    return pl.BlockSpec((None,) + arr.shape[1:], lambda *_: (l,) + (0,) * (nd - 1),
                        pipeline_mode=pl.Buffered(1))


def _rows3(v):
    return v.reshape(v.shape[0], 1, v.shape[1])


def _params(n_grid=1):
    return pltpu.CompilerParams(dimension_semantics=("arbitrary",) * n_grid,
                                vmem_limit_bytes=VMEM_LIMIT)


def _cast_specs(jobs, n_steps):
    def chunk(i):
        return jnp.minimum(i * N_CAST_CHUNKS // n_steps, N_CAST_CHUNKS - 1)

    in_specs, out_specs, out_shapes = [], [], []
    for w, l in jobs:
        _, r, c = w.shape
        rows = r // N_CAST_CHUNKS
        in_specs.append(pl.BlockSpec((None, rows, c), lambda i, l=l: (l, chunk(i), 0)))
        out_specs.append(pl.BlockSpec((rows, c), lambda i: (chunk(i), 0)))
        out_shapes.append(jax.ShapeDtypeStruct((r, c), BF16))
    return in_specs, out_specs, out_shapes


def _cast_chunks(in_refs, out_refs):
    for w_ref, o_ref in zip(in_refs, out_refs):
        o_ref[...] = w_ref[...].astype(BF16)


def _s5_prep_kernel(lr_ref, li_ref, ldt_ref, br_ref, bi_ref, ci_ref,
                    lbr_ref, lbi_ref, bbr_ref, bbi_ref, cneg_ref):
    lr = lr_ref[...]
    li = li_ref[...]
    dt = jnp.exp(ldt_ref[...])
    mag = jnp.exp(lr * dt)
    lbr = mag * jnp.cos(li * dt)
    lbi = mag * jnp.sin(li * dt)
    lbr_ref[...] = lbr
    lbi_ref[...] = lbi
    nr = lbr - 1.0
    ni = lbi
    den = lr * lr + li * li
    qr = (nr * lr + ni * li) / den
    qi = (ni * lr - nr * li) / den
    br = br_ref[...]
    bi = bi_ref[...]
    qr3 = qr[:, None, :]
    qi3 = qi[:, None, :]
    bbr_ref[...] = qr3 * br - qi3 * bi
    bbi_ref[...] = qr3 * bi + qi3 * br
    cneg_ref[...] = -ci_ref[...]


def _s5_prep(lam_re, lam_im, log_dt, b_re, b_im, c_im):
    gp = jax.ShapeDtypeStruct((N_GA, P_A), F32)
    gcp = jax.ShapeDtypeStruct((N_GA, S5_GROUP, P_A), F32)
    return pl.pallas_call(
        _s5_prep_kernel,
        out_shape=(gp, gp, gcp, gcp, gcp),
        name="s5_prep",
    )(lam_re, lam_im, log_dt.reshape(N_GA, 1), b_re, b_im, c_im)


def _block_diag_kernel(bre_ref, bim_ref, cre_ref, cim_ref, bbd_ref, cbd_ref):
    def iota(shape, axis):
        return lax.broadcasted_iota(jnp.int32, shape, axis)

    p_bits = P_A.bit_length() - 1
    c_bits = S5_GROUP.bit_length() - 1
    rep_cols = (jnp.bitwise_and(iota((P_A, ST_PER_CG), 1), P_A - 1) == iota((P_A, ST_PER_CG), 0)).astype(BF16)
    rep_rows = (jnp.bitwise_and(iota((ST_PER_CG, P_A), 0), P_A - 1) == iota((ST_PER_CG, P_A), 1)).astype(BF16)
    diag_b = (lax.shift_right_logical(iota((CH_PER_CG, ST_PER_CG), 0), c_bits)
              == lax.shift_right_logical(iota((CH_PER_CG, ST_PER_CG), 1), p_bits))
    diag_c = (lax.shift_right_logical(iota((ST_PER_CG, CH_PER_CG), 0), p_bits)
              == lax.shift_right_logical(iota((ST_PER_CG, CH_PER_CG), 1), c_bits))
    for j in range(N_CG):
        for part, src in enumerate((bre_ref, bim_ref)):
            x = _dot(src[j].astype(BF16), rep_cols)
            bbd_ref[j, :, ST_PER_CG * part:ST_PER_CG * (part + 1)] = jnp.where(diag_b, x, 0.0).astype(BF16)
        for part, src in enumerate((cre_ref, cim_ref)):
            y = lax.dot_general(rep_rows, src[j].astype(BF16), (((1,), (1,)), ((), ())),
                                preferred_element_type=F32)
            cbd_ref[j, ST_PER_CG * part:ST_PER_CG * (part + 1), :] = jnp.where(diag_c, y, 0.0).astype(BF16)


def _block_diag_params(lbr, lbi, bbr, bbi, c_re, cneg):
    slabs = [m.reshape(N_CG, CH_PER_CG, P_A) for m in (bbr, bbi, c_re, cneg)]
    bbd, cbd = pl.pallas_call(
        _block_diag_kernel,
        out_shape=(jax.ShapeDtypeStruct((N_CG, CH_PER_CG, 2 * ST_PER_CG), BF16),
                   jax.ShapeDtypeStruct((N_CG, 2 * ST_PER_CG, CH_PER_CG), BF16)),
        name="s5_block_diag",
    )(*slabs)
    lam = jnp.stack([lbr.reshape(-1), lbi.reshape(-1)])
    return bbd, cbd, lam


def _even_prompt_kernel(x_ref, win_ref, bbd_ref, cbd_ref, lam_ref, d_ref, gluw_ref, glub_ref,
                        sg_ref, sb_ref, sw_ref, sbt_ref, wout_ref, lng_ref, lnb_ref, *rest, n_cast):
    cast_in, rest = rest[:n_cast], rest[n_cast:]
    o_ref, sre_ref, sim_ref = rest[:3]
    cast_out, rest = rest[3:3 + n_cast], rest[3 + n_cast:]
    st_ref, xt_ref, scr_ref, cat_ref, vn_ref, gt_ref = rest

    @pl.when(pl.program_id(0) == 0)
    def _():
        st_ref[...] = jnp.zeros_like(st_ref)

    _cast_chunks(cast_in, cast_out)

    for c in range(D_MODEL // LANES):
        for b in range(BATCH):
            xt_ref.at[c][pl.ds(b, CHUNK, stride=BATCH), :] = x_ref[b, :, LANES * c:LANES * (c + 1)]
    xb = jnp.concatenate([xt_ref[c] for c in range(D_MODEL // LANES)], axis=-1)
    xbf = xb.astype(BF16)

    ua = _dot(xbf, win_ref[:, 0:D_A])
    uab = ua.astype(BF16)
    ys = []
    for j in range(N_CG):
        re_cols = slice(2 * ST_PER_CG * j, 2 * ST_PER_CG * j + ST_PER_CG)
        im_cols = slice(2 * ST_PER_CG * j + ST_PER_CG, 2 * ST_PER_CG * (j + 1))
        scr_ref[...] = _dot(uab[:, CH_PER_CG * j:CH_PER_CG * (j + 1)], bbd_ref[j])
        lr = jnp.broadcast_to(lam_ref[0:1, ST_PER_CG * j:ST_PER_CG * (j + 1)], (BATCH, ST_PER_CG))
        li = jnp.broadcast_to(lam_ref[1:2, ST_PER_CG * j:ST_PER_CG * (j + 1)], (BATCH, ST_PER_CG))

        def step(t, carry, lr=lr, li=li):
            sr, si = carry
            row = pl.multiple_of(t * BATCH, BATCH)
            nr = lr * sr - li * si + scr_ref[pl.ds(row, BATCH), 0:ST_PER_CG]
            ni = lr * si + li * sr + scr_ref[pl.ds(row, BATCH), ST_PER_CG:2 * ST_PER_CG]
            scr_ref[pl.ds(row, BATCH), 0:ST_PER_CG] = nr
            scr_ref[pl.ds(row, BATCH), ST_PER_CG:2 * ST_PER_CG] = ni
            return nr, ni

        sr, si = lax.fori_loop(0, CHUNK, step, (st_ref[:, re_cols], st_ref[:, im_cols]), unroll=4)
        st_ref[:, re_cols] = sr
        st_ref[:, im_cols] = si
        sre_ref[:, ST_PER_CG * j:ST_PER_CG * (j + 1)] = sr
        sim_ref[:, ST_PER_CG * j:ST_PER_CG * (j + 1)] = si
        ys.append(_dot(scr_ref[...].astype(BF16), cbd_ref[j]))
    y = jnp.concatenate(ys, axis=-1) + d_ref[...] * ua
    g = jax.nn.gelu(y, approximate=True)
    ya = g * jax.nn.sigmoid(_dot(g.astype(BF16), gluw_ref[...]) + glub_ref[...])
    cat_ref[:, 0:D_A] = ya.astype(BF16)

    ub = _dot(xbf, win_ref[:, D_A:D_A + D_B])
    vb = _dot(xbf, win_ref[:, D_A + D_B:D_A + 2 * D_B])
    vn = _ln(vb, sg_ref[...], sb_ref[...])
    for h in range(N_HB):
        vn_ref[h] = vn[:, HD_B * h:HD_B * (h + 1)]
    r_id = lax.broadcasted_iota(jnp.int32, (CHUNK, CHUNK), 0)
    c_id = lax.broadcasted_iota(jnp.int32, (CHUNK, CHUNK), 1)
    tril = (c_id <= r_id).astype(F32)
    wm = [(sw_ref[h] * tril).astype(BF16) for h in range(N_HB)]
    for b in range(0, BATCH, 2):
        for h in range(N_HB):
            v2 = jnp.concatenate([vn_ref.at[h][pl.ds(b, CHUNK, stride=BATCH), :],
                                  vn_ref.at[h][pl.ds(b + 1, CHUNK, stride=BATCH), :]], axis=-1)
            g2 = _dot(wm[h], v2.astype(BF16)) + sbt_ref[:, h:h + 1]
            gt_ref.at[h][pl.ds(b, CHUNK, stride=BATCH), :] = g2[:, 0:HD_B]
            gt_ref.at[h][pl.ds(b + 1, CHUNK, stride=BATCH), :] = g2[:, HD_B:]
    gate = jnp.concatenate([gt_ref[h] for h in range(N_HB)], axis=-1)
    cat_ref[:, D_A:] = (ub * gate).astype(BF16)

    out = _dot(cat_ref[...], wout_ref[...])
    o_ref[...] = _ln(ALPHA * xb + out, lng_ref[...], lnb_ref[...])


def _even_prompt(x, e, l, win, bbd, cbd, lam, d3, gluw, glub3, sg3, sb3, sw, sbt, wout, lng3, lnb3, *, cast_jobs=()):
    rows = ROWS_PER_CHUNK
    n = SEQ // CHUNK
    c_in, c_out, c_shapes = _cast_specs(cast_jobs, n)
    in_specs = [pl.BlockSpec((BATCH, CHUNK, D_MODEL), lambda i: (0, i, 0)),
                _layer(win, e), _whole(bbd), _whole(cbd), _whole(lam), _layer(d3, e), _layer(gluw, e),
                _layer(glub3, e), _layer(sg3, e), _layer(sb3, e), _layer(sw, e), _layer(sbt, e),
                _layer(wout, e), _layer(lng3, l), _layer(lnb3, l)] + c_in
    st = jax.ShapeDtypeStruct((BATCH, N_GA * P_A), F32)
    st_spec = pl.BlockSpec((BATCH, N_GA * P_A), lambda i: (0, 0))
    outs = pl.pallas_call(
        functools.partial(_even_prompt_kernel, n_cast=len(cast_jobs)),
        grid=(n,),
        in_specs=in_specs,
        out_specs=(pl.BlockSpec((rows, D_MODEL), lambda i: (i, 0)), st_spec, st_spec, *c_out),
        out_shape=(jax.ShapeDtypeStruct((M_PROMPT, D_MODEL), F32), st, st, *c_shapes),
        scratch_shapes=[pltpu.VMEM((BATCH, 2 * N_GA * P_A), F32),
                        pltpu.VMEM((D_MODEL // LANES, rows, LANES), F32),
                        pltpu.VMEM((rows, 2 * ST_PER_CG), F32),
                        pltpu.VMEM((rows, D_MODEL), BF16),
                        pltpu.VMEM((N_HB, rows, HD_B), F32),
                        pltpu.VMEM((N_HB, rows, HD_B), F32)],
        compiler_params=_params(),
        name="even_prompt",
    )(x, win, bbd, cbd, lam, d3, gluw, glub3, sg3, sb3, sw, sbt, wout, lng3, lnb3, *[w for w, _ in cast_jobs])
    return outs[:3], outs[3:]


def _even_sample_kernel(x_ref, s0r_ref, s0i_ref, win_ref, bbd_ref, cbd_ref, lam_ref, d_ref,
                        gluw_ref, glub_ref, sg_ref, sb_ref, sw0_ref, sb0_ref, wout_ref, lng_ref, lnb_ref,
                        o_ref, sre_ref, sim_ref, vn_ref):
    xb = x_ref[...]
    xbf = xb.astype(BF16)
    ua = _dot(xbf, win_ref[:, 0:D_A])
    uab = ua.astype(BF16)
    ys = []
    for j in range(N_CG):
        cols = slice(ST_PER_CG * j, ST_PER_CG * (j + 1))
        bu = _dot(uab[:, CH_PER_CG * j:CH_PER_CG * (j + 1)], bbd_ref[j])
        lr = lam_ref[0:1, cols]
        li = lam_ref[1:2, cols]
        s0r = s0r_ref[:, cols]
        s0i = s0i_ref[:, cols]
        sr = lr * s0r - li * s0i + bu[:, 0:ST_PER_CG]
        si = lr * s0i + li * s0r + bu[:, ST_PER_CG:]
        sre_ref[:, cols] = sr
        sim_ref[:, cols] = si
        ys.append(_dot(jnp.concatenate([sr, si], axis=-1).astype(BF16), cbd_ref[j]))
    y = jnp.concatenate(ys, axis=-1) + d_ref[...] * ua
    g = jax.nn.gelu(y, approximate=True)
    ya = g * jax.nn.sigmoid(_dot(g.astype(BF16), gluw_ref[...]) + glub_ref[...])

    ub = _dot(xbf, win_ref[:, D_A:D_A + D_B])
    vb = _dot(xbf, win_ref[:, D_A + D_B:D_A + 2 * D_B])
    vn = _ln(vb, sg_ref[...], sb_ref[...])
    vn_ref[...] = vn
    yb = ub * (sw0_ref[...] * vn + sb0_ref[...])

    cat = jnp.concatenate([ya, yb], axis=-1).astype(BF16)
    out = _dot(cat, wout_ref[...])
    o_ref[...] = _ln(ALPHA * xb + out, lng_ref[...], lnb_ref[...])


def _even_sample(x, s0r, s0i, e, l, win, bbd, cbd, lam, d3, gluw, glub3, sg3, sb3, sw0, sb0, wout, lng3, lnb3):
    in_specs = [_whole(x), _whole(s0r), _whole(s0i),
                _layer(win, e), _whole(bbd), _whole(cbd), _whole(lam), _layer(d3, e), _layer(gluw, e),
                _layer(glub3, e), _layer(sg3, e), _layer(sb3, e), _whole(sw0), _whole(sb0),
                _layer(wout, e), _layer(lng3, l), _layer(lnb3, l)]
    shapes = ((DEC_BATCH, D_MODEL), (DEC_BATCH, N_GA * P_A), (DEC_BATCH, N_GA * P_A), (DEC_BATCH, D_B))
    return pl.pallas_call(
        _even_sample_kernel,
        grid=(1,),
        in_specs=in_specs,
        out_specs=tuple(pl.BlockSpec(s, lambda i: (0, 0)) for s in shapes),
        out_shape=tuple(jax.ShapeDtypeStruct(s, F32) for s in shapes),
        compiler_params=_params(),
        name="even_sample",
    )(x, s0r, s0i, win, bbd, cbd, lam, d3, gluw, glub3, sg3, sb3, sw0, sb0, wout, lng3, lnb3)


def _store_batch_major(y, o_ref, yt_ref, tm):
    for c in range(D_MODEL // LANES):
        yt_ref[c] = y[:, LANES * c:LANES * (c + 1)]
    for c in range(D_MODEL // LANES):
        for b in range(BATCH):
            o_ref[b, :, LANES * c:LANES * (c + 1)] = yt_ref.at[c][pl.ds(b, tm // BATCH, stride=BATCH), :]


def _ffn_tile(xb, gs_ref, rows, shift, wg_ref, wu_ref, cw_ref, cb_ref, wd_ref, lng_ref, lnb_ref):
    hs = (K_F - 1) * shift
    xbf = xb.astype(BF16)
    acc = None
    for c0, c1 in FF_CHUNKS:
        gs_ref[hs:hs + rows, c0:c1] = _dot(xbf, wg_ref[:, c0:c1])
        conv = cb_ref[:, c0:c1]
        for k in range(K_F):
            conv = conv + cw_ref[k:k + 1, c0:c1] * gs_ref[k * shift:k * shift + rows, c0:c1]
        up = _dot(xbf, wu_ref[:, c0:c1])
        h = (jax.nn.silu(conv) * up).astype(BF16)
        part = _dot(h, wd_ref[c0:c1, :])
        acc = part if acc is None else acc + part
    return _ln(ALPHA * xb + acc, lng_ref[...], lnb_ref[...])


def _ffn_kernel(x_ref, xs_ref, halo_s_ref, wg_ref, wu_ref, cw_ref, cb_ref, wd_ref, lng_ref, lnb_ref, *rest,
                tm, n, n_cast, batch_major_out):
    cast_in, rest = rest[:n_cast], rest[n_cast:]
    o_ref, cache_ref, os_ref, cache_s_ref = rest[:4]
    cast_out, rest = rest[4:4 + n_cast], rest[4 + n_cast:]
    gs_ref = rest[0]
    i = pl.program_id(0)
    hs = (K_F - 1) * BATCH
    hs_s = (K_F - 1) * DEC_BATCH
    weights = (wg_ref, wu_ref, cw_ref, cb_ref, wd_ref, lng_ref, lnb_ref)

    @pl.when(i == 0)
    def _():
        gs_ref[0:hs, :] = jnp.zeros((hs, D_FF), F32)

    @pl.when(i < n)
    def _():
        _cast_chunks(cast_in, cast_out)
        y = _ffn_tile(x_ref[...], gs_ref, tm, BATCH, *weights)
        if batch_major_out:
            _store_batch_major(y, o_ref, rest[1], tm)
        else:
            o_ref[...] = y
        tail = gs_ref[tm:tm + hs, :]
        cache_ref[...] = tail
        gs_ref[0:hs, :] = tail

    @pl.when(i == n)
    def _():
        gs_ref[0:hs_s, :] = halo_s_ref[...]
        os_ref[...] = _ffn_tile(xs_ref[...], gs_ref, DEC_BATCH, DEC_BATCH, *weights)
        cache_s_ref[...] = gs_ref[DEC_BATCH:DEC_BATCH + hs_s, :]


def _ffn(x, xs, halo_s, l, wg, wu, cw, cb3, wd, lng3, lnb3, *, tm, batch_major_out=False, cast_jobs=()):
    m = x.shape[0]
    n = m // tm
    hs = (K_F - 1) * BATCH
    hs_s = (K_F - 1) * DEC_BATCH
    c_in, c_out, c_shapes = _cast_specs(cast_jobs, n)
    in_specs = [pl.BlockSpec((tm, D_MODEL), lambda i: (jnp.minimum(i, n - 1), 0)), _whole(xs), _whole(halo_s),
                _whole(wg), _whole(wu), _layer(cw, l), _layer(cb3, l), _whole(wd),
                _layer(lng3, l), _layer(lnb3, l)] + c_in
    scratch = [pltpu.VMEM((max(hs + tm, hs_s + DEC_BATCH), D_FF), F32)]
    if batch_major_out:
        o_spec = pl.BlockSpec((BATCH, tm // BATCH, D_MODEL), lambda i: (0, jnp.minimum(i, n - 1), 0))
        o_shape = jax.ShapeDtypeStruct((BATCH, m // BATCH, D_MODEL), F32)
        scratch.append(pltpu.VMEM((D_MODEL // LANES, tm, LANES), F32))
    else:
        o_spec = pl.BlockSpec((tm, D_MODEL), lambda i: (jnp.minimum(i, n - 1), 0))
        o_shape = jax.ShapeDtypeStruct((m, D_MODEL), F32)
    outs = pl.pallas_call(
        functools.partial(_ffn_kernel, tm=tm, n=n, n_cast=len(cast_jobs), batch_major_out=batch_major_out),
        grid=(n + 1,),
        in_specs=in_specs,
        out_specs=(o_spec, pl.BlockSpec((hs, D_FF), lambda i: (0, 0)),
                   pl.BlockSpec((DEC_BATCH, D_MODEL), lambda i: (0, 0)),
                   pl.BlockSpec((hs_s, D_FF), lambda i: (0, 0)), *c_out),
        out_shape=(o_shape, jax.ShapeDtypeStruct((hs, D_FF), F32),
                   jax.ShapeDtypeStruct((DEC_BATCH, D_MODEL), F32),
                   jax.ShapeDtypeStruct((hs_s, D_FF), F32), *c_shapes),
        scratch_shapes=scratch,
        compiler_params=_params(),
        name="conv_ffn",
    )(x, xs, halo_s, wg, wu, cw, cb3, wd, lng3, lnb3, *[w for w, _ in cast_jobs])
    return outs[:4], outs[4:]


def _odd_prompt_kernel(x_ref, win_ref, cw_ref, cb_ref, lcg_ref, lcb_ref, wout_ref, lng_ref, lnb_ref,
                       o_ref, cache_ref, gs_ref, hc_ref, *, tm):
    hs = (K_C - 1) * BATCH

    @pl.when(pl.program_id(0) == 0)
    def _():
        gs_ref[0:hs, :] = jnp.zeros((hs, D_C), F32)

    xb = x_ref[...]
    xbf = xb.astype(BF16)
    z1 = _dot(xbf, win_ref[:, 0:D_C])
    z2 = _dot(xbf, win_ref[:, D_C:2 * D_C])
    gs_ref[hs:hs + tm, :] = z1 * jax.nn.sigmoid(z2)

    def conv_rows(i, _):
        r0 = pl.multiple_of(i * CONV_ROWS, CONV_ROWS)
        for l0 in range(0, D_C, LANES):
            cols = slice(l0, l0 + LANES)
            win = gs_ref[pl.ds(r0, CONV_ROWS + hs), cols]
            acc = jnp.broadcast_to(cb_ref[:, cols], (CONV_ROWS, LANES))
            for k in range(K_C):
                acc = acc + cw_ref[k:k + 1, cols] * win[k * BATCH:k * BATCH + CONV_ROWS, :]
            hc_ref[pl.ds(r0, CONV_ROWS), cols] = acc
        return 0

    lax.fori_loop(0, tm // CONV_ROWS, conv_rows, 0)
    h = jax.nn.silu(_ln(hc_ref[...], lcg_ref[...], lcb_ref[...]))
    out = _dot(h.astype(BF16), wout_ref[...])
    o_ref[...] = _ln(ALPHA * xb + out, lng_ref[...], lnb_ref[...])
    tail = gs_ref[tm:tm + hs, :]
    cache_ref[...] = tail
    gs_ref[0:hs, :] = tail


def _odd_prompt(x, o, l, win, cw, cb3, lcg3, lcb3, wout, lng3, lnb3, *, tm):
    hs = (K_C - 1) * BATCH
    in_specs = [pl.BlockSpec((tm, D_MODEL), lambda i: (i, 0)),
                _whole(win), _layer(cw, o), _layer(cb3, o), _layer(lcg3, o), _layer(lcb3, o),
                _whole(wout), _layer(lng3, l), _layer(lnb3, l)]
    return pl.pallas_call(
        functools.partial(_odd_prompt_kernel, tm=tm),
        grid=(M_PROMPT // tm,),
        in_specs=in_specs,
        out_specs=(pl.BlockSpec((tm, D_MODEL), lambda i: (i, 0)), pl.BlockSpec((hs, D_C), lambda i: (0, 0))),
        out_shape=(jax.ShapeDtypeStruct((M_PROMPT, D_MODEL), F32), jax.ShapeDtypeStruct((hs, D_C), F32)),
        scratch_shapes=[pltpu.VMEM((hs + tm, D_C), F32), pltpu.VMEM((tm, D_C), F32)],
        compiler_params=_params(),
        name="odd_prompt",
    )(x, win, cw, cb3, lcg3, lcb3, wout, lng3, lnb3)


def _odd_sample_kernel(x_ref, c_ref, cn_ref, win_ref, cw_ref, cb_ref, lcg_ref, lcb_ref, wout_ref, lng_ref, lnb_ref,
                       o_ref, oc_ref, g_ref, pc_ref):
    k = pl.program_id(0)
    last = (K_C - 1) // SAMPLE_TAPS - 1

    @pl.when(k == 0)
    def _():
        xbf = x_ref[...].astype(BF16)
        z1 = _dot(xbf, win_ref[:, 0:D_C])
        z2 = _dot(xbf, win_ref[:, D_C:2 * D_C])
        g_ref[...] = z1 * jax.nn.sigmoid(z2)
        pc_ref[...] = jnp.broadcast_to(cb_ref[...], pc_ref.shape)

    pc = pc_ref[...]
    for t in range(SAMPLE_TAPS):
        pc = pc + cw_ref[pl.ds(k * SAMPLE_TAPS + t, 1), :] * c_ref[t]
    pc_ref[...] = pc
    oc_ref[0:SAMPLE_TAPS - 1] = c_ref[1:SAMPLE_TAPS]

    @pl.when(k < last)
    def _():
        oc_ref[SAMPLE_TAPS - 1] = cn_ref[...]

    @pl.when(k == last)
    def _():
        g = g_ref[...]
        oc_ref[SAMPLE_TAPS - 1] = g
        hc = pc + cw_ref[K_C - 1:K_C, :] * g
        h = jax.nn.silu(_ln(hc, lcg_ref[...], lcb_ref[...]))
        out = _dot(h.astype(BF16), wout_ref[...])
        o_ref[...] = _ln(ALPHA * x_ref[...] + out, lng_ref[...], lnb_ref[...])


def _odd_sample(x, cache_t, o, l, win, cw, cb3, lcg3, lcb3, wout, lng3, lnb3):
    nk = K_C - 1
    steps = nk // SAMPLE_TAPS
    in_specs = [_whole(x),
                pl.BlockSpec((None, SAMPLE_TAPS, DEC_BATCH, D_C), lambda k: (o, k, 0, 0)),
                pl.BlockSpec((None, None, DEC_BATCH, D_C),
                             lambda k: (o, jnp.minimum((k + 1) * SAMPLE_TAPS, nk - 1), 0, 0)),
                _whole(win), _layer(cw, o), _layer(cb3, o), _layer(lcg3, o), _layer(lcb3, o),
                _whole(wout), _layer(lng3, l), _layer(lnb3, l)]
    return pl.pallas_call(
        _odd_sample_kernel,
        grid=(steps,),
        in_specs=in_specs,
        out_specs=(pl.BlockSpec((DEC_BATCH, D_MODEL), lambda k: (0, 0)),
                   pl.BlockSpec((SAMPLE_TAPS, DEC_BATCH, D_C), lambda k: (k, 0, 0))),
        out_shape=(jax.ShapeDtypeStruct((DEC_BATCH, D_MODEL), F32),
                   jax.ShapeDtypeStruct((nk, DEC_BATCH, D_C), F32)),
        scratch_shapes=[pltpu.VMEM((DEC_BATCH, D_C), F32), pltpu.VMEM((DEC_BATCH, D_C), F32)],
        compiler_params=_params(),
        name="odd_sample",
    )(x, cache_t, cache_t, win, cw, cb3, lcg3, lcb3, wout, lng3, lnb3)


def kernel(x_prompt, x_sample, state_a_re, state_a_im, cache_c_conv, cache_ffn_conv, w_in_ab, s5_lam_re, s5_lam_im, s5_log_dt, s5_b_re, s5_b_im, s5_c_re, s5_c_im, s5_d, s5_glu_w, s5_glu_b, sgu_ln_g, sgu_ln_b, sgu_w, sgu_b, w_out_ab, w_in_c, conv_c_w, conv_c_b, ln_c_g, ln_c_b, w_out_c, ffn_w_gate, ffn_w_up, ffn_conv_w, ffn_conv_b, ffn_w_down, ln_mix_g, ln_mix_b, ln_ffn_g, ln_ffn_b):
    xs = x_sample.reshape(DEC_BATCH, D_MODEL)
    xp = x_prompt

    assert N_EVEN == 1, "side-stream casts below are laid out for one even layer followed by odd layers"
    win_ab, wout_ab, gluw = w_in_ab.astype(BF16), w_out_ab.astype(BF16), s5_glu_w.astype(BF16)
    cache_c_t = jnp.transpose(cache_c_conv, (0, 2, 1, 3))
    fcb3 = _rows3(ffn_conv_b)
    d3, glub3, sg3, sb3 = _rows3(s5_d), _rows3(s5_glu_b), _rows3(sgu_ln_g), _rows3(sgu_ln_b)
    sbt = jnp.swapaxes(sgu_b, 1, 2)
    ccb3, lcg3, lcb3 = _rows3(conv_c_b), _rows3(ln_c_g), _rows3(ln_c_b)
    lmg3, lmb3, lfg3, lfb3 = _rows3(ln_mix_g), _rows3(ln_mix_b), _rows3(ln_ffn_g), _rows3(ln_ffn_b)

    def ffn_jobs(l):
        return [(ffn_w_gate, l), (ffn_w_up, l), (ffn_w_down, l)]

    sa_re_p, sa_im_p, sa_re_s, sa_im_s, sb_v_s = [], [], [], [], []
    cc_p, cc_s, cf_p, cf_s = [], [], [], []
    for l in range(DEPTH):
        if l % 2 == 0:
            e = l // 2
            lbr, lbi, bbr, bbi, cneg = _s5_prep(
                s5_lam_re[e], s5_lam_im[e], s5_log_dt[e],
                jnp.swapaxes(s5_b_re[e], 1, 2), jnp.swapaxes(s5_b_im[e], 1, 2), s5_c_im[e])
            bbd, cbd, lam = _block_diag_params(lbr, lbi, bbr, bbi, s5_c_re[e], cneg)
            shared = (win_ab, bbd, cbd, lam, d3, gluw, glub3, sg3, sb3)
            (xp, sre, sim), ffn_w = _even_prompt(xp, e, l, *shared, sgu_w, sbt, wout_ab, lmg3, lmb3,
                                                 cast_jobs=ffn_jobs(l))
            sw0 = jnp.repeat(sgu_w[e][:, 0, 0], HD_B).reshape(1, D_B)
            sb0 = jnp.repeat(sgu_b[e][:, 0], HD_B).reshape(1, D_B)
            xs, sres, sims, vn = _even_sample(
                xs, state_a_re[e].reshape(DEC_BATCH, -1), state_a_im[e].reshape(DEC_BATCH, -1),
                e, l, *shared, sw0, sb0, wout_ab, lmg3, lmb3)
            sa_re_p.append(sre.reshape(BATCH, N_GA, P_A))
            sa_im_p.append(sim.reshape(BATCH, N_GA, P_A))
            sa_re_s.append(sres.reshape(DEC_BATCH, N_GA, P_A))
            sa_im_s.append(sims.reshape(DEC_BATCH, N_GA, P_A))
            sb_v_s.append(vn.reshape(DEC_BATCH, 1, D_B))
        else:
            o = l // 2
            win_c, wout_c = odd_w
            rest = (win_c, conv_c_w, ccb3, lcg3, lcb3, wout_c, lmg3, lmb3)
            xp, cp = _odd_prompt(xp, o, l, *rest, tm=ODD_TM)
            xs, cs_t = _odd_sample(xs, cache_c_t, o, l, *rest)
            cc_p.append(jnp.transpose(cp.reshape(K_C - 1, BATCH, D_C), (1, 0, 2)))
            cc_s.append(jnp.transpose(cs_t, (1, 0, 2)))
        wg, wu, wd = ffn_w
        jobs = [(w_in_c, (l + 1) // 2), (w_out_c, (l + 1) // 2)] + ffn_jobs(l + 1) if l + 1 < DEPTH else []
        halo_s = jnp.transpose(cache_ffn_conv[l], (1, 0, 2)).reshape((K_F - 1) * DEC_BATCH, D_FF)
        (xp, fcp, xs, fcs), casts = _ffn(xp, xs, halo_s, l, wg, wu, ffn_conv_w, fcb3, wd, lfg3, lfb3, tm=FFN_TM,
                                         batch_major_out=(l == DEPTH - 1), cast_jobs=jobs)
        if jobs:
            odd_w, ffn_w = casts[:2], casts[2:]
        cf_p.append(jnp.transpose(fcp.reshape(K_F - 1, BATCH, D_FF), (1, 0, 2)))
        cf_s.append(jnp.transpose(fcs.reshape(K_F - 1, DEC_BATCH, D_FF), (1, 0, 2)))

    ys = xs.reshape(DEC_BATCH, 1, D_MODEL)
    return (xp, ys,
            jnp.stack(sa_re_p), jnp.stack(sa_im_p), jnp.stack(sa_re_s), jnp.stack(sa_im_s),
            jnp.stack(sb_v_s),
            jnp.stack(cc_p), jnp.stack(cc_s),
            jnp.stack(cf_p), jnp.stack(cf_s))
```

```python
import functools

import jax
import jax.numpy as jnp
from jax import lax
from jax.experimental import pallas as pl
from jax.experimental.pallas import tpu as pltpu

D_MODEL = 1024
BATCH = 8
SEQ = 2048
DEPTH = 2
DEC_BATCH = 128
N_EVEN = (DEPTH + 1) // 2
N_ODD = DEPTH // 2
D_A = D_MODEL // 2
S5_GROUP = 16
N_GA = D_A // S5_GROUP
P_A = 64
D_B = D_MODEL // 2
N_HB = 4
HD_B = D_B // N_HB
CHUNK = 128
D_C = D_MODEL
K_C = 31
D_FF = ((8 * D_MODEL) // 3 + 127) // 128 * 128
K_F = 3
ALPHA = (2.0 * DEPTH) ** 0.25
LN_EPS = 1e-5

F32 = jnp.float32
BF16 = jnp.bfloat16

LANES = 128
MXU_N = 256
M_PROMPT = BATCH * SEQ
ROWS_PER_CHUNK = CHUNK * BATCH
N_CG = 4
G_PER_CG = N_GA // N_CG
CH_PER_CG = G_PER_CG * S5_GROUP
ST_PER_CG = G_PER_CG * P_A
FF_CHUNKS = ((0, 6 * MXU_N), (6 * MXU_N, D_FF))
FFN_TM = 512
ODD_TM = 1024
CONV_ROWS = 8 * BATCH
SAMPLE_TAPS = 5
N_CAST_CHUNKS = 16
VMEM_LIMIT = 56 * 1024 * 1024


def _ln(x, g, b):
    mu = jnp.mean(x, axis=-1, keepdims=True)
    xc = x - mu
    var = jnp.mean(xc * xc, axis=-1, keepdims=True)
    return xc * lax.rsqrt(var + LN_EPS) * g + b


def _dot(a, b):
    return jnp.dot(a, b, preferred_element_type=F32)


def _whole(arr):
    nd = arr.ndim
    return pl.BlockSpec(arr.shape, lambda *_: (0,) * nd, pipeline_mode=pl.Buffered(1))


def _layer(arr, l):
    nd = arr.ndim
    return pl.BlockSpec((None,) + arr.shape[1:], lambda *_: (l,) + (0,) * (nd - 1),
                        pipeline_mode=pl.Buffered(1))


def _rows3(v):
    return v.reshape(v.shape[0], 1, v.shape[1])


def _params(n_grid=1):
    return pltpu.CompilerParams(dimension_semantics=("arbitrary",) * n_grid,
                                vmem_limit_bytes=VMEM_LIMIT)


def _cast_specs(jobs, n_steps):
    def chunk(i):
        return jnp.minimum(i * N_CAST_CHUNKS // n_steps, N_CAST_CHUNKS - 1)

    in_specs, out_specs, out_shapes = [], [], []
    for w, l in jobs:
        _, r, c = w.shape
        rows = r // N_CAST_CHUNKS
        in_specs.append(pl.BlockSpec((None, rows, c), lambda i, l=l: (l, chunk(i), 0)))
        out_specs.append(pl.BlockSpec((rows, c), lambda i: (chunk(i), 0)))
        out_shapes.append(jax.ShapeDtypeStruct((r, c), BF16))
    return in_specs, out_specs, out_shapes


def _cast_chunks(in_refs, out_refs):
    for w_ref, o_ref in zip(in_refs, out_refs):
        o_ref[...] = w_ref[...].astype(BF16)


def _s5_prep_kernel(lr_ref, li_ref, ldt_ref, br_ref, bi_ref, ci_ref,
                    lbr_ref, lbi_ref, bbr_ref, bbi_ref, cneg_ref):
    lr = lr_ref[...]
    li = li_ref[...]
    dt = jnp.exp(ldt_ref[...])
    mag = jnp.exp(lr * dt)
    lbr = mag * jnp.cos(li * dt)
    lbi = mag * jnp.sin(li * dt)
    lbr_ref[...] = lbr
    lbi_ref[...] = lbi
    nr = lbr - 1.0
    ni = lbi
    den = lr * lr + li * li
    qr = (nr * lr + ni * li) / den
    qi = (ni * lr - nr * li) / den
    br = br_ref[...]
    bi = bi_ref[...]
    qr3 = qr[:, None, :]
    qi3 = qi[:, None, :]
    bbr_ref[...] = qr3 * br - qi3 * bi
    bbi_ref[...] = qr3 * bi + qi3 * br
    cneg_ref[...] = -ci_ref[...]


def _s5_prep(lam_re, lam_im, log_dt, b_re, b_im, c_im):
    gp = jax.ShapeDtypeStruct((N_GA, P_A), F32)
    gcp = jax.ShapeDtypeStruct((N_GA, S5_GROUP, P_A), F32)
    return pl.pallas_call(
        _s5_prep_kernel,
        out_shape=(gp, gp, gcp, gcp, gcp),
        name="s5_prep",
    )(lam_re, lam_im, log_dt.reshape(N_GA, 1), b_re, b_im, c_im)


def _block_diag_kernel(bre_ref, bim_ref, cre_ref, cim_ref, bbd_ref, cbd_ref):
    def iota(shape, axis):
        return lax.broadcasted_iota(jnp.int32, shape, axis)

    p_bits = P_A.bit_length() - 1
    c_bits = S5_GROUP.bit_length() - 1
    rep_cols = (jnp.bitwise_and(iota((P_A, ST_PER_CG), 1), P_A - 1) == iota((P_A, ST_PER_CG), 0)).astype(BF16)
    rep_rows = (jnp.bitwise_and(iota((ST_PER_CG, P_A), 0), P_A - 1) == iota((ST_PER_CG, P_A), 1)).astype(BF16)
    diag_b = (lax.shift_right_logical(iota((CH_PER_CG, ST_PER_CG), 0), c_bits)
              == lax.shift_right_logical(iota((CH_PER_CG, ST_PER_CG), 1), p_bits))
    diag_c = (lax.shift_right_logical(iota((ST_PER_CG, CH_PER_CG), 0), p_bits)
              == lax.shift_right_logical(iota((ST_PER_CG, CH_PER_CG), 1), c_bits))
    for j in range(N_CG):
        for part, src in enumerate((bre_ref, bim_ref)):
            x = _dot(src[j].astype(BF16), rep_cols)
            bbd_ref[j, :, ST_PER_CG * part:ST_PER_CG * (part + 1)] = jnp.where(diag_b, x, 0.0).astype(BF16)
        for part, src in enumerate((cre_ref, cim_ref)):
            y = lax.dot_general(rep_rows, src[j].astype(BF16), (((1,), (1,)), ((), ())),
                                preferred_element_type=F32)
            cbd_ref[j, ST_PER_CG * part:ST_PER_CG * (part + 1), :] = jnp.where(diag_c, y, 0.0).astype(BF16)


def _block_diag_params(lbr, lbi, bbr, bbi, c_re, cneg):
    slabs = [m.reshape(N_CG, CH_PER_CG, P_A) for m in (bbr, bbi, c_re, cneg)]
    bbd, cbd = pl.pallas_call(
        _block_diag_kernel,
        out_shape=(jax.ShapeDtypeStruct((N_CG, CH_PER_CG, 2 * ST_PER_CG), BF16),
                   jax.ShapeDtypeStruct((N_CG, 2 * ST_PER_CG, CH_PER_CG), BF16)),
        name="s5_block_diag",
    )(*slabs)
    lam = jnp.stack([lbr.reshape(-1), lbi.reshape(-1)])
    return bbd, cbd, lam


def _even_prompt_kernel(x_ref, win_ref, bbd_ref, cbd_ref, lam_ref, d_ref, gluw_ref, glub_ref,
                        sg_ref, sb_ref, sw_ref, sbt_ref, wout_ref, lng_ref, lnb_ref, *rest, n_cast):
    cast_in, rest = rest[:n_cast], rest[n_cast:]
    o_ref, sre_ref, sim_ref = rest[:3]
    cast_out, rest = rest[3:3 + n_cast], rest[3 + n_cast:]
    st_ref, xt_ref, scr_ref, cat_ref, vn_ref, gt_ref = rest

    @pl.when(pl.program_id(0) == 0)
    def _():
        st_ref[...] = jnp.zeros_like(st_ref)

    _cast_chunks(cast_in, cast_out)

    for c in range(D_MODEL // LANES):
        for b in range(BATCH):
            xt_ref.at[c][pl.ds(b, CHUNK, stride=BATCH), :] = x_ref[b, :, LANES * c:LANES * (c + 1)]
    xb = jnp.concatenate([xt_ref[c] for c in range(D_MODEL // LANES)], axis=-1)
    xbf = xb.astype(BF16)

    ua = _dot(xbf, win_ref[:, 0:D_A])
    uab = ua.astype(BF16)
    ys = []
    for j in range(N_CG):
        re_cols = slice(2 * ST_PER_CG * j, 2 * ST_PER_CG * j + ST_PER_CG)
        im_cols = slice(2 * ST_PER_CG * j + ST_PER_CG, 2 * ST_PER_CG * (j + 1))
        scr_ref[...] = _dot(uab[:, CH_PER_CG * j:CH_PER_CG * (j + 1)], bbd_ref[j])
        lr = jnp.broadcast_to(lam_ref[0:1, ST_PER_CG * j:ST_PER_CG * (j + 1)], (BATCH, ST_PER_CG))
        li = jnp.broadcast_to(lam_ref[1:2, ST_PER_CG * j:ST_PER_CG * (j + 1)], (BATCH, ST_PER_CG))

        def step(t, carry, lr=lr, li=li):
            sr, si = carry
            row = pl.multiple_of(t * BATCH, BATCH)
            nr = lr * sr - li * si + scr_ref[pl.ds(row, BATCH), 0:ST_PER_CG]
            ni = lr * si + li * sr + scr_ref[pl.ds(row, BATCH), ST_PER_CG:2 * ST_PER_CG]
            scr_ref[pl.ds(row, BATCH), 0:ST_PER_CG] = nr
            scr_ref[pl.ds(row, BATCH), ST_PER_CG:2 * ST_PER_CG] = ni
            return nr, ni

        sr, si = lax.fori_loop(0, CHUNK, step, (st_ref[:, re_cols], st_ref[:, im_cols]), unroll=4)
        st_ref[:, re_cols] = sr
        st_ref[:, im_cols] = si
        sre_ref[:, ST_PER_CG * j:ST_PER_CG * (j + 1)] = sr
        sim_ref[:, ST_PER_CG * j:ST_PER_CG * (j + 1)] = si
        ys.append(_dot(scr_ref[...].astype(BF16), cbd_ref[j]))
    y = jnp.concatenate(ys, axis=-1) + d_ref[...] * ua
    g = jax.nn.gelu(y, approximate=True)
    ya = g * jax.nn.sigmoid(_dot(g.astype(BF16), gluw_ref[...]) + glub_ref[...])
    cat_ref[:, 0:D_A] = ya.astype(BF16)

    ub = _dot(xbf, win_ref[:, D_A:D_A + D_B])
    vb = _dot(xbf, win_ref[:, D_A + D_B:D_A + 2 * D_B])
    vn = _ln(vb, sg_ref[...], sb_ref[...])
    for h in range(N_HB):
        vn_ref[h] = vn[:, HD_B * h:HD_B * (h + 1)]
    r_id = lax.broadcasted_iota(jnp.int32, (CHUNK, CHUNK), 0)
    c_id = lax.broadcasted_iota(jnp.int32, (CHUNK, CHUNK), 1)
    tril = (c_id <= r_id).astype(F32)
    wm = [(sw_ref[h] * tril).astype(BF16) for h in range(N_HB)]
    for b in range(0, BATCH, 2):
        for h in range(N_HB):
            v2 = jnp.concatenate([vn_ref.at[h][pl.ds(b, CHUNK, stride=BATCH), :],
                                  vn_ref.at[h][pl.ds(b + 1, CHUNK, stride=BATCH), :]], axis=-1)
            g2 = _dot(wm[h], v2.astype(BF16)) + sbt_ref[:, h:h + 1]
            gt_ref.at[h][pl.ds(b, CHUNK, stride=BATCH), :] = g2[:, 0:HD_B]
            gt_ref.at[h][pl.ds(b + 1, CHUNK, stride=BATCH), :] = g2[:, HD_B:]
    gate = jnp.concatenate([gt_ref[h] for h in range(N_HB)], axis=-1)
    cat_ref[:, D_A:] = (ub * gate).astype(BF16)

    out = _dot(cat_ref[...], wout_ref[...])
    o_ref[...] = _ln(ALPHA * xb + out, lng_ref[...], lnb_ref[...])


def _even_prompt(x, e, l, win, bbd, cbd, lam, d3, gluw, glub3, sg3, sb3, sw, sbt, wout, lng3, lnb3, *, cast_jobs=()):
    rows = ROWS_PER_CHUNK
    n = SEQ // CHUNK
    c_in, c_out, c_shapes = _cast_specs(cast_jobs, n)
    in_specs = [pl.BlockSpec((BATCH, CHUNK, D_MODEL), lambda i: (0, i, 0)),
                _layer(win, e), _whole(bbd), _whole(cbd), _whole(lam), _layer(d3, e), _layer(gluw, e),
                _layer(glub3, e), _layer(sg3, e), _layer(sb3, e), _layer(sw, e), _layer(sbt, e),
                _layer(wout, e), _layer(lng3, l), _layer(lnb3, l)] + c_in
    st = jax.ShapeDtypeStruct((BATCH, N_GA * P_A), F32)
    st_spec = pl.BlockSpec((BATCH, N_GA * P_A), lambda i: (0, 0))
    outs = pl.pallas_call(
        functools.partial(_even_prompt_kernel, n_cast=len(cast_jobs)),
        grid=(n,),
        in_specs=in_specs,
        out_specs=(pl.BlockSpec((rows, D_MODEL), lambda i: (i, 0)), st_spec, st_spec, *c_out),
        out_shape=(jax.ShapeDtypeStruct((M_PROMPT, D_MODEL), F32), st, st, *c_shapes),
        scratch_shapes=[pltpu.VMEM((BATCH, 2 * N_GA * P_A), F32),
                        pltpu.VMEM((D_MODEL // LANES, rows, LANES), F32),
                        pltpu.VMEM((rows, 2 * ST_PER_CG), F32),
                        pltpu.VMEM((rows, D_MODEL), BF16),
                        pltpu.VMEM((N_HB, rows, HD_B), F32),
                        pltpu.VMEM((N_HB, rows, HD_B), F32)],
        compiler_params=_params(),
        name="even_prompt",
    )(x, win, bbd, cbd, lam, d3, gluw, glub3, sg3, sb3, sw, sbt, wout, lng3, lnb3, *[w for w, _ in cast_jobs])
    return outs[:3], outs[3:]


def _even_sample_kernel(x_ref, s0r_ref, s0i_ref, win_ref, bbd_ref, cbd_ref, lam_ref, d_ref,
                        gluw_ref, glub_ref, sg_ref, sb_ref, sw0_ref, sb0_ref, wout_ref, lng_ref, lnb_ref,
                        o_ref, sre_ref, sim_ref, vn_ref):
    xb = x_ref[...]
    xbf = xb.astype(BF16)
    ua = _dot(xbf, win_ref[:, 0:D_A])
    uab = ua.astype(BF16)
    ys = []
    for j in range(N_CG):
        cols = slice(ST_PER_CG * j, ST_PER_CG * (j + 1))
        bu = _dot(uab[:, CH_PER_CG * j:CH_PER_CG * (j + 1)], bbd_ref[j])
        lr = lam_ref[0:1, cols]
        li = lam_ref[1:2, cols]
        s0r = s0r_ref[:, cols]
        s0i = s0i_ref[:, cols]
        sr = lr * s0r - li * s0i + bu[:, 0:ST_PER_CG]
        si = lr * s0i + li * s0r + bu[:, ST_PER_CG:]
        sre_ref[:, cols] = sr
        sim_ref[:, cols] = si
        ys.append(_dot(jnp.concatenate([sr, si], axis=-1).astype(BF16), cbd_ref[j]))
    y = jnp.concatenate(ys, axis=-1) + d_ref[...] * ua
    g = jax.nn.gelu(y, approximate=True)
    ya = g * jax.nn.sigmoid(_dot(g.astype(BF16), gluw_ref[...]) + glub_ref[...])

    ub = _dot(xbf, win_ref[:, D_A:D_A + D_B])
    vb = _dot(xbf, win_ref[:, D_A + D_B:D_A + 2 * D_B])
    vn = _ln(vb, sg_ref[...], sb_ref[...])
    vn_ref[...] = vn
    yb = ub * (sw0_ref[...] * vn + sb0_ref[...])

    cat = jnp.concatenate([ya, yb], axis=-1).astype(BF16)
    out = _dot(cat, wout_ref[...])
    o_ref[...] = _ln(ALPHA * xb + out, lng_ref[...], lnb_ref[...])


def _even_sample(x, s0r, s0i, e, l, win, bbd, cbd, lam, d3, gluw, glub3, sg3, sb3, sw0, sb0, wout, lng3, lnb3):
    in_specs = [_whole(x), _whole(s0r), _whole(s0i),
                _layer(win, e), _whole(bbd), _whole(cbd), _whole(lam), _layer(d3, e), _layer(gluw, e),
                _layer(glub3, e), _layer(sg3, e), _layer(sb3, e), _whole(sw0), _whole(sb0),
                _layer(wout, e), _layer(lng3, l), _layer(lnb3, l)]
    shapes = ((DEC_BATCH, D_MODEL), (DEC_BATCH, N_GA * P_A), (DEC_BATCH, N_GA * P_A), (DEC_BATCH, D_B))
    return pl.pallas_call(
        _even_sample_kernel,
        grid=(1,),
        in_specs=in_specs,
        out_specs=tuple(pl.BlockSpec(s, lambda i: (0, 0)) for s in shapes),
        out_shape=tuple(jax.ShapeDtypeStruct(s, F32) for s in shapes),
        compiler_params=_params(),
        name="even_sample",
    )(x, s0r, s0i, win, bbd, cbd, lam, d3, gluw, glub3, sg3, sb3, sw0, sb0, wout, lng3, lnb3)


def _store_batch_major(y, o_ref, yt_ref, tm):
    for c in range(D_MODEL // LANES):
        yt_ref[c] = y[:, LANES * c:LANES * (c + 1)]
    for c in range(D_MODEL // LANES):
        for b in range(BATCH):
            o_ref[b, :, LANES * c:LANES * (c + 1)] = yt_ref.at[c][pl.ds(b, tm // BATCH, stride=BATCH), :]


def _ffn_tile(xb, gs_ref, rows, shift, wg_ref, wu_ref, cw_ref, cb_ref, wd_ref, lng_ref, lnb_ref):
    hs = (K_F - 1) * shift
    xbf = xb.astype(BF16)
    acc = None
    for c0, c1 in FF_CHUNKS:
        gs_ref[hs:hs + rows, c0:c1] = _dot(xbf, wg_ref[:, c0:c1])
        conv = cb_ref[:, c0:c1]
        for k in range(K_F):
            conv = conv + cw_ref[k:k + 1, c0:c1] * gs_ref[k * shift:k * shift + rows, c0:c1]
        up = _dot(xbf, wu_ref[:, c0:c1])
        h = (jax.nn.silu(conv) * up).astype(BF16)
        part = _dot(h, wd_ref[c0:c1, :])
        acc = part if acc is None else acc + part
    return _ln(ALPHA * xb + acc, lng_ref[...], lnb_ref[...])


def _ffn_kernel(x_ref, xs_ref, halo_s_ref, wg_ref, wu_ref, cw_ref, cb_ref, wd_ref, lng_ref, lnb_ref, *rest,
                tm, n, n_cast, batch_major_out):
    cast_in, rest = rest[:n_cast], rest[n_cast:]
    o_ref, cache_ref, os_ref, cache_s_ref = rest[:4]
    cast_out, rest = rest[4:4 + n_cast], rest[4 + n_cast:]
    gs_ref = rest[0]
    i = pl.program_id(0)
    hs = (K_F - 1) * BATCH
    hs_s = (K_F - 1) * DEC_BATCH
    weights = (wg_ref, wu_ref, cw_ref, cb_ref, wd_ref, lng_ref, lnb_ref)

    @pl.when(i == 0)
    def _():
        gs_ref[0:hs, :] = jnp.zeros((hs, D_FF), F32)

    @pl.when(i < n)
    def _():
        _cast_chunks(cast_in, cast_out)
        y = _ffn_tile(x_ref[...], gs_ref, tm, BATCH, *weights)
        if batch_major_out:
            _store_batch_major(y, o_ref, rest[1], tm)
        else:
            o_ref[...] = y
        tail = gs_ref[tm:tm + hs, :]
        cache_ref[...] = tail
        gs_ref[0:hs, :] = tail

    @pl.when(i == n)
    def _():
        gs_ref[0:hs_s, :] = halo_s_ref[...]
        os_ref[...] = _ffn_tile(xs_ref[...], gs_ref, DEC_BATCH, DEC_BATCH, *weights)
        cache_s_ref[...] = gs_ref[DEC_BATCH:DEC_BATCH + hs_s, :]


def _ffn(x, xs, halo_s, l, wg, wu, cw, cb3, wd, lng3, lnb3, *, tm, batch_major_out=False, cast_jobs=()):
    m = x.shape[0]
    n = m // tm
    hs = (K_F - 1) * BATCH
    hs_s = (K_F - 1) * DEC_BATCH
    c_in, c_out, c_shapes = _cast_specs(cast_jobs, n)
    in_specs = [pl.BlockSpec((tm, D_MODEL), lambda i: (jnp.minimum(i, n - 1), 0)), _whole(xs), _layer(halo_s, l),
                _whole(wg), _whole(wu), _layer(cw, l), _layer(cb3, l), _whole(wd),
                _layer(lng3, l), _layer(lnb3, l)] + c_in
    scratch = [pltpu.VMEM((max(hs + tm, hs_s + DEC_BATCH), D_FF), F32)]
    if batch_major_out:
        o_spec = pl.BlockSpec((BATCH, tm // BATCH, D_MODEL), lambda i: (0, jnp.minimum(i, n - 1), 0))
        o_shape = jax.ShapeDtypeStruct((BATCH, m // BATCH, D_MODEL), F32)
        scratch.append(pltpu.VMEM((D_MODEL // LANES, tm, LANES), F32))
    else:
        o_spec = pl.BlockSpec((tm, D_MODEL), lambda i: (jnp.minimum(i, n - 1), 0))
        o_shape = jax.ShapeDtypeStruct((m, D_MODEL), F32)
    outs = pl.pallas_call(
        functools.partial(_ffn_kernel, tm=tm, n=n, n_cast=len(cast_jobs), batch_major_out=batch_major_out),
        grid=(n + 1,),
        in_specs=in_specs,
        out_specs=(o_spec, pl.BlockSpec((hs, D_FF), lambda i: (0, 0)),
                   pl.BlockSpec((DEC_BATCH, D_MODEL), lambda i: (0, 0)),
                   pl.BlockSpec((hs_s, D_FF), lambda i: (0, 0)), *c_out),
        out_shape=(o_shape, jax.ShapeDtypeStruct((hs, D_FF), F32),
                   jax.ShapeDtypeStruct((DEC_BATCH, D_MODEL), F32),
                   jax.ShapeDtypeStruct((hs_s, D_FF), F32), *c_shapes),
        scratch_shapes=scratch,
        compiler_params=_params(),
        name="conv_ffn",
    )(x, xs, halo_s, wg, wu, cw, cb3, wd, lng3, lnb3, *[w for w, _ in cast_jobs])
    return outs[:4], outs[4:]


def _odd_prompt_kernel(x_ref, win_ref, cw_ref, cb_ref, lcg_ref, lcb_ref, wout_ref, lng_ref, lnb_ref,
                       o_ref, cache_ref, gs_ref, hc_ref, *, tm):
    hs = (K_C - 1) * BATCH

    @pl.when(pl.program_id(0) == 0)
    def _():
        gs_ref[0:hs, :] = jnp.zeros((hs, D_C), F32)

    xb = x_ref[...]
    xbf = xb.astype(BF16)
    z1 = _dot(xbf, win_ref[:, 0:D_C])
    z2 = _dot(xbf, win_ref[:, D_C:2 * D_C])
    gs_ref[hs:hs + tm, :] = z1 * jax.nn.sigmoid(z2)

    def conv_rows(i, _):
        r0 = pl.multiple_of(i * CONV_ROWS, CONV_ROWS)
        for l0 in range(0, D_C, LANES):
            cols = slice(l0, l0 + LANES)
            win = gs_ref[pl.ds(r0, CONV_ROWS + hs), cols]
            acc = jnp.broadcast_to(cb_ref[:, cols], (CONV_ROWS, LANES))
            for k in range(K_C):
                acc = acc + cw_ref[k:k + 1, cols] * win[k * BATCH:k * BATCH + CONV_ROWS, :]
            hc_ref[pl.ds(r0, CONV_ROWS), cols] = acc
        return 0

    lax.fori_loop(0, tm // CONV_ROWS, conv_rows, 0)
    h = jax.nn.silu(_ln(hc_ref[...], lcg_ref[...], lcb_ref[...]))
    out = _dot(h.astype(BF16), wout_ref[...])
    o_ref[...] = _ln(ALPHA * xb + out, lng_ref[...], lnb_ref[...])
    tail = gs_ref[tm:tm + hs, :]
    cache_ref[...] = tail
    gs_ref[0:hs, :] = tail


def _odd_prompt(x, o, l, win, cw, cb3, lcg3, lcb3, wout, lng3, lnb3, *, tm):
    hs = (K_C - 1) * BATCH
    in_specs = [pl.BlockSpec((tm, D_MODEL), lambda i: (i, 0)),
                _whole(win), _layer(cw, o), _layer(cb3, o), _layer(lcg3, o), _layer(lcb3, o),
                _whole(wout), _layer(lng3, l), _layer(lnb3, l)]
    return pl.pallas_call(
        functools.partial(_odd_prompt_kernel, tm=tm),
        grid=(M_PROMPT // tm,),
        in_specs=in_specs,
        out_specs=(pl.BlockSpec((tm, D_MODEL), lambda i: (i, 0)), pl.BlockSpec((hs, D_C), lambda i: (0, 0))),
        out_shape=(jax.ShapeDtypeStruct((M_PROMPT, D_MODEL), F32), jax.ShapeDtypeStruct((hs, D_C), F32)),
        scratch_shapes=[pltpu.VMEM((hs + tm, D_C), F32), pltpu.VMEM((tm, D_C), F32)],
        compiler_params=_params(),
        name="odd_prompt",
    )(x, win, cw, cb3, lcg3, lcb3, wout, lng3, lnb3)


def _odd_sample_kernel(x_ref, c_ref, cn_ref, win_ref, cw_ref, cb_ref, lcg_ref, lcb_ref, wout_ref, lng_ref, lnb_ref,
                       o_ref, oc_ref, g_ref, pc_ref):
    k = pl.program_id(0)
    last = (K_C - 1) // SAMPLE_TAPS - 1

    @pl.when(k == 0)
    def _():
        xbf = x_ref[...].astype(BF16)
        z1 = _dot(xbf, win_ref[:, 0:D_C])
        z2 = _dot(xbf, win_ref[:, D_C:2 * D_C])
        g_ref[...] = z1 * jax.nn.sigmoid(z2)
        pc_ref[...] = jnp.broadcast_to(cb_ref[...], pc_ref.shape)

    pc = pc_ref[...]
    for t in range(SAMPLE_TAPS):
        pc = pc + cw_ref[pl.ds(k * SAMPLE_TAPS + t, 1), :] * c_ref[t]
    pc_ref[...] = pc
    oc_ref[0:SAMPLE_TAPS - 1] = c_ref[1:SAMPLE_TAPS]

    @pl.when(k < last)
    def _():
        oc_ref[SAMPLE_TAPS - 1] = cn_ref[...]

    @pl.when(k == last)
    def _():
        g = g_ref[...]
        oc_ref[SAMPLE_TAPS - 1] = g
        hc = pc + cw_ref[K_C - 1:K_C, :] * g
        h = jax.nn.silu(_ln(hc, lcg_ref[...], lcb_ref[...]))
        out = _dot(h.astype(BF16), wout_ref[...])
        o_ref[...] = _ln(ALPHA * x_ref[...] + out, lng_ref[...], lnb_ref[...])


def _odd_sample(x, cache_t, o, l, win, cw, cb3, lcg3, lcb3, wout, lng3, lnb3):
    nk = K_C - 1
    steps = nk // SAMPLE_TAPS
    in_specs = [_whole(x),
                pl.BlockSpec((None, SAMPLE_TAPS, DEC_BATCH, D_C), lambda k: (o, k, 0, 0)),
                pl.BlockSpec((None, None, DEC_BATCH, D_C),
                             lambda k: (o, jnp.minimum((k + 1) * SAMPLE_TAPS, nk - 1), 0, 0)),
                _whole(win), _layer(cw, o), _layer(cb3, o), _layer(lcg3, o), _layer(lcb3, o),
                _whole(wout), _layer(lng3, l), _layer(lnb3, l)]
    return pl.pallas_call(
        _odd_sample_kernel,
        grid=(steps,),
        in_specs=in_specs,
        out_specs=(pl.BlockSpec((DEC_BATCH, D_MODEL), lambda k: (0, 0)),
                   pl.BlockSpec((SAMPLE_TAPS, DEC_BATCH, D_C), lambda k: (k, 0, 0))),
        out_shape=(jax.ShapeDtypeStruct((DEC_BATCH, D_MODEL), F32),
                   jax.ShapeDtypeStruct((nk, DEC_BATCH, D_C), F32)),
        scratch_shapes=[pltpu.VMEM((DEC_BATCH, D_C), F32), pltpu.VMEM((DEC_BATCH, D_C), F32)],
        compiler_params=_params(),
        name="odd_sample",
    )(x, cache_t, cache_t, win, cw, cb3, lcg3, lcb3, wout, lng3, lnb3)


def kernel(x_prompt, x_sample, state_a_re, state_a_im, cache_c_conv, cache_ffn_conv, w_in_ab, s5_lam_re, s5_lam_im, s5_log_dt, s5_b_re, s5_b_im, s5_c_re, s5_c_im, s5_d, s5_glu_w, s5_glu_b, sgu_ln_g, sgu_ln_b, sgu_w, sgu_b, w_out_ab, w_in_c, conv_c_w, conv_c_b, ln_c_g, ln_c_b, w_out_c, ffn_w_gate, ffn_w_up, ffn_conv_w, ffn_conv_b, ffn_w_down, ln_mix_g, ln_mix_b, ln_ffn_g, ln_ffn_b):
    xs = x_sample.reshape(DEC_BATCH, D_MODEL)
    xp = x_prompt

    assert N_EVEN == 1, "side-stream casts below are laid out for one even layer followed by odd layers"
    win_ab, wout_ab, gluw = w_in_ab.astype(BF16), w_out_ab.astype(BF16), s5_glu_w.astype(BF16)
    cache_c_t = jnp.transpose(cache_c_conv, (0, 2, 1, 3))
    halo_all = jnp.transpose(cache_ffn_conv, (0, 2, 1, 3)).reshape(DEPTH, (K_F - 1) * DEC_BATCH, D_FF)
    fcb3 = _rows3(ffn_conv_b)
    d3, glub3, sg3, sb3 = _rows3(s5_d), _rows3(s5_glu_b), _rows3(sgu_ln_g), _rows3(sgu_ln_b)
    sbt = jnp.swapaxes(sgu_b, 1, 2)
    ccb3, lcg3, lcb3 = _rows3(conv_c_b), _rows3(ln_c_g), _rows3(ln_c_b)
    lmg3, lmb3, lfg3, lfb3 = _rows3(ln_mix_g), _rows3(ln_mix_b), _rows3(ln_ffn_g), _rows3(ln_ffn_b)

    def ffn_jobs(l):
        return [(ffn_w_gate, l), (ffn_w_up, l), (ffn_w_down, l)]

    sa_re_p, sa_im_p, sa_re_s, sa_im_s, sb_v_s = [], [], [], [], []
    cc_p, cc_s, cf_p, cf_s = [], [], [], []
    for l in range(DEPTH):
        if l % 2 == 0:
            e = l // 2
            lbr, lbi, bbr, bbi, cneg = _s5_prep(
                s5_lam_re[e], s5_lam_im[e], s5_log_dt[e],
                jnp.swapaxes(s5_b_re[e], 1, 2), jnp.swapaxes(s5_b_im[e], 1, 2), s5_c_im[e])
            bbd, cbd, lam = _block_diag_params(lbr, lbi, bbr, bbi, s5_c_re[e], cneg)
            shared = (win_ab, bbd, cbd, lam, d3, gluw, glub3, sg3, sb3)
            (xp, sre, sim), ffn_w = _even_prompt(xp, e, l, *shared, sgu_w, sbt, wout_ab, lmg3, lmb3,
                                                 cast_jobs=ffn_jobs(l))
            sw0 = jnp.repeat(sgu_w[e][:, 0, 0], HD_B).reshape(1, D_B)
            sb0 = jnp.repeat(sgu_b[e][:, 0], HD_B).reshape(1, D_B)
            xs, sres, sims, vn = _even_sample(
                xs, state_a_re[e].reshape(DEC_BATCH, -1), state_a_im[e].reshape(DEC_BATCH, -1),
                e, l, *shared, sw0, sb0, wout_ab, lmg3, lmb3)
            sa_re_p.append(sre.reshape(BATCH, N_GA, P_A))
            sa_im_p.append(sim.reshape(BATCH, N_GA, P_A))
            sa_re_s.append(sres.reshape(DEC_BATCH, N_GA, P_A))
            sa_im_s.append(sims.reshape(DEC_BATCH, N_GA, P_A))
            sb_v_s.append(vn.reshape(DEC_BATCH, 1, D_B))
        else:
            o = l // 2
            win_c, wout_c = odd_w
            rest = (win_c, conv_c_w, ccb3, lcg3, lcb3, wout_c, lmg3, lmb3)
            xp, cp = _odd_prompt(xp, o, l, *rest, tm=ODD_TM)
            xs, cs_t = _odd_sample(xs, cache_c_t, o, l, *rest)
            cc_p.append(jnp.transpose(cp.reshape(K_C - 1, BATCH, D_C), (1, 0, 2)))
            cc_s.append(jnp.transpose(cs_t, (1, 0, 2)))
        wg, wu, wd = ffn_w
        jobs = [(w_in_c, (l + 1) // 2), (w_out_c, (l + 1) // 2)] + ffn_jobs(l + 1) if l + 1 < DEPTH else []
        (xp, fcp, xs, fcs), casts = _ffn(xp, xs, halo_all, l, wg, wu, ffn_conv_w, fcb3, wd, lfg3, lfb3, tm=FFN_TM,
                                         batch_major_out=(l == DEPTH - 1), cast_jobs=jobs)
        if jobs:
            odd_w, ffn_w = casts[:2], casts[2:]
        cf_p.append(fcp)
        cf_s.append(fcs)

    def position_minor(tails, rows):
        return jnp.transpose(jnp.stack(tails).reshape(DEPTH, K_F - 1, rows, D_FF), (0, 2, 1, 3))

    ys = xs.reshape(DEC_BATCH, 1, D_MODEL)
    return (xp, ys,
            jnp.stack(sa_re_p), jnp.stack(sa_im_p), jnp.stack(sa_re_s), jnp.stack(sa_im_s),
            jnp.stack(sb_v_s),
            jnp.stack(cc_p), jnp.stack(cc_s),
            position_minor(cf_p, BATCH), position_minor(cf_s, DEC_BATCH))
```

```python
import functools

import jax
import jax.numpy as jnp
from jax import lax
from jax.experimental import pallas as pl
from jax.experimental.pallas import tpu as pltpu

D_MODEL = 1024
BATCH = 8
SEQ = 2048
DEPTH = 2
DEC_BATCH = 128
N_EVEN = (DEPTH + 1) // 2
N_ODD = DEPTH // 2
D_A = D_MODEL // 2
S5_GROUP = 16
N_GA = D_A // S5_GROUP
P_A = 64
D_B = D_MODEL // 2
N_HB = 4
HD_B = D_B // N_HB
CHUNK = 128
D_C = D_MODEL
K_C = 31
D_FF = ((8 * D_MODEL) // 3 + 127) // 128 * 128
K_F = 3
ALPHA = (2.0 * DEPTH) ** 0.25
LN_EPS = 1e-5

F32 = jnp.float32
BF16 = jnp.bfloat16

LANES = 128
MXU_N = 256
M_PROMPT = BATCH * SEQ
ROWS_PER_CHUNK = CHUNK * BATCH
N_CG = 4
G_PER_CG = N_GA // N_CG
CH_PER_CG = G_PER_CG * S5_GROUP
ST_PER_CG = G_PER_CG * P_A
FF_CHUNKS = ((0, 6 * MXU_N), (6 * MXU_N, D_FF))
FFN_TM = 512
ODD_TM = 1024
CONV_ROWS = 8 * BATCH
SAMPLE_TAPS = 5
N_CAST_CHUNKS = 16
VMEM_LIMIT = 56 * 1024 * 1024


def _ln(x, g, b):
    mu = jnp.mean(x, axis=-1, keepdims=True)
    xc = x - mu
    var = jnp.mean(xc * xc, axis=-1, keepdims=True)
    return xc * lax.rsqrt(var + LN_EPS) * g + b


def _dot(a, b):
    return jnp.dot(a, b, preferred_element_type=F32)


def _whole(arr):
    nd = arr.ndim
    return pl.BlockSpec(arr.shape, lambda *_: (0,) * nd, pipeline_mode=pl.Buffered(1))


def _layer(arr, l):
    nd = arr.ndim
    return pl.BlockSpec((None,) + arr.shape[1:], lambda *_: (l,) + (0,) * (nd - 1),
                        pipeline_mode=pl.Buffered(1))


def _rows3(v):
    return v.reshape(v.shape[0], 1, v.shape[1])


def _params(n_grid=1):
    return pltpu.CompilerParams(dimension_semantics=("arbitrary",) * n_grid,
                                vmem_limit_bytes=VMEM_LIMIT)


def _cast_specs(jobs, n_steps):
    def chunk(i):
        return jnp.minimum(i * N_CAST_CHUNKS // n_steps, N_CAST_CHUNKS - 1)

    in_specs, out_specs, out_shapes = [], [], []
    for w, l in jobs:
        _, r, c = w.shape
        rows = r // N_CAST_CHUNKS
        in_specs.append(pl.BlockSpec((None, rows, c), lambda i, l=l: (l, chunk(i), 0)))
        out_specs.append(pl.BlockSpec((rows, c), lambda i: (chunk(i), 0)))
        out_shapes.append(jax.ShapeDtypeStruct((r, c), BF16))
    return in_specs, out_specs, out_shapes


def _cast_chunks(in_refs, out_refs):
    for w_ref, o_ref in zip(in_refs, out_refs):
        o_ref[...] = w_ref[...].astype(BF16)


def _s5_prep_kernel(lr_ref, li_ref, ldt_ref, br_ref, bi_ref, ci_ref,
                    lbr_ref, lbi_ref, bbr_ref, bbi_ref, cneg_ref):
    lr = lr_ref[...]
    li = li_ref[...]
    dt = jnp.exp(ldt_ref[...])
    mag = jnp.exp(lr * dt)
    lbr = mag * jnp.cos(li * dt)
    lbi = mag * jnp.sin(li * dt)
    lbr_ref[...] = lbr
    lbi_ref[...] = lbi
    nr = lbr - 1.0
    ni = lbi
    den = lr * lr + li * li
    qr = (nr * lr + ni * li) / den
    qi = (ni * lr - nr * li) / den
    br = br_ref[...]
    bi = bi_ref[...]
    qr3 = qr[:, None, :]
    qi3 = qi[:, None, :]
    bbr_ref[...] = qr3 * br - qi3 * bi
    bbi_ref[...] = qr3 * bi + qi3 * br
    cneg_ref[...] = -ci_ref[...]


def _s5_prep(lam_re, lam_im, log_dt, b_re, b_im, c_im):
    gp = jax.ShapeDtypeStruct((N_GA, P_A), F32)
    gcp = jax.ShapeDtypeStruct((N_GA, S5_GROUP, P_A), F32)
    return pl.pallas_call(
        _s5_prep_kernel,
        out_shape=(gp, gp, gcp, gcp, gcp),
        name="s5_prep",
    )(lam_re, lam_im, log_dt.reshape(N_GA, 1), b_re, b_im, c_im)


def _block_diag_kernel(bre_ref, bim_ref, cre_ref, cim_ref, bbd_ref, cbd_ref):
    def iota(shape, axis):
        return lax.broadcasted_iota(jnp.int32, shape, axis)

    p_bits = P_A.bit_length() - 1
    c_bits = S5_GROUP.bit_length() - 1
    rep_cols = (jnp.bitwise_and(iota((P_A, ST_PER_CG), 1), P_A - 1) == iota((P_A, ST_PER_CG), 0)).astype(BF16)
    rep_rows = (jnp.bitwise_and(iota((ST_PER_CG, P_A), 0), P_A - 1) == iota((ST_PER_CG, P_A), 1)).astype(BF16)
    diag_b = (lax.shift_right_logical(iota((CH_PER_CG, ST_PER_CG), 0), c_bits)
              == lax.shift_right_logical(iota((CH_PER_CG, ST_PER_CG), 1), p_bits))
    diag_c = (lax.shift_right_logical(iota((ST_PER_CG, CH_PER_CG), 0), p_bits)
              == lax.shift_right_logical(iota((ST_PER_CG, CH_PER_CG), 1), c_bits))
    for j in range(N_CG):
        for part, src in enumerate((bre_ref, bim_ref)):
            x = _dot(src[j].astype(BF16), rep_cols)
            bbd_ref[j, :, ST_PER_CG * part:ST_PER_CG * (part + 1)] = jnp.where(diag_b, x, 0.0).astype(BF16)
        for part, src in enumerate((cre_ref, cim_ref)):
            y = lax.dot_general(rep_rows, src[j].astype(BF16), (((1,), (1,)), ((), ())),
                                preferred_element_type=F32)
            cbd_ref[j, ST_PER_CG * part:ST_PER_CG * (part + 1), :] = jnp.where(diag_c, y, 0.0).astype(BF16)


def _block_diag_params(lbr, lbi, bbr, bbi, c_re, cneg):
    slabs = [m.reshape(N_CG, CH_PER_CG, P_A) for m in (bbr, bbi, c_re, cneg)]
    bbd, cbd = pl.pallas_call(
        _block_diag_kernel,
        out_shape=(jax.ShapeDtypeStruct((N_CG, CH_PER_CG, 2 * ST_PER_CG), BF16),
                   jax.ShapeDtypeStruct((N_CG, 2 * ST_PER_CG, CH_PER_CG), BF16)),
        name="s5_block_diag",
    )(*slabs)
    lam = jnp.stack([lbr.reshape(-1), lbi.reshape(-1)])
    return bbd, cbd, lam


def _even_prompt_kernel(x_ref, win_ref, bbd_ref, cbd_ref, lam_ref, d_ref, gluw_ref, glub_ref,
                        sg_ref, sb_ref, sw_ref, sbt_ref, wout_ref, lng_ref, lnb_ref, *rest, n_cast):
    cast_in, rest = rest[:n_cast], rest[n_cast:]
    o_ref, sre_ref, sim_ref = rest[:3]
    cast_out, rest = rest[3:3 + n_cast], rest[3 + n_cast:]
    st_ref, xt_ref, scr_ref, cat_ref, vn_ref, gt_ref = rest

    @pl.when(pl.program_id(0) == 0)
    def _():
        st_ref[...] = jnp.zeros_like(st_ref)

    _cast_chunks(cast_in, cast_out)

    for c in range(D_MODEL // LANES):
        for b in range(BATCH):
            xt_ref.at[c][pl.ds(b, CHUNK, stride=BATCH), :] = x_ref[b, :, LANES * c:LANES * (c + 1)]
    xb = jnp.concatenate([xt_ref[c] for c in range(D_MODEL // LANES)], axis=-1)
    xbf = xb.astype(BF16)

    ua = _dot(xbf, win_ref[:, 0:D_A])
    uab = ua.astype(BF16)
    ys = []
    for j in range(N_CG):
        re_cols = slice(2 * ST_PER_CG * j, 2 * ST_PER_CG * j + ST_PER_CG)
        im_cols = slice(2 * ST_PER_CG * j + ST_PER_CG, 2 * ST_PER_CG * (j + 1))
        scr_ref[...] = _dot(uab[:, CH_PER_CG * j:CH_PER_CG * (j + 1)], bbd_ref[j])
        lr = jnp.broadcast_to(lam_ref[0:1, ST_PER_CG * j:ST_PER_CG * (j + 1)], (BATCH, ST_PER_CG))
        li = jnp.broadcast_to(lam_ref[1:2, ST_PER_CG * j:ST_PER_CG * (j + 1)], (BATCH, ST_PER_CG))

        def step(t, carry, lr=lr, li=li):
            sr, si = carry
            row = pl.multiple_of(t * BATCH, BATCH)
            nr = lr * sr - li * si + scr_ref[pl.ds(row, BATCH), 0:ST_PER_CG]
            ni = lr * si + li * sr + scr_ref[pl.ds(row, BATCH), ST_PER_CG:2 * ST_PER_CG]
            scr_ref[pl.ds(row, BATCH), 0:ST_PER_CG] = nr
            scr_ref[pl.ds(row, BATCH), ST_PER_CG:2 * ST_PER_CG] = ni
            return nr, ni

        sr, si = lax.fori_loop(0, CHUNK, step, (st_ref[:, re_cols], st_ref[:, im_cols]), unroll=4)
        st_ref[:, re_cols] = sr
        st_ref[:, im_cols] = si
        sre_ref[:, ST_PER_CG * j:ST_PER_CG * (j + 1)] = sr
        sim_ref[:, ST_PER_CG * j:ST_PER_CG * (j + 1)] = si
        ys.append(_dot(scr_ref[...].astype(BF16), cbd_ref[j]))
    y = jnp.concatenate(ys, axis=-1) + d_ref[...] * ua
    g = jax.nn.gelu(y, approximate=True)
    ya = g * jax.nn.sigmoid(_dot(g.astype(BF16), gluw_ref[...]) + glub_ref[...])
    cat_ref[:, 0:D_A] = ya.astype(BF16)

    ub = _dot(xbf, win_ref[:, D_A:D_A + D_B])
    vb = _dot(xbf, win_ref[:, D_A + D_B:D_A + 2 * D_B])
    vn = _ln(vb, sg_ref[...], sb_ref[...])
    for h in range(N_HB):
        vn_ref[h] = vn[:, HD_B * h:HD_B * (h + 1)]
    r_id = lax.broadcasted_iota(jnp.int32, (CHUNK, CHUNK), 0)
    c_id = lax.broadcasted_iota(jnp.int32, (CHUNK, CHUNK), 1)
    tril = (c_id <= r_id).astype(F32)
    wm = [(sw_ref[h] * tril).astype(BF16) for h in range(N_HB)]
    for b in range(0, BATCH, 2):
        for h in range(N_HB):
            v2 = jnp.concatenate([vn_ref.at[h][pl.ds(b, CHUNK, stride=BATCH), :],
                                  vn_ref.at[h][pl.ds(b + 1, CHUNK, stride=BATCH), :]], axis=-1)
            g2 = _dot(wm[h], v2.astype(BF16)) + sbt_ref[:, h:h + 1]
            gt_ref.at[h][pl.ds(b, CHUNK, stride=BATCH), :] = g2[:, 0:HD_B]
            gt_ref.at[h][pl.ds(b + 1, CHUNK, stride=BATCH), :] = g2[:, HD_B:]
    gate = jnp.concatenate([gt_ref[h] for h in range(N_HB)], axis=-1)
    cat_ref[:, D_A:] = (ub * gate).astype(BF16)

    out = _dot(cat_ref[...], wout_ref[...])
    o_ref[...] = _ln(ALPHA * xb + out, lng_ref[...], lnb_ref[...])


def _even_prompt(x, e, l, win, bbd, cbd, lam, d3, gluw, glub3, sg3, sb3, sw, sbt, wout, lng3, lnb3, *, cast_jobs=()):
    rows = ROWS_PER_CHUNK
    n = SEQ // CHUNK
    c_in, c_out, c_shapes = _cast_specs(cast_jobs, n)
    in_specs = [pl.BlockSpec((BATCH, CHUNK, D_MODEL), lambda i: (0, i, 0)),
                _layer(win, e), _whole(bbd), _whole(cbd), _whole(lam), _layer(d3, e), _layer(gluw, e),
                _layer(glub3, e), _layer(sg3, e), _layer(sb3, e), _layer(sw, e), _layer(sbt, e),
                _layer(wout, e), _layer(lng3, l), _layer(lnb3, l)] + c_in
    st = jax.ShapeDtypeStruct((BATCH, N_GA * P_A), F32)
    st_spec = pl.BlockSpec((BATCH, N_GA * P_A), lambda i: (0, 0))
    outs = pl.pallas_call(
        functools.partial(_even_prompt_kernel, n_cast=len(cast_jobs)),
        grid=(n,),
        in_specs=in_specs,
        out_specs=(pl.BlockSpec((rows, D_MODEL), lambda i: (i, 0)), st_spec, st_spec, *c_out),
        out_shape=(jax.ShapeDtypeStruct((M_PROMPT, D_MODEL), F32), st, st, *c_shapes),
        scratch_shapes=[pltpu.VMEM((BATCH, 2 * N_GA * P_A), F32),
                        pltpu.VMEM((D_MODEL // LANES, rows, LANES), F32),
                        pltpu.VMEM((rows, 2 * ST_PER_CG), F32),
                        pltpu.VMEM((rows, D_MODEL), BF16),
                        pltpu.VMEM((N_HB, rows, HD_B), F32),
                        pltpu.VMEM((N_HB, rows, HD_B), F32)],
        compiler_params=_params(),
        name="even_prompt",
    )(x, win, bbd, cbd, lam, d3, gluw, glub3, sg3, sb3, sw, sbt, wout, lng3, lnb3, *[w for w, _ in cast_jobs])
    return outs[:3], outs[3:]


def _even_sample_kernel(x_ref, s0r_ref, s0i_ref, win_ref, bbd_ref, cbd_ref, lam_ref, d_ref,
                        gluw_ref, glub_ref, sg_ref, sb_ref, sw0_ref, sb0_ref, wout_ref, lng_ref, lnb_ref,
                        o_ref, sre_ref, sim_ref, vn_ref):
    xb = x_ref[...]
    xbf = xb.astype(BF16)
    ua = _dot(xbf, win_ref[:, 0:D_A])
    uab = ua.astype(BF16)
    ys = []
    for j in range(N_CG):
        cols = slice(ST_PER_CG * j, ST_PER_CG * (j + 1))
        bu = _dot(uab[:, CH_PER_CG * j:CH_PER_CG * (j + 1)], bbd_ref[j])
        lr = lam_ref[0:1, cols]
        li = lam_ref[1:2, cols]
        s0r = s0r_ref[:, cols]
        s0i = s0i_ref[:, cols]
        sr = lr * s0r - li * s0i + bu[:, 0:ST_PER_CG]
        si = lr * s0i + li * s0r + bu[:, ST_PER_CG:]
        sre_ref[:, cols] = sr
        sim_ref[:, cols] = si
        ys.append(_dot(jnp.concatenate([sr, si], axis=-1).astype(BF16), cbd_ref[j]))
    y = jnp.concatenate(ys, axis=-1) + d_ref[...] * ua
    g = jax.nn.gelu(y, approximate=True)
    ya = g * jax.nn.sigmoid(_dot(g.astype(BF16), gluw_ref[...]) + glub_ref[...])

    ub = _dot(xbf, win_ref[:, D_A:D_A + D_B])
    vb = _dot(xbf, win_ref[:, D_A + D_B:D_A + 2 * D_B])
    vn = _ln(vb, sg_ref[...], sb_ref[...])
    vn_ref[...] = vn
    yb = ub * (sw0_ref[...] * vn + sb0_ref[...])

    cat = jnp.concatenate([ya, yb], axis=-1).astype(BF16)
    out = _dot(cat, wout_ref[...])
    o_ref[...] = _ln(ALPHA * xb + out, lng_ref[...], lnb_ref[...])


def _even_sample(x, s0r, s0i, e, l, win, bbd, cbd, lam, d3, gluw, glub3, sg3, sb3, sw0, sb0, wout, lng3, lnb3):
    in_specs = [_whole(x), _whole(s0r), _whole(s0i),
                _layer(win, e), _whole(bbd), _whole(cbd), _whole(lam), _layer(d3, e), _layer(gluw, e),
                _layer(glub3, e), _layer(sg3, e), _layer(sb3, e), _whole(sw0), _whole(sb0),
                _layer(wout, e), _layer(lng3, l), _layer(lnb3, l)]
    shapes = ((DEC_BATCH, D_MODEL), (DEC_BATCH, N_GA * P_A), (DEC_BATCH, N_GA * P_A), (DEC_BATCH, D_B))
    return pl.pallas_call(
        _even_sample_kernel,
        grid=(1,),
        in_specs=in_specs,
        out_specs=tuple(pl.BlockSpec(s, lambda i: (0, 0)) for s in shapes),
        out_shape=tuple(jax.ShapeDtypeStruct(s, F32) for s in shapes),
        compiler_params=_params(),
        name="even_sample",
    )(x, s0r, s0i, win, bbd, cbd, lam, d3, gluw, glub3, sg3, sb3, sw0, sb0, wout, lng3, lnb3)


def _store_batch_major(y, o_ref, yt_ref, tm):
    for c in range(D_MODEL // LANES):
        yt_ref[c] = y[:, LANES * c:LANES * (c + 1)]
    for c in range(D_MODEL // LANES):
        for b in range(BATCH):
            o_ref[b, :, LANES * c:LANES * (c + 1)] = yt_ref.at[c][pl.ds(b, tm // BATCH, stride=BATCH), :]


def _ffn_tile(x_ref, gs_ref, rows, shift, wg_ref, wu_ref, cw_ref, cb_ref, wd_ref, lng_ref, lnb_ref):
    hs = (K_F - 1) * shift
    xbf = x_ref[...].astype(BF16)
    acc = None
    for c0, c1 in FF_CHUNKS:
        gs_ref[hs:hs + rows, c0:c1] = _dot(xbf, wg_ref[:, c0:c1])
        conv = cb_ref[:, c0:c1]
        for k in range(K_F):
            conv = conv + cw_ref[k:k + 1, c0:c1] * gs_ref[k * shift:k * shift + rows, c0:c1]
        up = _dot(xbf, wu_ref[:, c0:c1])
        h = (jax.nn.silu(conv) * up).astype(BF16)
        part = _dot(h, wd_ref[c0:c1, :])
        acc = part if acc is None else acc + part
    return _ln(ALPHA * x_ref[...] + acc, lng_ref[...], lnb_ref[...])


def _ffn_kernel(x_ref, xs_ref, halo_s_ref, wg_ref, wu_ref, cw_ref, cb_ref, wd_ref, lng_ref, lnb_ref, *rest,
                tm, n, n_cast, batch_major_out):
    cast_in, rest = rest[:n_cast], rest[n_cast:]
    o_ref, cache_ref, os_ref, cache_s_ref = rest[:4]
    cast_out, rest = rest[4:4 + n_cast], rest[4 + n_cast:]
    gs_ref = rest[0]
    i = pl.program_id(0)
    hs = (K_F - 1) * BATCH
    hs_s = (K_F - 1) * DEC_BATCH
    weights = (wg_ref, wu_ref, cw_ref, cb_ref, wd_ref, lng_ref, lnb_ref)

    @pl.when(i == 0)
    def _():
        gs_ref[0:hs, :] = jnp.zeros((hs, D_FF), F32)

    @pl.when(i < n)
    def _():
        _cast_chunks(cast_in, cast_out)
        y = _ffn_tile(x_ref, gs_ref, tm, BATCH, *weights)
        if batch_major_out:
            _store_batch_major(y, o_ref, rest[1], tm)
        else:
            o_ref[...] = y
        tail = gs_ref[tm:tm + hs, :]
        cache_ref[...] = tail
        gs_ref[0:hs, :] = tail

    @pl.when(i == n)
    def _():
        gs_ref[0:hs_s, :] = halo_s_ref[...]
        os_ref[...] = _ffn_tile(xs_ref, gs_ref, DEC_BATCH, DEC_BATCH, *weights)
        cache_s_ref[...] = gs_ref[DEC_BATCH:DEC_BATCH + hs_s, :]


def _ffn(x, xs, halo_s, l, wg, wu, cw, cb3, wd, lng3, lnb3, *, tm, batch_major_out=False, cast_jobs=()):
    m = x.shape[0]
    n = m // tm
    hs = (K_F - 1) * BATCH
    hs_s = (K_F - 1) * DEC_BATCH
    c_in, c_out, c_shapes = _cast_specs(cast_jobs, n)
    in_specs = [pl.BlockSpec((tm, D_MODEL), lambda i: (jnp.minimum(i, n - 1), 0)), _whole(xs), _whole(halo_s),
                _whole(wg), _whole(wu), _layer(cw, l), _layer(cb3, l), _whole(wd),
                _layer(lng3, l), _layer(lnb3, l)] + c_in
    scratch = [pltpu.VMEM((max(hs + tm, hs_s + DEC_BATCH), D_FF), F32)]
    if batch_major_out:
        o_spec = pl.BlockSpec((BATCH, tm // BATCH, D_MODEL), lambda i: (0, jnp.minimum(i, n - 1), 0))
        o_shape = jax.ShapeDtypeStruct((BATCH, m // BATCH, D_MODEL), F32)
        scratch.append(pltpu.VMEM((D_MODEL // LANES, tm, LANES), F32))
    else:
        o_spec = pl.BlockSpec((tm, D_MODEL), lambda i: (jnp.minimum(i, n - 1), 0))
        o_shape = jax.ShapeDtypeStruct((m, D_MODEL), F32)
    outs = pl.pallas_call(
        functools.partial(_ffn_kernel, tm=tm, n=n, n_cast=len(cast_jobs), batch_major_out=batch_major_out),
        grid=(n + 1,),
        in_specs=in_specs,
        out_specs=(o_spec, pl.BlockSpec((hs, D_FF), lambda i: (0, 0)),
                   pl.BlockSpec((DEC_BATCH, D_MODEL), lambda i: (0, 0)),
                   pl.BlockSpec((hs_s, D_FF), lambda i: (0, 0)), *c_out),
        out_shape=(o_shape, jax.ShapeDtypeStruct((hs, D_FF), F32),
                   jax.ShapeDtypeStruct((DEC_BATCH, D_MODEL), F32),
                   jax.ShapeDtypeStruct((hs_s, D_FF), F32), *c_shapes),
        scratch_shapes=scratch,
        compiler_params=_params(),
        name="conv_ffn",
    )(x, xs, halo_s, wg, wu, cw, cb3, wd, lng3, lnb3, *[w for w, _ in cast_jobs])
    return outs[:4], outs[4:]


def _odd_prompt_kernel(x_ref, win_ref, cw_ref, cb_ref, lcg_ref, lcb_ref, wout_ref, lng_ref, lnb_ref,
                       o_ref, cache_ref, gs_ref, hc_ref, *, tm):
    hs = (K_C - 1) * BATCH

    @pl.when(pl.program_id(0) == 0)
    def _():
        gs_ref[0:hs, :] = jnp.zeros((hs, D_C), F32)

    xb = x_ref[...]
    xbf = xb.astype(BF16)
    z1 = _dot(xbf, win_ref[:, 0:D_C])
    z2 = _dot(xbf, win_ref[:, D_C:2 * D_C])
    gs_ref[hs:hs + tm, :] = z1 * jax.nn.sigmoid(z2)

    def conv_rows(i, _):
        r0 = pl.multiple_of(i * CONV_ROWS, CONV_ROWS)
        for l0 in range(0, D_C, LANES):
            cols = slice(l0, l0 + LANES)
            win = gs_ref[pl.ds(r0, CONV_ROWS + hs), cols]
            acc = jnp.broadcast_to(cb_ref[:, cols], (CONV_ROWS, LANES))
            for k in range(K_C):
                acc = acc + cw_ref[k:k + 1, cols] * win[k * BATCH:k * BATCH + CONV_ROWS, :]
            hc_ref[pl.ds(r0, CONV_ROWS), cols] = acc
        return 0

    lax.fori_loop(0, tm // CONV_ROWS, conv_rows, 0)
    h = jax.nn.silu(_ln(hc_ref[...], lcg_ref[...], lcb_ref[...]))
    out = _dot(h.astype(BF16), wout_ref[...])
    o_ref[...] = _ln(ALPHA * xb + out, lng_ref[...], lnb_ref[...])
    tail = gs_ref[tm:tm + hs, :]
    cache_ref[...] = tail
    gs_ref[0:hs, :] = tail


def _odd_prompt(x, o, l, win, cw, cb3, lcg3, lcb3, wout, lng3, lnb3, *, tm):
    hs = (K_C - 1) * BATCH
    in_specs = [pl.BlockSpec((tm, D_MODEL), lambda i: (i, 0)),
                _whole(win), _layer(cw, o), _layer(cb3, o), _layer(lcg3, o), _layer(lcb3, o),
                _whole(wout), _layer(lng3, l), _layer(lnb3, l)]
    return pl.pallas_call(
        functools.partial(_odd_prompt_kernel, tm=tm),
        grid=(M_PROMPT // tm,),
        in_specs=in_specs,
        out_specs=(pl.BlockSpec((tm, D_MODEL), lambda i: (i, 0)), pl.BlockSpec((hs, D_C), lambda i: (0, 0))),
        out_shape=(jax.ShapeDtypeStruct((M_PROMPT, D_MODEL), F32), jax.ShapeDtypeStruct((hs, D_C), F32)),
        scratch_shapes=[pltpu.VMEM((hs + tm, D_C), F32), pltpu.VMEM((tm, D_C), F32)],
        compiler_params=_params(),
        name="odd_prompt",
    )(x, win, cw, cb3, lcg3, lcb3, wout, lng3, lnb3)


def _odd_sample_kernel(x_ref, c_ref, cn_ref, win_ref, cw_ref, cb_ref, lcg_ref, lcb_ref, wout_ref, lng_ref, lnb_ref,
                       o_ref, oc_ref, g_ref, pc_ref):
    k = pl.program_id(0)
    last = (K_C - 1) // SAMPLE_TAPS - 1

    @pl.when(k == 0)
    def _():
        xbf = x_ref[...].astype(BF16)
        z1 = _dot(xbf, win_ref[:, 0:D_C])
        z2 = _dot(xbf, win_ref[:, D_C:2 * D_C])
        g_ref[...] = z1 * jax.nn.sigmoid(z2)
        pc_ref[...] = jnp.broadcast_to(cb_ref[...], pc_ref.shape)

    pc = pc_ref[...]
    for t in range(SAMPLE_TAPS):
        pc = pc + cw_ref[pl.ds(k * SAMPLE_TAPS + t, 1), :] * c_ref[t]
    pc_ref[...] = pc
    oc_ref[0:SAMPLE_TAPS - 1] = c_ref[1:SAMPLE_TAPS]

    @pl.when(k < last)
    def _():
        oc_ref[SAMPLE_TAPS - 1] = cn_ref[...]

    @pl.when(k == last)
    def _():
        g = g_ref[...]
        oc_ref[SAMPLE_TAPS - 1] = g
        hc = pc + cw_ref[K_C - 1:K_C, :] * g
        h = jax.nn.silu(_ln(hc, lcg_ref[...], lcb_ref[...]))
        out = _dot(h.astype(BF16), wout_ref[...])
        o_ref[...] = _ln(ALPHA * x_ref[...] + out, lng_ref[...], lnb_ref[...])


def _odd_sample(x, cache_t, o, l, win, cw, cb3, lcg3, lcb3, wout, lng3, lnb3):
    nk = K_C - 1
    steps = nk // SAMPLE_TAPS
    in_specs = [_whole(x),
                pl.BlockSpec((None, SAMPLE_TAPS, DEC_BATCH, D_C), lambda k: (o, k, 0, 0)),
                pl.BlockSpec((None, None, DEC_BATCH, D_C),
                             lambda k: (o, jnp.minimum((k + 1) * SAMPLE_TAPS, nk - 1), 0, 0)),
                _whole(win), _layer(cw, o), _layer(cb3, o), _layer(lcg3, o), _layer(lcb3, o),
                _whole(wout), _layer(lng3, l), _layer(lnb3, l)]
    return pl.pallas_call(
        _odd_sample_kernel,
        grid=(steps,),
        in_specs=in_specs,
        out_specs=(pl.BlockSpec((DEC_BATCH, D_MODEL), lambda k: (0, 0)),
                   pl.BlockSpec((SAMPLE_TAPS, DEC_BATCH, D_C), lambda k: (k, 0, 0))),
        out_shape=(jax.ShapeDtypeStruct((DEC_BATCH, D_MODEL), F32),
                   jax.ShapeDtypeStruct((nk, DEC_BATCH, D_C), F32)),
        scratch_shapes=[pltpu.VMEM((DEC_BATCH, D_C), F32), pltpu.VMEM((DEC_BATCH, D_C), F32)],
        compiler_params=_params(),
        name="odd_sample",
    )(x, cache_t, cache_t, win, cw, cb3, lcg3, lcb3, wout, lng3, lnb3)


def kernel(x_prompt, x_sample, state_a_re, state_a_im, cache_c_conv, cache_ffn_conv, w_in_ab, s5_lam_re, s5_lam_im, s5_log_dt, s5_b_re, s5_b_im, s5_c_re, s5_c_im, s5_d, s5_glu_w, s5_glu_b, sgu_ln_g, sgu_ln_b, sgu_w, sgu_b, w_out_ab, w_in_c, conv_c_w, conv_c_b, ln_c_g, ln_c_b, w_out_c, ffn_w_gate, ffn_w_up, ffn_conv_w, ffn_conv_b, ffn_w_down, ln_mix_g, ln_mix_b, ln_ffn_g, ln_ffn_b):
    xs = x_sample.reshape(DEC_BATCH, D_MODEL)
    xp = x_prompt

    assert N_EVEN == 1, "side-stream casts below are laid out for one even layer followed by odd layers"
    win_ab, wout_ab, gluw = w_in_ab.astype(BF16), w_out_ab.astype(BF16), s5_glu_w.astype(BF16)
    cache_c_t = jnp.transpose(cache_c_conv, (0, 2, 1, 3))
    fcb3 = _rows3(ffn_conv_b)
    d3, glub3, sg3, sb3 = _rows3(s5_d), _rows3(s5_glu_b), _rows3(sgu_ln_g), _rows3(sgu_ln_b)
    sbt = jnp.swapaxes(sgu_b, 1, 2)
    ccb3, lcg3, lcb3 = _rows3(conv_c_b), _rows3(ln_c_g), _rows3(ln_c_b)
    lmg3, lmb3, lfg3, lfb3 = _rows3(ln_mix_g), _rows3(ln_mix_b), _rows3(ln_ffn_g), _rows3(ln_ffn_b)

    def ffn_jobs(l):
        return [(ffn_w_gate, l), (ffn_w_up, l), (ffn_w_down, l)]

    sa_re_p, sa_im_p, sa_re_s, sa_im_s, sb_v_s = [], [], [], [], []
    cc_p, cc_s, cf_p, cf_s = [], [], [], []
    for l in range(DEPTH):
        if l % 2 == 0:
            e = l // 2
            lbr, lbi, bbr, bbi, cneg = _s5_prep(
                s5_lam_re[e], s5_lam_im[e], s5_log_dt[e],
                jnp.swapaxes(s5_b_re[e], 1, 2), jnp.swapaxes(s5_b_im[e], 1, 2), s5_c_im[e])
            bbd, cbd, lam = _block_diag_params(lbr, lbi, bbr, bbi, s5_c_re[e], cneg)
            shared = (win_ab, bbd, cbd, lam, d3, gluw, glub3, sg3, sb3)
            (xp, sre, sim), ffn_w = _even_prompt(xp, e, l, *shared, sgu_w, sbt, wout_ab, lmg3, lmb3,
                                                 cast_jobs=ffn_jobs(l))
            sw0 = jnp.repeat(sgu_w[e][:, 0, 0], HD_B).reshape(1, D_B)
            sb0 = jnp.repeat(sgu_b[e][:, 0], HD_B).reshape(1, D_B)
            xs, sres, sims, vn = _even_sample(
                xs, state_a_re[e].reshape(DEC_BATCH, -1), state_a_im[e].reshape(DEC_BATCH, -1),
                e, l, *shared, sw0, sb0, wout_ab, lmg3, lmb3)
            sa_re_p.append(sre.reshape(BATCH, N_GA, P_A))
            sa_im_p.append(sim.reshape(BATCH, N_GA, P_A))
            sa_re_s.append(sres.reshape(DEC_BATCH, N_GA, P_A))
            sa_im_s.append(sims.reshape(DEC_BATCH, N_GA, P_A))
            sb_v_s.append(vn.reshape(DEC_BATCH, 1, D_B))
        else:
            o = l // 2
            win_c, wout_c = odd_w
            rest = (win_c, conv_c_w, ccb3, lcg3, lcb3, wout_c, lmg3, lmb3)
            xp, cp = _odd_prompt(xp, o, l, *rest, tm=ODD_TM)
            xs, cs_t = _odd_sample(xs, cache_c_t, o, l, *rest)
            cc_p.append(jnp.transpose(cp.reshape(K_C - 1, BATCH, D_C), (1, 0, 2)))
            cc_s.append(jnp.transpose(cs_t, (1, 0, 2)))
        wg, wu, wd = ffn_w
        jobs = [(w_in_c, (l + 1) // 2), (w_out_c, (l + 1) // 2)] + ffn_jobs(l + 1) if l + 1 < DEPTH else []
        halo_s = jnp.transpose(cache_ffn_conv[l], (1, 0, 2)).reshape((K_F - 1) * DEC_BATCH, D_FF)
        (xp, fcp, xs, fcs), casts = _ffn(xp, xs, halo_s, l, wg, wu, ffn_conv_w, fcb3, wd, lfg3, lfb3, tm=FFN_TM,
                                         batch_major_out=(l == DEPTH - 1), cast_jobs=jobs)
        if jobs:
            odd_w, ffn_w = casts[:2], casts[2:]
        cf_p.append(jnp.transpose(fcp.reshape(K_F - 1, BATCH, D_FF), (1, 0, 2)))
        cf_s.append(jnp.transpose(fcs.reshape(K_F - 1, DEC_BATCH, D_FF), (1, 0, 2)))

    ys = xs.reshape(DEC_BATCH, 1, D_MODEL)
    return (xp, ys,
            jnp.stack(sa_re_p), jnp.stack(sa_im_p), jnp.stack(sa_re_s), jnp.stack(sa_im_s),
            jnp.stack(sb_v_s),
            jnp.stack(cc_p), jnp.stack(cc_s),
            jnp.stack(cf_p), jnp.stack(cf_s))
```

```python
import functools

import jax
import jax.numpy as jnp
from jax import lax
from jax.experimental import pallas as pl
from jax.experimental.pallas import tpu as pltpu

D_MODEL = 1024
BATCH = 8
SEQ = 2048
DEPTH = 2
DEC_BATCH = 128
N_EVEN = (DEPTH + 1) // 2
N_ODD = DEPTH // 2
D_A = D_MODEL // 2
S5_GROUP = 16
N_GA = D_A // S5_GROUP
P_A = 64
D_B = D_MODEL // 2
N_HB = 4
HD_B = D_B // N_HB
CHUNK = 128
D_C = D_MODEL
K_C = 31
D_FF = ((8 * D_MODEL) // 3 + 127) // 128 * 128
K_F = 3
ALPHA = (2.0 * DEPTH) ** 0.25
LN_EPS = 1e-5

F32 = jnp.float32
BF16 = jnp.bfloat16

LANES = 128
MXU_N = 256
M_PROMPT = BATCH * SEQ
ROWS_PER_CHUNK = CHUNK * BATCH
N_CG = 4
G_PER_CG = N_GA // N_CG
CH_PER_CG = G_PER_CG * S5_GROUP
ST_PER_CG = G_PER_CG * P_A
FF_CHUNKS = ((0, 6 * MXU_N), (6 * MXU_N, D_FF))
FFN_TM = 512
ODD_TM = 1024
CONV_ROWS = 8 * BATCH
SAMPLE_TAPS = 5
N_CAST_CHUNKS = 16
VMEM_LIMIT = 56 * 1024 * 1024


def _ln(x, g, b):
    mu = jnp.mean(x, axis=-1, keepdims=True)
    xc = x - mu
    var = jnp.mean(xc * xc, axis=-1, keepdims=True)
    return xc * lax.rsqrt(var + LN_EPS) * g + b


def _dot(a, b):
    return jnp.dot(a, b, preferred_element_type=F32)


def _whole(arr):
    nd = arr.ndim
    return pl.BlockSpec(arr.shape, lambda *_: (0,) * nd, pipeline_mode=pl.Buffered(1))


def _layer(arr, l):
    nd = arr.ndim
    return pl.BlockSpec((None,) + arr.shape[1:], lambda *_: (l,) + (0,) * (nd - 1),
                        pipeline_mode=pl.Buffered(1))


def _rows3(v):
    return v.reshape(v.shape[0], 1, v.shape[1])


def _params(n_grid=1):
    return pltpu.CompilerParams(dimension_semantics=("arbitrary",) * n_grid,
                                vmem_limit_bytes=VMEM_LIMIT)


def _cast_specs(jobs, n_steps):
    def chunk(i):
        return jnp.minimum(i * N_CAST_CHUNKS // n_steps, N_CAST_CHUNKS - 1)

    in_specs, out_specs, out_shapes = [], [], []
    for w, l in jobs:
        _, r, c = w.shape
        rows = r // N_CAST_CHUNKS
        in_specs.append(pl.BlockSpec((None, rows, c), lambda i, l=l: (l, chunk(i), 0)))
        out_specs.append(pl.BlockSpec((rows, c), lambda i: (chunk(i), 0)))
        out_shapes.append(jax.ShapeDtypeStruct((r, c), BF16))
    return in_specs, out_specs, out_shapes


def _cast_chunks(in_refs, out_refs):
    for w_ref, o_ref in zip(in_refs, out_refs):
        o_ref[...] = w_ref[...].astype(BF16)


def _s5_prep_kernel(lr_ref, li_ref, ldt_ref, br_ref, bi_ref, ci_ref,
                    lbr_ref, lbi_ref, bbr_ref, bbi_ref, cneg_ref):
    lr = lr_ref[...]
    li = li_ref[...]
    dt = jnp.exp(ldt_ref[...])
    mag = jnp.exp(lr * dt)
    lbr = mag * jnp.cos(li * dt)
    lbi = mag * jnp.sin(li * dt)
    lbr_ref[...] = lbr
    lbi_ref[...] = lbi
    nr = lbr - 1.0
    ni = lbi
    den = lr * lr + li * li
    qr = (nr * lr + ni * li) / den
    qi = (ni * lr - nr * li) / den
    br = br_ref[...]
    bi = bi_ref[...]
    qr3 = qr[:, None, :]
    qi3 = qi[:, None, :]
    bbr_ref[...] = qr3 * br - qi3 * bi
    bbi_ref[...] = qr3 * bi + qi3 * br
    cneg_ref[...] = -ci_ref[...]


def _s5_prep(lam_re, lam_im, log_dt, b_re, b_im, c_im):
    gp = jax.ShapeDtypeStruct((N_GA, P_A), F32)
    gcp = jax.ShapeDtypeStruct((N_GA, S5_GROUP, P_A), F32)
    return pl.pallas_call(
        _s5_prep_kernel,
        out_shape=(gp, gp, gcp, gcp, gcp),
        name="s5_prep",
    )(lam_re, lam_im, log_dt.reshape(N_GA, 1), b_re, b_im, c_im)


def _block_diag_kernel(bre_ref, bim_ref, cre_ref, cim_ref, bbd_ref, cbd_ref):
    def iota(shape, axis):
        return lax.broadcasted_iota(jnp.int32, shape, axis)

    p_bits = P_A.bit_length() - 1
    c_bits = S5_GROUP.bit_length() - 1
    rep_cols = (jnp.bitwise_and(iota((P_A, ST_PER_CG), 1), P_A - 1) == iota((P_A, ST_PER_CG), 0)).astype(BF16)
    rep_rows = (jnp.bitwise_and(iota((ST_PER_CG, P_A), 0), P_A - 1) == iota((ST_PER_CG, P_A), 1)).astype(BF16)
    diag_b = (lax.shift_right_logical(iota((CH_PER_CG, ST_PER_CG), 0), c_bits)
              == lax.shift_right_logical(iota((CH_PER_CG, ST_PER_CG), 1), p_bits))
    diag_c = (lax.shift_right_logical(iota((ST_PER_CG, CH_PER_CG), 0), p_bits)
              == lax.shift_right_logical(iota((ST_PER_CG, CH_PER_CG), 1), c_bits))
    for j in range(N_CG):
        for part, src in enumerate((bre_ref, bim_ref)):
            x = _dot(src[j].astype(BF16), rep_cols)
            bbd_ref[j, :, ST_PER_CG * part:ST_PER_CG * (part + 1)] = jnp.where(diag_b, x, 0.0).astype(BF16)
        for part, src in enumerate((cre_ref, cim_ref)):
            y = lax.dot_general(rep_rows, src[j].astype(BF16), (((1,), (1,)), ((), ())),
                                preferred_element_type=F32)
            cbd_ref[j, ST_PER_CG * part:ST_PER_CG * (part + 1), :] = jnp.where(diag_c, y, 0.0).astype(BF16)


def _block_diag_params(lbr, lbi, bbr, bbi, c_re, cneg):
    slabs = [m.reshape(N_CG, CH_PER_CG, P_A) for m in (bbr, bbi, c_re, cneg)]
    bbd, cbd = pl.pallas_call(
        _block_diag_kernel,
        out_shape=(jax.ShapeDtypeStruct((N_CG, CH_PER_CG, 2 * ST_PER_CG), BF16),
                   jax.ShapeDtypeStruct((N_CG, 2 * ST_PER_CG, CH_PER_CG), BF16)),
        name="s5_block_diag",
    )(*slabs)
    lam = jnp.stack([lbr.reshape(-1), lbi.reshape(-1)])
    return bbd, cbd, lam


def _even_prompt_kernel(x_ref, win_ref, bbd_ref, cbd_ref, lam_ref, d_ref, gluw_ref, glub_ref,
                        sg_ref, sb_ref, sw_ref, sbt_ref, wout_ref, lng_ref, lnb_ref, *rest, n_cast):
    cast_in, rest = rest[:n_cast], rest[n_cast:]
    o_ref, sre_ref, sim_ref = rest[:3]
    cast_out, rest = rest[3:3 + n_cast], rest[3 + n_cast:]
    st_ref, xt_ref, scr_ref, cat_ref, vn_ref, gt_ref = rest

    @pl.when(pl.program_id(0) == 0)
    def _():
        st_ref[...] = jnp.zeros_like(st_ref)

    _cast_chunks(cast_in, cast_out)

    for c in range(D_MODEL // LANES):
        for b in range(BATCH):
            xt_ref.at[c][pl.ds(b, CHUNK, stride=BATCH), :] = x_ref[b, :, LANES * c:LANES * (c + 1)]
    xb = jnp.concatenate([xt_ref[c] for c in range(D_MODEL // LANES)], axis=-1)
    xbf = xb.astype(BF16)

    ua = _dot(xbf, win_ref[:, 0:D_A])
    uab = ua.astype(BF16)
    ys = []
    for j in range(N_CG):
        re_cols = slice(2 * ST_PER_CG * j, 2 * ST_PER_CG * j + ST_PER_CG)
        im_cols = slice(2 * ST_PER_CG * j + ST_PER_CG, 2 * ST_PER_CG * (j + 1))
        scr_ref[...] = _dot(uab[:, CH_PER_CG * j:CH_PER_CG * (j + 1)], bbd_ref[j])
        lr = jnp.broadcast_to(lam_ref[0:1, ST_PER_CG * j:ST_PER_CG * (j + 1)], (BATCH, ST_PER_CG))
        li = jnp.broadcast_to(lam_ref[1:2, ST_PER_CG * j:ST_PER_CG * (j + 1)], (BATCH, ST_PER_CG))

        def step(t, carry, lr=lr, li=li):
            sr, si = carry
            row = pl.multiple_of(t * BATCH, BATCH)
            nr = lr * sr - li * si + scr_ref[pl.ds(row, BATCH), 0:ST_PER_CG]
            ni = lr * si + li * sr + scr_ref[pl.ds(row, BATCH), ST_PER_CG:2 * ST_PER_CG]
            scr_ref[pl.ds(row, BATCH), 0:ST_PER_CG] = nr
            scr_ref[pl.ds(row, BATCH), ST_PER_CG:2 * ST_PER_CG] = ni
            return nr, ni

        sr, si = lax.fori_loop(0, CHUNK, step, (st_ref[:, re_cols], st_ref[:, im_cols]), unroll=4)
        st_ref[:, re_cols] = sr
        st_ref[:, im_cols] = si
        sre_ref[:, ST_PER_CG * j:ST_PER_CG * (j + 1)] = sr
        sim_ref[:, ST_PER_CG * j:ST_PER_CG * (j + 1)] = si
        ys.append(_dot(scr_ref[...].astype(BF16), cbd_ref[j]))
    y = jnp.concatenate(ys, axis=-1) + d_ref[...] * ua
    g = jax.nn.gelu(y, approximate=True)
    ya = g * jax.nn.sigmoid(_dot(g.astype(BF16), gluw_ref[...]) + glub_ref[...])
    cat_ref[:, 0:D_A] = ya.astype(BF16)

    ub = _dot(xbf, win_ref[:, D_A:D_A + D_B])
    vb = _dot(xbf, win_ref[:, D_A + D_B:D_A + 2 * D_B])
    vn = _ln(vb, sg_ref[...], sb_ref[...])
    for h in range(N_HB):
        vn_ref[h] = vn[:, HD_B * h:HD_B * (h + 1)]
    r_id = lax.broadcasted_iota(jnp.int32, (CHUNK, CHUNK), 0)
    c_id = lax.broadcasted_iota(jnp.int32, (CHUNK, CHUNK), 1)
    tril = (c_id <= r_id).astype(F32)
    wm = [(sw_ref[h] * tril).astype(BF16) for h in range(N_HB)]
    for b in range(0, BATCH, 2):
        for h in range(N_HB):
            v2 = jnp.concatenate([vn_ref.at[h][pl.ds(b, CHUNK, stride=BATCH), :],
                                  vn_ref.at[h][pl.ds(b + 1, CHUNK, stride=BATCH), :]], axis=-1)
            g2 = _dot(wm[h], v2.astype(BF16)) + sbt_ref[:, h:h + 1]
            gt_ref.at[h][pl.ds(b, CHUNK, stride=BATCH), :] = g2[:, 0:HD_B]
            gt_ref.at[h][pl.ds(b + 1, CHUNK, stride=BATCH), :] = g2[:, HD_B:]
    gate = jnp.concatenate([gt_ref[h] for h in range(N_HB)], axis=-1)
    cat_ref[:, D_A:] = (ub * gate).astype(BF16)

    out = _dot(cat_ref[...], wout_ref[...])
    xres = jnp.concatenate([xt_ref[c] for c in range(D_MODEL // LANES)], axis=-1)
    o_ref[...] = _ln(ALPHA * xres + out, lng_ref[...], lnb_ref[...])


def _even_prompt(x, e, l, win, bbd, cbd, lam, d3, gluw, glub3, sg3, sb3, sw, sbt, wout, lng3, lnb3, *, cast_jobs=()):
    rows = ROWS_PER_CHUNK
    n = SEQ // CHUNK
    c_in, c_out, c_shapes = _cast_specs(cast_jobs, n)
    in_specs = [pl.BlockSpec((BATCH, CHUNK, D_MODEL), lambda i: (0, i, 0)),
                _layer(win, e), _whole(bbd), _whole(cbd), _whole(lam), _layer(d3, e), _layer(gluw, e),
                _layer(glub3, e), _layer(sg3, e), _layer(sb3, e), _layer(sw, e), _layer(sbt, e),
                _layer(wout, e), _layer(lng3, l), _layer(lnb3, l)] + c_in
    st = jax.ShapeDtypeStruct((BATCH, N_GA * P_A), F32)
    st_spec = pl.BlockSpec((BATCH, N_GA * P_A), lambda i: (0, 0))
    outs = pl.pallas_call(
        functools.partial(_even_prompt_kernel, n_cast=len(cast_jobs)),
        grid=(n,),
        in_specs=in_specs,
        out_specs=(pl.BlockSpec((rows, D_MODEL), lambda i: (i, 0)), st_spec, st_spec, *c_out),
        out_shape=(jax.ShapeDtypeStruct((M_PROMPT, D_MODEL), F32), st, st, *c_shapes),
        scratch_shapes=[pltpu.VMEM((BATCH, 2 * N_GA * P_A), F32),
                        pltpu.VMEM((D_MODEL // LANES, rows, LANES), F32),
                        pltpu.VMEM((rows, 2 * ST_PER_CG), F32),
                        pltpu.VMEM((rows, D_MODEL), BF16),
                        pltpu.VMEM((N_HB, rows, HD_B), F32),
                        pltpu.VMEM((N_HB, rows, HD_B), F32)],
        compiler_params=_params(),
        name="even_prompt",
    )(x, win, bbd, cbd, lam, d3, gluw, glub3, sg3, sb3, sw, sbt, wout, lng3, lnb3, *[w for w, _ in cast_jobs])
    return outs[:3], outs[3:]


def _even_sample_kernel(x_ref, s0r_ref, s0i_ref, win_ref, bbd_ref, cbd_ref, lam_ref, d_ref,
                        gluw_ref, glub_ref, sg_ref, sb_ref, sw0_ref, sb0_ref, wout_ref, lng_ref, lnb_ref,
                        o_ref, sre_ref, sim_ref, vn_ref):
    xb = x_ref[...]
    xbf = xb.astype(BF16)
    ua = _dot(xbf, win_ref[:, 0:D_A])
    uab = ua.astype(BF16)
    ys = []
    for j in range(N_CG):
        cols = slice(ST_PER_CG * j, ST_PER_CG * (j + 1))
        bu = _dot(uab[:, CH_PER_CG * j:CH_PER_CG * (j + 1)], bbd_ref[j])
        lr = lam_ref[0:1, cols]
        li = lam_ref[1:2, cols]
        s0r = s0r_ref[:, cols]
        s0i = s0i_ref[:, cols]
        sr = lr * s0r - li * s0i + bu[:, 0:ST_PER_CG]
        si = lr * s0i + li * s0r + bu[:, ST_PER_CG:]
        sre_ref[:, cols] = sr
        sim_ref[:, cols] = si
        ys.append(_dot(jnp.concatenate([sr, si], axis=-1).astype(BF16), cbd_ref[j]))
    y = jnp.concatenate(ys, axis=-1) + d_ref[...] * ua
    g = jax.nn.gelu(y, approximate=True)
    ya = g * jax.nn.sigmoid(_dot(g.astype(BF16), gluw_ref[...]) + glub_ref[...])

    ub = _dot(xbf, win_ref[:, D_A:D_A + D_B])
    vb = _dot(xbf, win_ref[:, D_A + D_B:D_A + 2 * D_B])
    vn = _ln(vb, sg_ref[...], sb_ref[...])
    vn_ref[...] = vn
    yb = ub * (sw0_ref[...] * vn + sb0_ref[...])

    cat = jnp.concatenate([ya, yb], axis=-1).astype(BF16)
    out = _dot(cat, wout_ref[...])
    o_ref[...] = _ln(ALPHA * xb + out, lng_ref[...], lnb_ref[...])


def _even_sample(x, s0r, s0i, e, l, win, bbd, cbd, lam, d3, gluw, glub3, sg3, sb3, sw0, sb0, wout, lng3, lnb3):
    in_specs = [_whole(x), _whole(s0r), _whole(s0i),
                _layer(win, e), _whole(bbd), _whole(cbd), _whole(lam), _layer(d3, e), _layer(gluw, e),
                _layer(glub3, e), _layer(sg3, e), _layer(sb3, e), _whole(sw0), _whole(sb0),
                _layer(wout, e), _layer(lng3, l), _layer(lnb3, l)]
    shapes = ((DEC_BATCH, D_MODEL), (DEC_BATCH, N_GA * P_A), (DEC_BATCH, N_GA * P_A), (DEC_BATCH, D_B))
    return pl.pallas_call(
        _even_sample_kernel,
        grid=(1,),
        in_specs=in_specs,
        out_specs=tuple(pl.BlockSpec(s, lambda i: (0, 0)) for s in shapes),
        out_shape=tuple(jax.ShapeDtypeStruct(s, F32) for s in shapes),
        compiler_params=_params(),
        name="even_sample",
    )(x, s0r, s0i, win, bbd, cbd, lam, d3, gluw, glub3, sg3, sb3, sw0, sb0, wout, lng3, lnb3)


def _store_batch_major(y, o_ref, yt_ref, tm):
    for c in range(D_MODEL // LANES):
        yt_ref[c] = y[:, LANES * c:LANES * (c + 1)]
    for c in range(D_MODEL // LANES):
        for b in range(BATCH):
            o_ref[b, :, LANES * c:LANES * (c + 1)] = yt_ref.at[c][pl.ds(b, tm // BATCH, stride=BATCH), :]


def _ffn_tile(x_ref, gs_ref, rows, shift, wg_ref, wu_ref, cw_ref, cb_ref, wd_ref, lng_ref, lnb_ref):
    hs = (K_F - 1) * shift
    xbf = x_ref[...].astype(BF16)
    acc = None
    for c0, c1 in FF_CHUNKS:
        gs_ref[hs:hs + rows, c0:c1] = _dot(xbf, wg_ref[:, c0:c1])
        conv = cb_ref[:, c0:c1]
        for k in range(K_F):
            conv = conv + cw_ref[k:k + 1, c0:c1] * gs_ref[k * shift:k * shift + rows, c0:c1]
        up = _dot(xbf, wu_ref[:, c0:c1])
        h = (jax.nn.silu(conv) * up).astype(BF16)
        part = _dot(h, wd_ref[c0:c1, :])
        acc = part if acc is None else acc + part
    return _ln(ALPHA * x_ref[...] + acc, lng_ref[...], lnb_ref[...])


def _ffn_kernel(x_ref, xs_ref, halo_s_ref, wg_ref, wu_ref, cw_ref, cb_ref, wd_ref, lng_ref, lnb_ref, *rest,
                tm, n, n_cast, batch_major_out):
    cast_in, rest = rest[:n_cast], rest[n_cast:]
    o_ref, cache_ref, os_ref, cache_s_ref = rest[:4]
    cast_out, rest = rest[4:4 + n_cast], rest[4 + n_cast:]
    gs_ref = rest[0]
    i = pl.program_id(0)
    hs = (K_F - 1) * BATCH
    hs_s = (K_F - 1) * DEC_BATCH
    weights = (wg_ref, wu_ref, cw_ref, cb_ref, wd_ref, lng_ref, lnb_ref)

    @pl.when(i == 0)
    def _():
        gs_ref[0:hs, :] = jnp.zeros((hs, D_FF), F32)

    @pl.when(i < n)
    def _():
        _cast_chunks(cast_in, cast_out)
        y = _ffn_tile(x_ref, gs_ref, tm, BATCH, *weights)
        if batch_major_out:
            _store_batch_major(y, o_ref, rest[1], tm)
        else:
            o_ref[...] = y
        tail = gs_ref[tm:tm + hs, :]
        cache_ref[...] = tail
        gs_ref[0:hs, :] = tail

    @pl.when(i == n)
    def _():
        gs_ref[0:hs_s, :] = halo_s_ref[...]
        os_ref[...] = _ffn_tile(xs_ref, gs_ref, DEC_BATCH, DEC_BATCH, *weights)
        cache_s_ref[...] = gs_ref[DEC_BATCH:DEC_BATCH + hs_s, :]


def _ffn(x, xs, halo_s, l, wg, wu, cw, cb3, wd, lng3, lnb3, *, tm, batch_major_out=False, cast_jobs=()):
    m = x.shape[0]
    n = m // tm
    hs = (K_F - 1) * BATCH
    hs_s = (K_F - 1) * DEC_BATCH
    c_in, c_out, c_shapes = _cast_specs(cast_jobs, n)
    in_specs = [pl.BlockSpec((tm, D_MODEL), lambda i: (jnp.minimum(i, n - 1), 0)), _whole(xs), _whole(halo_s),
                _whole(wg), _whole(wu), _layer(cw, l), _layer(cb3, l), _whole(wd),
                _layer(lng3, l), _layer(lnb3, l)] + c_in
    scratch = [pltpu.VMEM((max(hs + tm, hs_s + DEC_BATCH), D_FF), F32)]
    if batch_major_out:
        o_spec = pl.BlockSpec((BATCH, tm // BATCH, D_MODEL), lambda i: (0, jnp.minimum(i, n - 1), 0))
        o_shape = jax.ShapeDtypeStruct((BATCH, m // BATCH, D_MODEL), F32)
        scratch.append(pltpu.VMEM((D_MODEL // LANES, tm, LANES), F32))
    else:
        o_spec = pl.BlockSpec((tm, D_MODEL), lambda i: (jnp.minimum(i, n - 1), 0))
        o_shape = jax.ShapeDtypeStruct((m, D_MODEL), F32)
    outs = pl.pallas_call(
        functools.partial(_ffn_kernel, tm=tm, n=n, n_cast=len(cast_jobs), batch_major_out=batch_major_out),
        grid=(n + 1,),
        in_specs=in_specs,
        out_specs=(o_spec, pl.BlockSpec((hs, D_FF), lambda i: (0, 0)),
                   pl.BlockSpec((DEC_BATCH, D_MODEL), lambda i: (0, 0)),
                   pl.BlockSpec((hs_s, D_FF), lambda i: (0, 0)), *c_out),
        out_shape=(o_shape, jax.ShapeDtypeStruct((hs, D_FF), F32),
                   jax.ShapeDtypeStruct((DEC_BATCH, D_MODEL), F32),
                   jax.ShapeDtypeStruct((hs_s, D_FF), F32), *c_shapes),
        scratch_shapes=scratch,
        compiler_params=_params(),
        name="conv_ffn",
    )(x, xs, halo_s, wg, wu, cw, cb3, wd, lng3, lnb3, *[w for w, _ in cast_jobs])
    return outs[:4], outs[4:]


def _odd_prompt_kernel(x_ref, win_ref, cw_ref, cb_ref, lcg_ref, lcb_ref, wout_ref, lng_ref, lnb_ref,
                       o_ref, cache_ref, gs_ref, hc_ref, *, tm):
    hs = (K_C - 1) * BATCH

    @pl.when(pl.program_id(0) == 0)
    def _():
        gs_ref[0:hs, :] = jnp.zeros((hs, D_C), F32)

    xb = x_ref[...]
    xbf = xb.astype(BF16)
    z1 = _dot(xbf, win_ref[:, 0:D_C])
    z2 = _dot(xbf, win_ref[:, D_C:2 * D_C])
    gs_ref[hs:hs + tm, :] = z1 * jax.nn.sigmoid(z2)

    def conv_rows(i, _):
        r0 = pl.multiple_of(i * CONV_ROWS, CONV_ROWS)
        for l0 in range(0, D_C, LANES):
            cols = slice(l0, l0 + LANES)
            win = gs_ref[pl.ds(r0, CONV_ROWS + hs), cols]
            acc = jnp.broadcast_to(cb_ref[:, cols], (CONV_ROWS, LANES))
            for k in range(K_C):
                acc = acc + cw_ref[k:k + 1, cols] * win[k * BATCH:k * BATCH + CONV_ROWS, :]
            hc_ref[pl.ds(r0, CONV_ROWS), cols] = acc
        return 0

    lax.fori_loop(0, tm // CONV_ROWS, conv_rows, 0)
    h = jax.nn.silu(_ln(hc_ref[...], lcg_ref[...], lcb_ref[...]))
    out = _dot(h.astype(BF16), wout_ref[...])
    o_ref[...] = _ln(ALPHA * x_ref[...] + out, lng_ref[...], lnb_ref[...])
    tail = gs_ref[tm:tm + hs, :]
    cache_ref[...] = tail
    gs_ref[0:hs, :] = tail


def _odd_prompt(x, o, l, win, cw, cb3, lcg3, lcb3, wout, lng3, lnb3, *, tm):
    hs = (K_C - 1) * BATCH
    in_specs = [pl.BlockSpec((tm, D_MODEL), lambda i: (i, 0)),
                _whole(win), _layer(cw, o), _layer(cb3, o), _layer(lcg3, o), _layer(lcb3, o),
                _whole(wout), _layer(lng3, l), _layer(lnb3, l)]
    return pl.pallas_call(
        functools.partial(_odd_prompt_kernel, tm=tm),
        grid=(M_PROMPT // tm,),
        in_specs=in_specs,
        out_specs=(pl.BlockSpec((tm, D_MODEL), lambda i: (i, 0)), pl.BlockSpec((hs, D_C), lambda i: (0, 0))),
        out_shape=(jax.ShapeDtypeStruct((M_PROMPT, D_MODEL), F32), jax.ShapeDtypeStruct((hs, D_C), F32)),
        scratch_shapes=[pltpu.VMEM((hs + tm, D_C), F32), pltpu.VMEM((tm, D_C), F32)],
        compiler_params=_params(),
        name="odd_prompt",
    )(x, win, cw, cb3, lcg3, lcb3, wout, lng3, lnb3)


def _odd_sample_kernel(x_ref, c_ref, cn_ref, win_ref, cw_ref, cb_ref, lcg_ref, lcb_ref, wout_ref, lng_ref, lnb_ref,
                       o_ref, oc_ref, g_ref, pc_ref):
    k = pl.program_id(0)
    last = (K_C - 1) // SAMPLE_TAPS - 1

    @pl.when(k == 0)
    def _():
        xbf = x_ref[...].astype(BF16)
        z1 = _dot(xbf, win_ref[:, 0:D_C])
        z2 = _dot(xbf, win_ref[:, D_C:2 * D_C])
        g_ref[...] = z1 * jax.nn.sigmoid(z2)
        pc_ref[...] = jnp.broadcast_to(cb_ref[...], pc_ref.shape)

    pc = pc_ref[...]
    for t in range(SAMPLE_TAPS):
        pc = pc + cw_ref[pl.ds(k * SAMPLE_TAPS + t, 1), :] * c_ref[t]
    pc_ref[...] = pc
    oc_ref[0:SAMPLE_TAPS - 1] = c_ref[1:SAMPLE_TAPS]

    @pl.when(k < last)
    def _():
        oc_ref[SAMPLE_TAPS - 1] = cn_ref[...]

    @pl.when(k == last)
    def _():
        g = g_ref[...]
        oc_ref[SAMPLE_TAPS - 1] = g
        hc = pc + cw_ref[K_C - 1:K_C, :] * g
        h = jax.nn.silu(_ln(hc, lcg_ref[...], lcb_ref[...]))
        out = _dot(h.astype(BF16), wout_ref[...])
        o_ref[...] = _ln(ALPHA * x_ref[...] + out, lng_ref[...], lnb_ref[...])


def _odd_sample(x, cache_t, o, l, win, cw, cb3, lcg3, lcb3, wout, lng3, lnb3):
    nk = K_C - 1
    steps = nk // SAMPLE_TAPS
    in_specs = [_whole(x),
                pl.BlockSpec((None, SAMPLE_TAPS, DEC_BATCH, D_C), lambda k: (o, k, 0, 0)),
                pl.BlockSpec((None, None, DEC_BATCH, D_C),
                             lambda k: (o, jnp.minimum((k + 1) * SAMPLE_TAPS, nk - 1), 0, 0)),
                _whole(win), _layer(cw, o), _layer(cb3, o), _layer(lcg3, o), _layer(lcb3, o),
                _whole(wout), _layer(lng3, l), _layer(lnb3, l)]
    return pl.pallas_call(
        _odd_sample_kernel,
        grid=(steps,),
        in_specs=in_specs,
        out_specs=(pl.BlockSpec((DEC_BATCH, D_MODEL), lambda k: (0, 0)),
                   pl.BlockSpec((SAMPLE_TAPS, DEC_BATCH, D_C), lambda k: (k, 0, 0))),
        out_shape=(jax.ShapeDtypeStruct((DEC_BATCH, D_MODEL), F32),
                   jax.ShapeDtypeStruct((nk, DEC_BATCH, D_C), F32)),
        scratch_shapes=[pltpu.VMEM((DEC_BATCH, D_C), F32), pltpu.VMEM((DEC_BATCH, D_C), F32)],
        compiler_params=_params(),
        name="odd_sample",
    )(x, cache_t, cache_t, win, cw, cb3, lcg3, lcb3, wout, lng3, lnb3)


def kernel(x_prompt, x_sample, state_a_re, state_a_im, cache_c_conv, cache_ffn_conv, w_in_ab, s5_lam_re, s5_lam_im, s5_log_dt, s5_b_re, s5_b_im, s5_c_re, s5_c_im, s5_d, s5_glu_w, s5_glu_b, sgu_ln_g, sgu_ln_b, sgu_w, sgu_b, w_out_ab, w_in_c, conv_c_w, conv_c_b, ln_c_g, ln_c_b, w_out_c, ffn_w_gate, ffn_w_up, ffn_conv_w, ffn_conv_b, ffn_w_down, ln_mix_g, ln_mix_b, ln_ffn_g, ln_ffn_b):
    xs = x_sample.reshape(DEC_BATCH, D_MODEL)
    xp = x_prompt

    assert N_EVEN == 1, "side-stream casts below are laid out for one even layer followed by odd layers"
    win_ab, wout_ab, gluw = w_in_ab.astype(BF16), w_out_ab.astype(BF16), s5_glu_w.astype(BF16)
    cache_c_t = jnp.transpose(cache_c_conv, (0, 2, 1, 3))
    fcb3 = _rows3(ffn_conv_b)
    d3, glub3, sg3, sb3 = _rows3(s5_d), _rows3(s5_glu_b), _rows3(sgu_ln_g), _rows3(sgu_ln_b)
    sbt = jnp.swapaxes(sgu_b, 1, 2)
    ccb3, lcg3, lcb3 = _rows3(conv_c_b), _rows3(ln_c_g), _rows3(ln_c_b)
    lmg3, lmb3, lfg3, lfb3 = _rows3(ln_mix_g), _rows3(ln_mix_b), _rows3(ln_ffn_g), _rows3(ln_ffn_b)

    def ffn_jobs(l):
        return [(ffn_w_gate, l), (ffn_w_up, l), (ffn_w_down, l)]

    sa_re_p, sa_im_p, sa_re_s, sa_im_s, sb_v_s = [], [], [], [], []
    cc_p, cc_s, cf_p, cf_s = [], [], [], []
    for l in range(DEPTH):
        if l % 2 == 0:
            e = l // 2
            lbr, lbi, bbr, bbi, cneg = _s5_prep(
                s5_lam_re[e], s5_lam_im[e], s5_log_dt[e],
                jnp.swapaxes(s5_b_re[e], 1, 2), jnp.swapaxes(s5_b_im[e], 1, 2), s5_c_im[e])
            bbd, cbd, lam = _block_diag_params(lbr, lbi, bbr, bbi, s5_c_re[e], cneg)
            shared = (win_ab, bbd, cbd, lam, d3, gluw, glub3, sg3, sb3)
            (xp, sre, sim), ffn_w = _even_prompt(xp, e, l, *shared, sgu_w, sbt, wout_ab, lmg3, lmb3,
                                                 cast_jobs=ffn_jobs(l))
            sw0 = jnp.repeat(sgu_w[e][:, 0, 0], HD_B).reshape(1, D_B)
            sb0 = jnp.repeat(sgu_b[e][:, 0], HD_B).reshape(1, D_B)
            xs, sres, sims, vn = _even_sample(
                xs, state_a_re[e].reshape(DEC_BATCH, -1), state_a_im[e].reshape(DEC_BATCH, -1),
                e, l, *shared, sw0, sb0, wout_ab, lmg3, lmb3)
            sa_re_p.append(sre.reshape(BATCH, N_GA, P_A))
            sa_im_p.append(sim.reshape(BATCH, N_GA, P_A))
            sa_re_s.append(sres.reshape(DEC_BATCH, N_GA, P_A))
            sa_im_s.append(sims.reshape(DEC_BATCH, N_GA, P_A))
            sb_v_s.append(vn.reshape(DEC_BATCH, 1, D_B))
        else:
            o = l // 2
            win_c, wout_c = odd_w
            rest = (win_c, conv_c_w, ccb3, lcg3, lcb3, wout_c, lmg3, lmb3)
            xp, cp = _odd_prompt(xp, o, l, *rest, tm=ODD_TM)
            xs, cs_t = _odd_sample(xs, cache_c_t, o, l, *rest)
            cc_p.append(jnp.transpose(cp.reshape(K_C - 1, BATCH, D_C), (1, 0, 2)))
            cc_s.append(jnp.transpose(cs_t, (1, 0, 2)))
        wg, wu, wd = ffn_w
        jobs = [(w_in_c, (l + 1) // 2), (w_out_c, (l + 1) // 2)] + ffn_jobs(l + 1) if l + 1 < DEPTH else []
        halo_s = jnp.transpose(cache_ffn_conv[l], (1, 0, 2)).reshape((K_F - 1) * DEC_BATCH, D_FF)
        (xp, fcp, xs, fcs), casts = _ffn(xp, xs, halo_s, l, wg, wu, ffn_conv_w, fcb3, wd, lfg3, lfb3, tm=FFN_TM,
                                         batch_major_out=(l == DEPTH - 1), cast_jobs=jobs)
        if jobs:
            odd_w, ffn_w = casts[:2], casts[2:]
        cf_p.append(jnp.transpose(fcp.reshape(K_F - 1, BATCH, D_FF), (1, 0, 2)))
        cf_s.append(jnp.transpose(fcs.reshape(K_F - 1, DEC_BATCH, D_FF), (1, 0, 2)))

    ys = xs.reshape(DEC_BATCH, 1, D_MODEL)
    return (xp, ys,
            jnp.stack(sa_re_p), jnp.stack(sa_im_p), jnp.stack(sa_re_s), jnp.stack(sa_im_s),
            jnp.stack(sb_v_s),
            jnp.stack(cc_p), jnp.stack(cc_s),
            jnp.stack(cf_p), jnp.stack(cf_s))
```
